```python
import math
import numpy as np
import jax
import jax.numpy as jnp
from jax import lax

D_MODEL = 2048
BATCH = 4
SEQ = 4096
DEPTH = 2

GRID_W = 64
HEAD_DIM = 128
ROPE_THETA = 500000.0
ROT_DIM = HEAD_DIM // 4
RMS_EPS = 1e-6
NEG_INF = -1e30

NA_HEADS = 8
NA_KH = 8
NA_KW = 16
NA_QB = 16
NA_KBW = NA_QB + NA_KW

DIL_GROUPS = ((128, 1), (512, 4), (2048, 16))
DIL_HEADS_PER_GROUP = 4
DIL_HEADS = DIL_HEADS_PER_GROUP * len(DIL_GROUPS)
DIL_OUT = DIL_HEADS_PER_GROUP * HEAD_DIM

MLA_HEADS = 8
MLA_NOPE = 128
MLA_ROPE = 64
MLA_V = 128
MLA_Q_RANK = 768
MLA_KV_RANK = 512
MLA_QB = 128

NA_W = NA_HEADS * HEAD_DIM
DIL_W = DIL_HEADS * HEAD_DIM
MLA_OUT = MLA_HEADS * MLA_V
IN_SPLITS = (NA_W, NA_W, NA_W, DIL_W, DIL_W, DIL_W, MLA_Q_RANK, MLA_KV_RANK, MLA_ROPE, D_MODEL, D_MODEL, D_MODEL)
IN_COLS = sum(IN_SPLITS)

DENSE_FF = 5632
N_EXPERTS = 8
TOP_K = 2
EXPERT_FF = 7168

kernel_name = 'hybrid_na_dilated_mla_moe_encoder'


def rms_norm(x, g):
    xf = x.astype(jnp.float32)
    y = xf * lax.rsqrt(jnp.mean(xf * xf, axis=-1, keepdims=True) + RMS_EPS)
    return (y * g.astype(jnp.float32)).astype(x.dtype)


def rope(x, pos):
    r = x.shape[-1]
    half = r // 2
    inv = ROPE_THETA ** (-jnp.arange(half, dtype=jnp.float32) * (2.0 / r))
    ang = pos.astype(jnp.float32)[:, None] * inv[None, :]
    cos = jnp.cos(ang)[None, :, None, :]
    sin = jnp.sin(ang)[None, :, None, :]
    xf = x.astype(jnp.float32)
    x1, x2 = xf[..., :half], xf[..., half:]
    return jnp.concatenate([x1 * cos - x2 * sin, x2 * cos + x1 * sin], axis=-1).astype(x.dtype)


def partial_rope(x, pos):
    return jnp.concatenate([rope(x[..., :ROT_DIM], pos), x[..., ROT_DIM:]], axis=-1)


def neighbourhood_attention(q, k, v, rpb):
    B, S, H, hd = q.shape
    rows = S // GRID_W
    kh = min(NA_KH, rows)
    n_cb = GRID_W // NA_QB
    scale = 1.0 / math.sqrt(hd)
    c0 = np.arange(n_cb) * NA_QB
    kstart = np.clip(c0 - NA_KW // 2, 0, GRID_W - NA_KBW)
    qcol = c0[:, None] + np.arange(NA_QB)[None, :]
    cstart = np.clip(qcol - NA_KW // 2, 0, GRID_W - NA_KW)
    kcol = kstart[:, None] + np.arange(NA_KBW)[None, :]
    kc = kcol[:, None, :]
    col_ok = (kc >= cstart[..., None]) & (kc < cstart[..., None] + NA_KW)
    dc_idx = np.clip(kc - qcol[..., None] + NA_KW - 1, 0, 2 * NA_KW - 2)
    col_bias = rpb.astype(jnp.float32)[:, :, dc_idx] + jnp.where(col_ok, 0.0, NEG_INF)[None, None]
    q_rows = q.reshape(B, rows, n_cb, NA_QB, H, hd).transpose(1, 0, 4, 2, 3, 5)
    kcb = k.reshape(B, rows, GRID_W, H, hd)[:, :, kcol].transpose(0, 4, 1, 2, 3, 5)
    vcb = v.reshape(B, rows, GRID_W, H, hd)[:, :, kcol].transpose(0, 4, 1, 2, 3, 5)

    def row_fn(args):
        q_row, r = args
        rs = jnp.clip(r - NA_KH // 2, 0, rows - kh)
        kb = lax.dynamic_slice_in_dim(kcb, rs, kh, axis=2)
        vb = lax.dynamic_slice_in_dim(vcb, rs, kh, axis=2)
        s = jnp.einsum('bhcqd,bhrckd->bhcqrk', q_row, kb, preferred_element_type=jnp.float32) * scale
        dr = rs + jnp.arange(kh) - r + NA_KH - 1
        bias = jnp.take(col_bias, dr, axis=1).transpose(0, 2, 3, 1, 4)
        s = s + bias[None]
        p = jax.nn.softmax(s.reshape(*s.shape[:4], kh * NA_KBW), axis=-1).reshape(s.shape)
        return jnp.einsum('bhcqrk,bhrckd->bhcqd', p.astype(vb.dtype), vb)

    o = lax.map(row_fn, (q_rows, jnp.arange(rows)))
    return o.transpose(1, 0, 3, 4, 2, 5).reshape(B, S, H * hd)


def dilated_attention(q, k, v):
    B, S, H, hd = q.shape
    hg = DIL_HEADS_PER_GROUP
    scale = 1.0 / math.sqrt(hd)
    outs, lses = [], []
    for g, (window, dil) in enumerate(DIL_GROUPS):
        n = window // (2 * dil)
        L = S // dil
        nb = -(-L // n)
        Lp = nb * n
        sl = slice(g * hg, (g + 1) * hg)

        def lattice(t):
            return t[:, :, sl].reshape(B, L, dil, hg, hd).transpose(0, 3, 2, 1, 4)

        qg = jnp.pad(lattice(q), ((0, 0), (0, 0), (0, 0), (0, Lp - L), (0, 0))).reshape(B, hg, dil, nb, n, hd)
        kv_pad = ((0, 0), (0, 0), (0, 0), (n, Lp - L + n), (0, 0))
        kp = jnp.pad(lattice(k), kv_pad).reshape(B, hg, dil, nb + 2, n, hd)
        vp = jnp.pad(lattice(v), kv_pad).reshape(B, hg, dil, nb + 2, n, hd)
        kb = jnp.concatenate([kp[:, :, :, :-2], kp[:, :, :, 1:-1], kp[:, :, :, 2:]], axis=4)
        vb = jnp.concatenate([vp[:, :, :, :-2], vp[:, :, :, 1:-1], vp[:, :, :, 2:]], axis=4)
        qi = np.arange(nb)[:, None, None] * n + np.arange(n)[None, :, None]
        ki = np.arange(nb)[:, None, None] * n - n + np.arange(3 * n)[None, None, :]
        ok = (np.abs(ki - qi) <= n) & (ki >= 0) & (ki < L)
        s = jnp.einsum('bhrnqd,bhrnkd->bhrnqk', qg, kb, preferred_element_type=jnp.float32) * scale
        s = s + jnp.where(ok, 0.0, NEG_INF)
        m = jnp.max(s, axis=-1, keepdims=True)
        e = jnp.exp(s - m)
        den = jnp.sum(e, axis=-1, keepdims=True)
        o = jnp.einsum('bhrnqk,bhrnkd->bhrnqd', e.astype(vb.dtype), vb, preferred_element_type=jnp.float32) / den
        lse = (m + jnp.log(den))[..., 0]
        o = o.reshape(B, hg, dil, Lp, hd)[:, :, :, :L].transpose(0, 3, 2, 1, 4).reshape(B, S, hg, hd)
        lse = lse.reshape(B, hg, dil, Lp)[..., :L].transpose(0, 3, 2, 1).reshape(B, S, hg)
        outs.append(o)
        lses.append(lse)
    w = jax.nn.softmax(jnp.stack(lses, axis=0), axis=0)
    o = jnp.sum(w[..., None] * jnp.stack(outs, axis=0), axis=0)
    return o.reshape(B, S, hg * hd).astype(q.dtype)


def mla(c_q, c_kv, k_r, q_a_norm, w_uq, kv_a_norm, w_ukv, q_norm, k_norm, pos):
    B, S, _ = c_q.shape
    H = MLA_HEADS
    dq = MLA_NOPE + MLA_ROPE
    q = (rms_norm(c_q, q_a_norm) @ w_uq).reshape(B, S, H, dq)
    kv = (rms_norm(c_kv, kv_a_norm) @ w_ukv).reshape(B, S, H, MLA_NOPE + MLA_V)
    k_nope, v = kv[..., :MLA_NOPE], kv[..., MLA_NOPE:]
    k = jnp.concatenate([k_nope, jnp.broadcast_to(k_r[:, :, None, :], (B, S, H, MLA_ROPE))], axis=-1)
    q = rms_norm(q, q_norm)
    k = rms_norm(k, k_norm)
    q = jnp.concatenate([q[..., :MLA_NOPE], rope(q[..., MLA_NOPE:], pos)], axis=-1).transpose(0, 2, 1, 3)
    k = jnp.concatenate([k[..., :MLA_NOPE], rope(k[..., MLA_NOPE:], pos)], axis=-1).transpose(0, 2, 1, 3)
    v = v.transpose(0, 2, 1, 3)
    nq = S // MLA_QB
    q_blocks = q.reshape(B, H, nq, MLA_QB, dq).transpose(2, 0, 1, 3, 4)
    scale = 1.0 / math.sqrt(dq)

    def block_fn(qb):
        s = jnp.einsum('bhqd,bhkd->bhqk', qb, k, preferred_element_type=jnp.float32) * scale
        p = jax.nn.softmax(s, axis=-1)
        return jnp.einsum('bhqk,bhkd->bhqd', p.astype(v.dtype), v)

    o = lax.map(block_fn, q_blocks)
    return o.transpose(1, 0, 3, 2, 4).reshape(B, S, H * MLA_V)


def token_mixer(h, pos, w_in, na_q_norm, na_k_norm, na_rpb, dil_q_norm, dil_k_norm,
                mla_q_a_norm, mla_w_uq, mla_kv_a_norm, mla_w_ukv, mla_q_norm, mla_k_norm,
                w_branch_na, w_branch_dil, w_branch_mla, w_out):
    B, S, _ = h.shape
    offs = np.cumsum(IN_SPLITS)[:-1].tolist()
    (na_q, na_k, na_v, dl_q, dl_k, dl_v, c_q, c_kv, k_r,
     g_na, g_dl, g_mla) = jnp.split(h @ w_in, offs, axis=-1)

    def heads(t, n_h):
        return t.reshape(B, S, n_h, HEAD_DIM)

    o_na = neighbourhood_attention(rms_norm(heads(na_q, NA_HEADS), na_q_norm),
                                   rms_norm(heads(na_k, NA_HEADS), na_k_norm),
                                   heads(na_v, NA_HEADS), na_rpb)
    o_dl = dilated_attention(partial_rope(rms_norm(heads(dl_q, DIL_HEADS), dil_q_norm), pos),
                             partial_rope(rms_norm(heads(dl_k, DIL_HEADS), dil_k_norm), pos),
                             heads(dl_v, DIL_HEADS))
    o_mla = mla(c_q, c_kv, k_r, mla_q_a_norm, mla_w_uq, mla_kv_a_norm, mla_w_ukv, mla_q_norm, mla_k_norm, pos)
    merged = (jax.nn.sigmoid(g_na) * (o_na @ w_branch_na)
              + jax.nn.sigmoid(g_dl) * (o_dl @ w_branch_dil)
              + jax.nn.sigmoid(g_mla) * (o_mla @ w_branch_mla))
    return merged @ w_out


def swiglu(h, w_gu, w_down):
    g, u = jnp.split(h @ w_gu, 2, axis=-1)
    return (jax.nn.silu(g) * u) @ w_down


def moe_swiglu(h, w_router, w_gu, w_down):
    logits = (h @ w_router).astype(jnp.float32)
    top_v, top_i = lax.top_k(logits, TOP_K)
    top_w = jax.nn.softmax(top_v, axis=-1)
    combine = jnp.sum(jax.nn.one_hot(top_i, N_EXPERTS, dtype=jnp.float32) * top_w[..., None], axis=-2)
    combine = combine.astype(h.dtype)
    y = combine[..., 0:1] * swiglu(h, w_gu[0], w_down[0])
    for e in range(1, N_EXPERTS):
        y = y + combine[..., e:e + 1] * swiglu(h, w_gu[e], w_down[e])
    return y


def setup_inputs(seed: int = 0) -> dict:
    key = jax.random.key(seed)
    ks = jax.random.split(key, 27)
    D = D_MODEL
    n_even = (DEPTH + 1) // 2
    n_odd = DEPTH // 2

    def nrm(k, shape, scale):
        return jax.random.normal(k, shape, jnp.float32) * scale

    def gain(k, shape):
        return 1.0 + 0.02 * jax.random.normal(k, shape, jnp.float32)

    return {
        'x': nrm(ks[0], (BATCH, SEQ, D), 1.0),
        'c': nrm(ks[1], (BATCH, D), 1.0),
        'w_ada': nrm(ks[2], (DEPTH, D, 6 * D), 0.2 * D ** -0.5),
        'b_ada': nrm(ks[3], (DEPTH, 6 * D), 0.02),
        'norm_mix': gain(ks[4], (DEPTH, D)),
        'norm_ffn': gain(ks[5], (DEPTH, D)),
        'w_in': nrm(ks[6], (DEPTH, D, IN_COLS), D ** -0.5),
        'na_q_norm': gain(ks[7], (DEPTH, HEAD_DIM)),
        'na_k_norm': gain(ks[8], (DEPTH, HEAD_DIM)),
        'na_rpb': nrm(ks[9], (DEPTH, NA_HEADS, 2 * NA_KH - 1, 2 * NA_KW - 1), 0.3),
        'dil_q_norm': gain(ks[10], (DEPTH, HEAD_DIM)),
        'dil_k_norm': gain(ks[11], (DEPTH, HEAD_DIM)),
        'mla_q_a_norm': gain(ks[12], (DEPTH, MLA_Q_RANK)),
        'mla_w_uq': nrm(ks[13], (DEPTH, MLA_Q_RANK, MLA_HEADS * (MLA_NOPE + MLA_ROPE)), MLA_Q_RANK ** -0.5),
        'mla_kv_a_norm': gain(ks[14], (DEPTH, MLA_KV_RANK)),
        'mla_w_ukv': nrm(ks[15], (DEPTH, MLA_KV_RANK, MLA_HEADS * (MLA_NOPE + MLA_V)), MLA_KV_RANK ** -0.5),
        'mla_q_norm': gain(ks[16], (DEPTH, MLA_NOPE + MLA_ROPE)),
        'mla_k_norm': gain(ks[17], (DEPTH, MLA_NOPE + MLA_ROPE)),
        'w_branch_na': nrm(ks[18], (DEPTH, NA_W, D), NA_W ** -0.5),
        'w_branch_dil': nrm(ks[19], (DEPTH, DIL_OUT, D), DIL_OUT ** -0.5),
        'w_branch_mla': nrm(ks[20], (DEPTH, MLA_OUT, D), MLA_OUT ** -0.5),
        'w_out': nrm(ks[21], (DEPTH, D, D), D ** -0.5),
        'ffn_w_gu': nrm(ks[22], (n_even, D, 2 * DENSE_FF), D ** -0.5),
        'ffn_w_down': nrm(ks[23], (n_even, DENSE_FF, D), DENSE_FF ** -0.5),
        'moe_router': nrm(ks[24], (n_odd, D, N_EXPERTS), D ** -0.5),
        'moe_w_gu': nrm(ks[25], (n_odd, N_EXPERTS, D, 2 * EXPERT_FF), D ** -0.5),
        'moe_w_down': nrm(ks[26], (n_odd, N_EXPERTS, EXPERT_FF, D), EXPERT_FF ** -0.5),
    }


def reference(x, c, w_ada, b_ada, norm_mix, norm_ffn, w_in, na_q_norm, na_k_norm, na_rpb,
              dil_q_norm, dil_k_norm, mla_q_a_norm, mla_w_uq, mla_kv_a_norm, mla_w_ukv,
              mla_q_norm, mla_k_norm, w_branch_na, w_branch_dil, w_branch_mla, w_out,
              ffn_w_gu, ffn_w_down, moe_router, moe_w_gu, moe_w_down):
    S = x.shape[1]
    pos = jnp.arange(S, dtype=jnp.int32)
    for l in range(DEPTH):
        mod = c @ w_ada[l] + b_ada[l]
        sh1, sc1, g1, sh2, sc2, g2 = jnp.split(mod[:, None, :], 6, axis=-1)
        h = rms_norm(x, norm_mix[l]) * (1.0 + sc1) + sh1
        y = token_mixer(h, pos, w_in[l], na_q_norm[l], na_k_norm[l], na_rpb[l],
                        dil_q_norm[l], dil_k_norm[l], mla_q_a_norm[l], mla_w_uq[l],
                        mla_kv_a_norm[l], mla_w_ukv[l], mla_q_norm[l], mla_k_norm[l],
                        w_branch_na[l], w_branch_dil[l], w_branch_mla[l], w_out[l])
        x = x + g1 * y
        h = rms_norm(x, norm_ffn[l]) * (1.0 + sc2) + sh2
        if l % 2 == 0:
            f = swiglu(h, ffn_w_gu[l // 2], ffn_w_down[l // 2])
        else:
            f = moe_swiglu(h, moe_router[l // 2], moe_w_gu[l // 2], moe_w_down[l // 2])
        x = x + g2 * f
    return x
```

```python
import functools
import math

import numpy as np
import jax
import jax.numpy as jnp
from jax import lax
from jax.experimental import pallas as pl
from jax.experimental.pallas import tpu as pltpu

GRID_W = 64
HEAD_DIM = 128
ROPE_THETA = 500000.0
ROT_DIM = HEAD_DIM // 4
RMS_EPS = 1e-6
NEG_INF = -1e30
NA_HEADS = 8
NA_KH = 8
NA_KW = 16
DIL_GROUPS = ((128, 1), (512, 4), (2048, 16))
DIL_HEADS_PER_GROUP = 4
DIL_HEADS = DIL_HEADS_PER_GROUP * len(DIL_GROUPS)
DIL_RADIUS = 64
MLA_HEADS = 8
MLA_NOPE = 128
MLA_ROPE = 64
MLA_V = 128
MLA_Q_RANK = 768
MLA_KV_RANK = 512
MLA_QK = MLA_NOPE + MLA_ROPE
MLA_QK_PAD = 256
N_EXPERTS = 8
LANES = 128

NA_W = NA_HEADS * HEAD_DIM
DIL_W = DIL_HEADS * HEAD_DIM
DIL_OUT = DIL_HEADS_PER_GROUP * HEAD_DIM

VMEM_LIMIT_BYTES = 48 * 1024 * 1024

F32 = jnp.float32
BF16 = jnp.bfloat16


def _params(*sem):
    return pltpu.CompilerParams(dimension_semantics=sem, vmem_limit_bytes=VMEM_LIMIT_BYTES)


def _dot(a, b):
    return jnp.dot(a, b, preferred_element_type=F32)


def _dot_nt(a, b):
    return lax.dot_general(a, b, (((1,), (1,)), ((), ())), preferred_element_type=F32)


def _adaln_kernel(c_ref, w_ref, b_ref, o_ref):
    o_ref[...] = _dot(c_ref[...], w_ref[...]) + b_ref[...]


def _adaln(c_pad, w_ada, b_ada):
    depth, d, n = w_ada.shape
    tn = 1024
    return pl.pallas_call(
        _adaln_kernel,
        grid=(depth, n // tn),
        in_specs=[pl.BlockSpec(c_pad.shape, lambda l, j: (0, 0)),
                  pl.BlockSpec((None, d, tn), lambda l, j: (l, 0, j)),
                  pl.BlockSpec((None, 1, tn), lambda l, j: (l, 0, j))],
        out_specs=pl.BlockSpec((None, c_pad.shape[0], tn), lambda l, j: (l, 0, j)),
        out_shape=jax.ShapeDtypeStruct((depth, c_pad.shape[0], n), F32),
        compiler_params=_params("parallel", "parallel"),
        name="adaln",
    )(c_pad, w_ada, b_ada.reshape(depth, 1, n))


def _modulated_norm(x, g, sc, sh):
    ms = jnp.mean(x * x, axis=-1, keepdims=True)
    return (x * lax.rsqrt(ms + RMS_EPS) * g) * (1.0 + sc) + sh


def _normmod_kernel(x_ref, g_ref, sc_ref, sh_ref, o_ref):
    o_ref[...] = _modulated_norm(x_ref[...], g_ref[...], sc_ref[...], sh_ref[...]).astype(o_ref.dtype)


def _normmod(x, g, sc, sh, seq, tm=512):
    t, d = x.shape
    per = seq // tm
    return pl.pallas_call(
        _normmod_kernel,
        grid=(t // tm,),
        in_specs=[pl.BlockSpec((tm, d), lambda i: (i, 0)),
                  pl.BlockSpec((1, d), lambda i: (0, 0)),
                  pl.BlockSpec((None, 1, d), lambda i: (i // per, 0, 0)),
                  pl.BlockSpec((None, 1, d), lambda i: (i // per, 0, 0))],
        out_specs=pl.BlockSpec((tm, d), lambda i: (i, 0)),
        out_shape=jax.ShapeDtypeStruct((t, d), BF16),
        compiler_params=_params("parallel"),
        name="normmod",
    )(x, g.reshape(1, d), sc, sh)


def _mm_kernel(*refs, nk, n_extra, epilogue):
    a_ref, b_ref = refs[0], refs[1]
    extras = refs[2:2 + n_extra]
    o_ref = refs[2 + n_extra]
    part = _dot(a_ref[...], b_ref[...])
    if nk == 1:
        epilogue(part, extras, o_ref)
        return
    acc_ref = refs[3 + n_extra]
    k = pl.program_id(2)

    @pl.when(k == 0)
    def _():
        acc_ref[...] = part

    @pl.when(k > 0)
    def _():
        acc_ref[...] += part

    @pl.when(k == nk - 1)
    def _():
        epilogue(acc_ref[...], extras, o_ref)


def _ep_store(acc, extras, o_ref):
    o_ref[...] = acc.astype(o_ref.dtype)


def _matmul(a, b, *, tm, tn, tk=None, out_dtype, epilogue=_ep_store, extras=(), name):
    m, kdim = a.shape
    n = b.shape[1]
    tk = kdim if tk is None else tk
    nk = kdim // tk
    assert m % tm == 0 and n % tn == 0 and kdim % tk == 0
    in_specs = [pl.BlockSpec((tm, tk), lambda i, j, k: (i, k)),
                pl.BlockSpec((tk, tn), lambda i, j, k: (k, j))]
    in_specs += [pl.BlockSpec(bs, im) for (_, bs, im) in extras]
    scratch = [pltpu.VMEM((tm, tn), F32)] if nk > 1 else []
    return pl.pallas_call(
        functools.partial(_mm_kernel, nk=nk, n_extra=len(extras), epilogue=epilogue),
        grid=(m // tm, n // tn, nk),
        in_specs=in_specs,
        out_specs=pl.BlockSpec((tm, tn), lambda i, j, k: (i, j)),
        out_shape=jax.ShapeDtypeStruct((m, n), out_dtype),
        scratch_shapes=scratch,
        compiler_params=_params("parallel", "parallel", "arbitrary"),
        name=name,
    )(a, b, *[e[0] for e in extras])


def _head_rms(blk, gain):
    ms = jnp.mean(blk * blk, axis=-1, keepdims=True)
    return blk * lax.rsqrt(ms + RMS_EPS) * gain


def _ep_headnorm(acc, extras, o_ref):
    g = extras[0][...]
    for c in range(acc.shape[1] // HEAD_DIM):
        sl = slice(c * HEAD_DIM, (c + 1) * HEAD_DIM)
        o_ref[:, sl] = _head_rms(acc[:, sl], g[:, sl]).astype(o_ref.dtype)


def _rotate_pairs(y, cos_t, sin_t, half):
    lane = lax.broadcasted_iota(jnp.int32, y.shape, 1)
    swapped = jnp.where(lane < half, pltpu.roll(y, LANES - half, 1), pltpu.roll(y, half, 1))
    return y * cos_t + swapped * sin_t


def _ep_headnorm_rope(acc, extras, o_ref):
    g = extras[0][...]
    cos_t = extras[1][...]
    sin_t = extras[2][...]
    for c in range(acc.shape[1] // HEAD_DIM):
        sl = slice(c * HEAD_DIM, (c + 1) * HEAD_DIM)
        y = _head_rms(acc[:, sl], g[:, sl])
        o_ref[:, sl] = _rotate_pairs(y, cos_t, sin_t, ROT_DIM // 2).astype(o_ref.dtype)


def _ep_rownorm(acc, extras, o_ref):
    o_ref[...] = _head_rms(acc, extras[0][...]).astype(o_ref.dtype)


def _ep_sigmoid(acc, extras, o_ref):
    o_ref[...] = jax.nn.sigmoid(acc).astype(o_ref.dtype)


def _ep_residual(acc, extras, o_ref):
    o_ref[...] = extras[0][...] + extras[1][...] * acc


def _rope_tables(seq, rot, fill):
    half = rot // 2
    inv = ROPE_THETA ** (-jnp.arange(half, dtype=F32) * (2.0 / rot))
    ang = jnp.arange(seq, dtype=jnp.int32).astype(F32)[:, None] * inv[None, :]
    cos, sin = jnp.cos(ang), jnp.sin(ang)
    pad = LANES - rot
    cos_t = jnp.concatenate([cos, cos, jnp.full((seq, pad), fill, F32)], axis=1)
    sin_t = jnp.concatenate([-sin, sin, jnp.zeros((seq, pad), F32)], axis=1)
    return cos_t, sin_t


def _na_bias_table(rpb, rows):
    h = rpb.shape[0]
    kh = min(NA_KH, rows)
    qc = np.arange(GRID_W)
    kc = np.arange(GRID_W)
    cstart = np.clip(qc - NA_KW // 2, 0, GRID_W - NA_KW)
    ok = (kc[None, :] >= cstart[:, None]) & (kc[None, :] < cstart[:, None] + NA_KW)
    dc = np.clip(kc[None, :] - qc[:, None] + NA_KW - 1, 0, 2 * NA_KW - 2)
    dr = np.arange(NA_KH)[:, None] + np.arange(kh)[None, :]
    tab = rpb.astype(F32)[:, dr[:, None, :, None], dc[None, :, None, :]]
    tab = tab + jnp.where(ok, 0.0, NEG_INF).astype(F32)[None, None, :, None, :]
    return tab.reshape(h, NA_KH, GRID_W, kh * GRID_W)


def _na_kernel(q_ref, k_ref, v_ref, b_ref, o_ref, *, rows, kh):
    scale = 1.0 / math.sqrt(HEAD_DIM)
    win = kh * GRID_W

    def body(r, carry):
        rs = jnp.clip(r - NA_KH // 2, 0, rows - kh)
        variant = rs - r + NA_KH - 1
        q0 = pl.multiple_of(r * GRID_W, GRID_W)
        k0 = pl.multiple_of(rs * GRID_W, GRID_W)
        q = q_ref[pl.ds(q0, GRID_W), :]
        k = k_ref[pl.ds(k0, win), :]
        v = v_ref[pl.ds(k0, win), :]
        s = _dot_nt(q, k) * scale + b_ref[variant]
        m = jnp.max(s, axis=-1, keepdims=True)
        p = jnp.exp(s - m)
        den = jnp.sum(p, axis=-1, keepdims=True)
        o = _dot(p.astype(BF16), v) / den
        o_ref[pl.ds(q0, GRID_W), :] = o.astype(o_ref.dtype)
        return carry

    lax.fori_loop(0, rows, body, 0)


def _na_attention(qk, v, bias, batch, seq):
    rows = seq // GRID_W
    kh = min(NA_KH, rows)
    t = qk.shape[0]
    blk = (seq, HEAD_DIM)
    return pl.pallas_call(
        functools.partial(_na_kernel, rows=rows, kh=kh),
        grid=(batch, NA_HEADS),
        in_specs=[pl.BlockSpec(blk, lambda b, h: (b, h)),
                  pl.BlockSpec(blk, lambda b, h: (b, NA_HEADS + h)),
                  pl.BlockSpec(blk, lambda b, h: (b, h)),
                  pl.BlockSpec((None,) + bias.shape[1:], lambda b, h: (h, 0, 0, 0))],
        out_specs=pl.BlockSpec(blk, lambda b, h: (b, h)),
        out_shape=jax.ShapeDtypeStruct((t, NA_W), BF16),
        compiler_params=_params("parallel", "parallel"),
        name="na_attention",
    )(qk, qk, v, bias)


DIL_QCHUNK = 2 * DIL_RADIUS
DIL_KWIN = 4 * DIL_RADIUS


def _dil_kernel(q_ref, k_ref, v_ref, o_ref, lse_ref, *, length):
    scale = 1.0 / math.sqrt(HEAD_DIM)

    def body(j, carry):
        q0 = pl.multiple_of(j * DIL_QCHUNK, DIL_QCHUNK)
        k0 = pl.multiple_of(jnp.clip(j * DIL_QCHUNK - DIL_RADIUS, 0, length - DIL_KWIN), DIL_RADIUS)
        q = q_ref[pl.ds(q0, DIL_QCHUNK), :]
        k = k_ref[pl.ds(k0, DIL_KWIN), :]
        v = v_ref[pl.ds(k0, DIL_KWIN), :]
        s = _dot_nt(q, k) * scale
        qi = q0 + lax.broadcasted_iota(jnp.int32, s.shape, 0)
        ki = k0 + lax.broadcasted_iota(jnp.int32, s.shape, 1)
        s = s + jnp.where(jnp.abs(ki - qi) <= DIL_RADIUS, 0.0, NEG_INF)
        m = jnp.max(s, axis=-1, keepdims=True)
        e = jnp.exp(s - m)
        den = jnp.sum(e, axis=-1, keepdims=True)
        o_ref[pl.ds(q0, DIL_QCHUNK), :] = _dot(e.astype(BF16), v) / den
        lse_ref[pl.ds(q0, DIL_QCHUNK), :] = m + jnp.log(den)
        return carry

    lax.fori_loop(0, length // DIL_QCHUNK, body, 0)


def _dil_group(q, k, v):
    streams, length, hd = q.shape
    assert length % DIL_QCHUNK == 0 and length >= DIL_KWIN
    spec = pl.BlockSpec((None, length, hd), lambda s: (s, 0, 0))
    return pl.pallas_call(
        functools.partial(_dil_kernel, length=length),
        grid=(streams,),
        in_specs=[spec, spec, spec],
        out_specs=[spec, pl.BlockSpec((None, length, 1), lambda s: (s, 0, 0))],
        out_shape=[jax.ShapeDtypeStruct((streams, length, hd), F32),
                   jax.ShapeDtypeStruct((streams, length, 1), F32)],
        compiler_params=_params("parallel"),
        name="dilated_attention",
    )(q, k, v)


def _dil_mix_kernel(o0_ref, o1_ref, o2_ref, l0_ref, l1_ref, l2_ref, o_ref):
    l0, l1, l2 = l0_ref[...], l1_ref[...], l2_ref[...]
    mx = jnp.maximum(jnp.maximum(l0, l1), l2)
    e0, e1, e2 = jnp.exp(l0 - mx), jnp.exp(l1 - mx), jnp.exp(l2 - mx)
    z = e0 + e1 + e2
    w0, w1, w2 = e0 / z, e1 / z, e2 / z
    for h in range(DIL_HEADS_PER_GROUP):
        sl = slice(h * HEAD_DIM, (h + 1) * HEAD_DIM)
        hs = slice(h, h + 1)
        o_ref[:, sl] = (w0[:, hs] * o0_ref[:, sl] + w1[:, hs] * o1_ref[:, sl]
                        + w2[:, hs] * o2_ref[:, sl]).astype(o_ref.dtype)


def _dil_mix(outs, lses, tm=512):
    t = outs[0].shape[0]
    ospec = pl.BlockSpec((tm, DIL_OUT), lambda i: (i, 0))
    lspec = pl.BlockSpec((tm, DIL_HEADS_PER_GROUP), lambda i: (i, 0))
    return pl.pallas_call(
        _dil_mix_kernel,
        grid=(t // tm,),
        in_specs=[ospec] * 3 + [lspec] * 3,
        out_specs=ospec,
        out_shape=jax.ShapeDtypeStruct((t, DIL_OUT), BF16),
        compiler_params=_params("parallel"),
        name="dilated_mix",
    )(*outs, *lses)


def _dilated_attention(qk, v, batch, seq):
    hg = DIL_HEADS_PER_GROUP
    outs, lses = [], []
    for g, (window, dil) in enumerate(DIL_GROUPS):
        assert window // (2 * dil) == DIL_RADIUS and seq % dil == 0
        length = seq // dil

        def lattice(t2d, col0):
            tt = t2d[:, col0:col0 + hg * HEAD_DIM].reshape(batch, length, dil, hg, HEAD_DIM)
            return tt.transpose(0, 3, 2, 1, 4).reshape(batch * hg * dil, length, HEAD_DIM)

        o, lse = _dil_group(lattice(qk, g * hg * HEAD_DIM),
                            lattice(qk, DIL_W + g * hg * HEAD_DIM),
                            lattice(v, g * hg * HEAD_DIM))
        o = o.reshape(batch, hg, dil, length, HEAD_DIM).transpose(0, 3, 2, 1, 4)
        lse = lse.reshape(batch, hg, dil, length).transpose(0, 3, 2, 1)
        outs.append(o.reshape(batch * seq, hg * HEAD_DIM))
        lses.append(lse.reshape(batch * seq, hg))
    return _dil_mix(outs, lses)


def _mla_q_kernel(a_ref, w_ref, g_ref, cos_ref, sin_ref, o_ref):
    acc = _dot(a_ref[...], w_ref[...])
    ms = jnp.sum(acc * acc, axis=-1, keepdims=True) * (1.0 / MLA_QK)
    y = acc * lax.rsqrt(ms + RMS_EPS) * g_ref[...]
    o_ref[:, :MLA_NOPE] = y[:, :MLA_NOPE].astype(o_ref.dtype)
    o_ref[:, MLA_NOPE:] = _rotate_pairs(y[:, MLA_NOPE:], cos_ref[...], sin_ref[...],
                                        MLA_ROPE // 2).astype(o_ref.dtype)


def _mla_q_proj(cq_n, w_uq_h, gain, cos_t, sin_t, seq, tm=512):
    t, rank = cq_n.shape
    per = seq // tm
    return pl.pallas_call(
        _mla_q_kernel,
        grid=(t // tm, MLA_HEADS),
        in_specs=[pl.BlockSpec((tm, rank), lambda i, h: (i, 0)),
                  pl.BlockSpec((None, rank, MLA_QK_PAD), lambda i, h: (h, 0, 0)),
                  pl.BlockSpec((1, MLA_QK_PAD), lambda i, h: (0, 0)),
                  pl.BlockSpec((tm, LANES), lambda i, h: (i % per, 0)),
                  pl.BlockSpec((tm, LANES), lambda i, h: (i % per, 0))],
        out_specs=pl.BlockSpec((None, tm, MLA_QK_PAD), lambda i, h: (h, i, 0)),
        out_shape=jax.ShapeDtypeStruct((MLA_HEADS, t, MLA_QK_PAD), BF16),
        compiler_params=_params("parallel", "parallel"),
        name="mla_q_proj",
    )(cq_n, w_uq_h, gain, cos_t, sin_t)


def _mla_kv_kernel(a_ref, w_ref, kr_ref, g0_ref, g1_ref, cos_ref, sin_ref, k_ref, v_ref):
    acc = _dot(a_ref[...], w_ref[...])
    kn = acc[:, :MLA_NOPE]
    kr = kr_ref[...]
    ms = (jnp.sum(kn * kn, axis=-1, keepdims=True)
          + jnp.sum(kr * kr, axis=-1, keepdims=True)) * (1.0 / MLA_QK)
    inv = lax.rsqrt(ms + RMS_EPS)
    k_ref[:, :MLA_NOPE] = (kn * inv * g0_ref[...]).astype(k_ref.dtype)
    k_ref[:, MLA_NOPE:] = _rotate_pairs(kr * inv * g1_ref[...], cos_ref[...], sin_ref[...],
                                        MLA_ROPE // 2).astype(k_ref.dtype)
    v_ref[...] = acc[:, MLA_NOPE:].astype(v_ref.dtype)


def _mla_kv_proj(ckv_n, w_ukv_h, k_r, g0, g1, cos_t, sin_t, seq, tm=512):
    t, rank = ckv_n.shape
    per = seq // tm
    return pl.pallas_call(
        _mla_kv_kernel,
        grid=(t // tm, MLA_HEADS),
        in_specs=[pl.BlockSpec((tm, rank), lambda i, h: (i, 0)),
                  pl.BlockSpec((None, rank, MLA_NOPE + MLA_V), lambda i, h: (h, 0, 0)),
                  pl.BlockSpec((tm, LANES), lambda i, h: (i, 0)),
                  pl.BlockSpec((1, LANES), lambda i, h: (0, 0)),
                  pl.BlockSpec((1, LANES), lambda i, h: (0, 0)),
                  pl.BlockSpec((tm, LANES), lambda i, h: (i % per, 0)),
                  pl.BlockSpec((tm, LANES), lambda i, h: (i % per, 0))],
        out_specs=[pl.BlockSpec((None, tm, MLA_QK_PAD), lambda i, h: (h, i, 0)),
                   pl.BlockSpec((None, tm, MLA_V), lambda i, h: (h, i, 0))],
        out_shape=[jax.ShapeDtypeStruct((MLA_HEADS, t, MLA_QK_PAD), BF16),
                   jax.ShapeDtypeStruct((MLA_HEADS, t, MLA_V), BF16)],
        compiler_params=_params("parallel", "parallel"),
        name="mla_kv_proj",
    )(ckv_n, w_ukv_h, k_r, g0, g1, cos_t, sin_t)


def _mla_attn_kernel(q_ref, k_ref, v_ref, o_ref, *, seq, tk):
    scale = 1.0 / math.sqrt(MLA_QK)
    q = q_ref[...]
    tq = q.shape[0]

    def body(c, carry):
        m, den, acc = carry
        k0 = pl.multiple_of(c * tk, tk)
        s = _dot_nt(q, k_ref[pl.ds(k0, tk), :]) * scale
        m_new = jnp.maximum(m, jnp.max(s, axis=-1, keepdims=True))
        alpha = jnp.exp(m - m_new)
        p = jnp.exp(s - m_new)
        den = alpha * den + jnp.sum(p, axis=-1, keepdims=True)
        acc = alpha * acc + _dot(p.astype(BF16), v_ref[pl.ds(k0, tk), :])
        return m_new, den, acc

    init = (jnp.full((tq, 1), NEG_INF, F32), jnp.zeros((tq, 1), F32), jnp.zeros((tq, MLA_V), F32))
    _, den, acc = lax.fori_loop(0, seq // tk, body, init)
    o_ref[...] = (acc / den).astype(o_ref.dtype)


def _mla_attention(q, k, v, batch, seq, tq=256, tk=512):
    t = q.shape[1]
    nq = seq // tq
    return pl.pallas_call(
        functools.partial(_mla_attn_kernel, seq=seq, tk=tk),
        grid=(batch, MLA_HEADS, nq),
        in_specs=[pl.BlockSpec((None, tq, MLA_QK_PAD), lambda b, h, i: (h, b * nq + i, 0)),
                  pl.BlockSpec((None, seq, MLA_QK_PAD), lambda b, h, i: (h, b, 0)),
                  pl.BlockSpec((None, seq, MLA_V), lambda b, h, i: (h, b, 0))],
        out_specs=pl.BlockSpec((tq, MLA_V), lambda b, h, i: (b * nq + i, h)),
        out_shape=jax.ShapeDtypeStruct((t, MLA_HEADS * MLA_V), BF16),
        compiler_params=_params("parallel", "parallel", "arbitrary"),
        name="mla_attention",
    )(q, k, v)


def _merge_kernel(ona_ref, odl_ref, omla_ref, wna_ref, wdl_ref, wmla_ref, g0_ref, g1_ref, g2_ref, o_ref):
    acc = g0_ref[...].astype(F32) * _dot(ona_ref[...], wna_ref[...])
    acc = acc + g1_ref[...].astype(F32) * _dot(odl_ref[...], wdl_ref[...])
    acc = acc + g2_ref[...].astype(F32) * _dot(omla_ref[...], wmla_ref[...])
    o_ref[...] = acc.astype(o_ref.dtype)


def _merge(o_na, o_dl, o_mla, w_na, w_dl, w_mla, gates, tm=512, tn=512):
    t = o_na.shape[0]
    d = w_na.shape[1]
    nj = d // tn

    def act(a):
        return pl.BlockSpec((tm, a.shape[1]), lambda i, j: (i, 0))

    def wgt(w):
        return pl.BlockSpec((w.shape[0], tn), lambda i, j: (0, j))

    def gate(idx):
        return pl.BlockSpec((tm, tn), lambda i, j: (i, idx * nj + j))

    return pl.pallas_call(
        _merge_kernel,
        grid=(t // tm, nj),
        in_specs=[act(o_na), act(o_dl), act(o_mla), wgt(w_na), wgt(w_dl), wgt(w_mla),
                  gate(0), gate(1), gate(2)],
        out_specs=pl.BlockSpec((tm, tn), lambda i, j: (i, j)),
        out_shape=jax.ShapeDtypeStruct((t, d), BF16),
        compiler_params=_params("parallel", "parallel"),
        name="branch_merge",
    )(o_na, o_dl, o_mla, w_na, w_dl, w_mla, gates, gates, gates)


def _gu_kernel(h_ref, wg_ref, wu_ref, o_ref):
    h = h_ref[...]
    g = _dot(h, wg_ref[...])
    u = _dot(h, wu_ref[...])
    o_ref[...] = (g * jax.nn.sigmoid(g) * u).astype(o_ref.dtype)


def _swiglu_up(h, w_gu, tm=512, tn=512):
    t, d = h.shape
    ff = w_gu.shape[1] // 2
    nj = ff // tn
    return pl.pallas_call(
        _gu_kernel,
        grid=(t // tm, nj),
        in_specs=[pl.BlockSpec((tm, d), lambda i, j: (i, 0)),
                  pl.BlockSpec((d, tn), lambda i, j: (0, j)),
                  pl.BlockSpec((d, tn), lambda i, j: (0, nj + j))],
        out_specs=pl.BlockSpec((tm, tn), lambda i, j: (i, j)),
        out_shape=jax.ShapeDtypeStruct((t, ff), BF16),
        compiler_params=_params("parallel", "parallel"),
        name="swiglu_up",
    )(h, w_gu, w_gu)


def _moe_gu_kernel(h_ref, wg_ref, wu_ref, cw_ref, o_ref):
    h = h_ref[...]
    g = _dot(h, wg_ref[...])
    u = _dot(h, wu_ref[...])
    o_ref[...] = (cw_ref[...] * (g * jax.nn.sigmoid(g) * u)).astype(o_ref.dtype)


def _moe_up(h, w_gu, comb_t, tm=1024, tn=512):
    t, d = h.shape
    n_e = w_gu.shape[0]
    ff = w_gu.shape[2] // 2
    nj = ff // tn
    tm = min(tm, t)
    return pl.pallas_call(
        _moe_gu_kernel,
        grid=(t // tm, n_e, nj),
        in_specs=[pl.BlockSpec((tm, d), lambda i, e, j: (i, 0)),
                  pl.BlockSpec((None, d, tn), lambda i, e, j: (e, 0, j)),
                  pl.BlockSpec((None, d, tn), lambda i, e, j: (e, 0, nj + j)),
                  pl.BlockSpec((None, tm, 1), lambda i, e, j: (e, i, 0))],
        out_specs=pl.BlockSpec((tm, tn), lambda i, e, j: (i, e * nj + j)),
        out_shape=jax.ShapeDtypeStruct((t, n_e * ff), BF16),
        compiler_params=_params("parallel", "parallel", "parallel"),
        name="moe_up",
    )(h, w_gu, w_gu, comb_t)


def _router_kernel(x_ref, g_ref, sc_ref, sh_ref, w_ref, o_ref):
    h = _modulated_norm(x_ref[...], g_ref[...], sc_ref[...], sh_ref[...])
    logits = jnp.dot(h, w_ref[...], preferred_element_type=F32, precision=lax.Precision.HIGHEST)
    lane = lax.broadcasted_iota(jnp.int32, logits.shape, 1).astype(F32)
    lg = jnp.where(lane < N_EXPERTS, logits, NEG_INF)
    m1 = jnp.max(lg, axis=-1, keepdims=True)
    i1 = jnp.min(jnp.where(lg == m1, lane, float(LANES)), axis=-1, keepdims=True)
    lg2 = jnp.where(lane == i1, NEG_INF, lg)
    m2 = jnp.max(lg2, axis=-1, keepdims=True)
    i2 = jnp.min(jnp.where(lg2 == m2, lane, float(LANES)), axis=-1, keepdims=True)
    e2 = jnp.exp(m2 - m1)
    z = 1.0 + e2
    o_ref[...] = jnp.where(lane == i1, 1.0 / z, 0.0) + jnp.where(lane == i2, e2 / z, 0.0)


def _router(x, g, sc, sh, w_router_pad, seq, tm=512):
    t, d = x.shape
    per = seq // tm
    return pl.pallas_call(
        _router_kernel,
        grid=(t // tm,),
        in_specs=[pl.BlockSpec((tm, d), lambda i: (i, 0)),
                  pl.BlockSpec((1, d), lambda i: (0, 0)),
                  pl.BlockSpec((None, 1, d), lambda i: (i // per, 0, 0)),
                  pl.BlockSpec((None, 1, d), lambda i: (i // per, 0, 0)),
                  pl.BlockSpec((d, LANES), lambda i: (0, 0))],
        out_specs=pl.BlockSpec((tm, LANES), lambda i: (i, 0)),
        out_shape=jax.ShapeDtypeStruct((t, LANES), F32),
        compiler_params=_params("parallel"),
        name="router",
    )(x, g.reshape(1, d), sc, sh, w_router_pad)


def _residual_matmul(a, w, x, gate, seq, name, tm=512, tn=512, tk=None):
    per = seq // tm
    return _matmul(a, w, tm=tm, tn=tn, tk=tk, out_dtype=F32, epilogue=_ep_residual,
                   extras=[(x, (tm, tn), lambda i, j, k: (i, j)),
                           (gate, (None, 1, tn), lambda i, j, k: (i // per, 0, j))],
                   name=name)


def _token_mixer(h, x, gate, lw, batch, seq, tables):
    d = h.shape[1]
    w_in = lw["w_in"]
    tm = 512
    per = seq // tm
    cos_d, sin_d, cos_m, sin_m = tables

    def cols(lo, hi):
        return w_in[:, lo:hi].astype(BF16)

    def tile_gain(gq, gk, heads):
        return jnp.concatenate([jnp.tile(gq, heads), jnp.tile(gk, heads)]).reshape(1, -1).astype(F32)

    o = 0
    na_qk = _matmul(h, cols(o, o + 2 * NA_W), tm=tm, tn=512, out_dtype=BF16, epilogue=_ep_headnorm,
                    extras=[(tile_gain(lw["na_q_norm"], lw["na_k_norm"], NA_HEADS), (1, 512),
                             lambda i, j, k: (0, j))], name="na_qk_proj")
    o += 2 * NA_W
    na_v = _matmul(h, cols(o, o + NA_W), tm=tm, tn=512, out_dtype=BF16, name="na_v_proj")
    o += NA_W
    dl_qk = _matmul(h, cols(o, o + 2 * DIL_W), tm=tm, tn=512, out_dtype=BF16, epilogue=_ep_headnorm_rope,
                    extras=[(tile_gain(lw["dil_q_norm"], lw["dil_k_norm"], DIL_HEADS), (1, 512),
                             lambda i, j, k: (0, j)),
                            (cos_d, (tm, LANES), lambda i, j, k: (i % per, 0)),
                            (sin_d, (tm, LANES), lambda i, j, k: (i % per, 0))], name="dil_qk_proj")
    o += 2 * DIL_W
    dl_v = _matmul(h, cols(o, o + DIL_W), tm=tm, tn=512, out_dtype=BF16, name="dil_v_proj")
    o += DIL_W
    cq_n = _matmul(h, cols(o, o + MLA_Q_RANK), tm=tm, tn=MLA_Q_RANK, out_dtype=BF16, epilogue=_ep_rownorm,
                   extras=[(lw["mla_q_a_norm"].reshape(1, -1), (1, MLA_Q_RANK), lambda i, j, k: (0, 0))],
                   name="mla_cq_proj")
    o += MLA_Q_RANK
    ckv_n = _matmul(h, cols(o, o + MLA_KV_RANK), tm=tm, tn=MLA_KV_RANK, out_dtype=BF16, epilogue=_ep_rownorm,
                    extras=[(lw["mla_kv_a_norm"].reshape(1, -1), (1, MLA_KV_RANK), lambda i, j, k: (0, 0))],
                    name="mla_ckv_proj")
    o += MLA_KV_RANK
    w_kr = jnp.pad(w_in[:, o:o + MLA_ROPE], ((0, 0), (0, LANES - MLA_ROPE))).astype(BF16)
    k_r = _matmul(h, w_kr, tm=tm, tn=LANES, out_dtype=F32, name="mla_kr_proj")
    o += MLA_ROPE
    gates = _matmul(h, cols(o, o + 3 * d), tm=tm, tn=512, out_dtype=BF16, epilogue=_ep_sigmoid,
                    name="gate_proj")

    bias = _na_bias_table(lw["na_rpb"], seq // GRID_W)
    o_na = _na_attention(na_qk, na_v, bias, batch, seq)
    o_dl = _dilated_attention(dl_qk, dl_v, batch, seq)
    w_uq = lw["mla_w_uq"].reshape(MLA_Q_RANK, MLA_HEADS, MLA_QK).transpose(1, 0, 2)
    w_uq = jnp.pad(w_uq, ((0, 0), (0, 0), (0, MLA_QK_PAD - MLA_QK))).astype(BF16)
    w_ukv = lw["mla_w_ukv"].reshape(MLA_KV_RANK, MLA_HEADS, MLA_NOPE + MLA_V).transpose(1, 0, 2).astype(BF16)
    gq = jnp.pad(lw["mla_q_norm"], (0, MLA_QK_PAD - MLA_QK)).reshape(1, MLA_QK_PAD)
    gk = lw["mla_k_norm"]
    gk0 = gk[:MLA_NOPE].reshape(1, LANES)
    gk1 = jnp.pad(gk[MLA_NOPE:], (0, LANES - MLA_ROPE)).reshape(1, LANES)
    q_m = _mla_q_proj(cq_n, w_uq, gq, cos_m, sin_m, seq)
    k_m, v_m = _mla_kv_proj(ckv_n, w_ukv, k_r, gk0, gk1, cos_m, sin_m, seq)
    o_mla = _mla_attention(q_m, k_m, v_m, batch, seq)

    merged = _merge(o_na, o_dl, o_mla, lw["w_branch_na"].astype(BF16), lw["w_branch_dil"].astype(BF16),
                    lw["w_branch_mla"].astype(BF16), gates)
    return _residual_matmul(merged, lw["w_out"].astype(BF16), x, gate, seq, "out_proj")


def kernel(x, c, w_ada, b_ada, norm_mix, norm_ffn, w_in, na_q_norm, na_k_norm, na_rpb, dil_q_norm, dil_k_norm, mla_q_a_norm, mla_w_uq, mla_kv_a_norm, mla_w_ukv, mla_q_norm, mla_k_norm, w_branch_na, w_branch_dil, w_branch_mla, w_out, ffn_w_gu, ffn_w_down, moe_router, moe_w_gu, moe_w_down):
    batch, seq, d = x.shape
    depth = w_ada.shape[0]
    assert seq % 512 == 0 and seq % GRID_W == 0
    t = batch * seq
    xf = x.reshape(t, d)

    c_pad = jnp.pad(c, ((0, (-batch) % 8), (0, 0)))
    mod = _adaln(c_pad, w_ada, b_ada)[:, :batch].reshape(depth, batch, 6, 1, d)
    tables = _rope_tables(seq, ROT_DIM, 1.0) + _rope_tables(seq, MLA_ROPE, 1.0)

    for l in range(depth):
        sh1, sc1, g1, sh2, sc2, g2 = [mod[l, :, i] for i in range(6)]
        lw = dict(w_in=w_in[l], na_q_norm=na_q_norm[l], na_k_norm=na_k_norm[l], na_rpb=na_rpb[l],
                  dil_q_norm=dil_q_norm[l], dil_k_norm=dil_k_norm[l], mla_q_a_norm=mla_q_a_norm[l],
                  mla_w_uq=mla_w_uq[l], mla_kv_a_norm=mla_kv_a_norm[l], mla_w_ukv=mla_w_ukv[l],
                  mla_q_norm=mla_q_norm[l], mla_k_norm=mla_k_norm[l], w_branch_na=w_branch_na[l],
                  w_branch_dil=w_branch_dil[l], w_branch_mla=w_branch_mla[l], w_out=w_out[l])
        h = _normmod(xf, norm_mix[l], sc1, sh1, seq)
        xf = _token_mixer(h, xf, g1, lw, batch, seq, tables)

        h = _normmod(xf, norm_ffn[l], sc2, sh2, seq)
        if l % 2 == 0:
            act = _swiglu_up(h, ffn_w_gu[l // 2].astype(BF16))
            w_down = ffn_w_down[l // 2].astype(BF16)
        else:
            w_r = jnp.pad(moe_router[l // 2], ((0, 0), (0, LANES - N_EXPERTS)))
            comb = _router(xf, norm_ffn[l], sc2, sh2, w_r, seq)
            comb_t = comb[:, :N_EXPERTS].T[:, :, None]
            act = _moe_up(h, moe_w_gu[l // 2].astype(BF16), comb_t)
            w_down = moe_w_down[l // 2].astype(BF16).reshape(-1, d)
        tk = 512
        xf = _residual_matmul(act, w_down, xf, g2, seq, "down_proj", tk=tk)
    return xf.reshape(batch, seq, d)
```

```python
import functools
import math

import numpy as np
import jax
import jax.numpy as jnp
from jax import lax
from jax.experimental import pallas as pl
from jax.experimental.pallas import tpu as pltpu

GRID_W = 64
HEAD_DIM = 128
ROPE_THETA = 500000.0
ROT_DIM = HEAD_DIM // 4
RMS_EPS = 1e-6
NEG_INF = -1e30
NA_HEADS = 8
NA_KH = 8
NA_KW = 16
DIL_GROUPS = ((128, 1), (512, 4), (2048, 16))
DIL_HEADS_PER_GROUP = 4
DIL_HEADS = DIL_HEADS_PER_GROUP * len(DIL_GROUPS)
DIL_RADIUS = 64
MLA_HEADS = 8
MLA_NOPE = 128
MLA_ROPE = 64
MLA_V = 128
MLA_Q_RANK = 768
MLA_KV_RANK = 512
MLA_QK = MLA_NOPE + MLA_ROPE
MLA_QK_PAD = 256
N_EXPERTS = 8
LANES = 128

NA_W = NA_HEADS * HEAD_DIM
DIL_W = DIL_HEADS * HEAD_DIM
DIL_OUT = DIL_HEADS_PER_GROUP * HEAD_DIM

VMEM_LIMIT_BYTES = 48 * 1024 * 1024

F32 = jnp.float32
BF16 = jnp.bfloat16


def _params(*sem):
    return pltpu.CompilerParams(dimension_semantics=sem, vmem_limit_bytes=VMEM_LIMIT_BYTES)


def _dot(a, b):
    return jnp.dot(a, b, preferred_element_type=F32)


def _dot_nt(a, b):
    return lax.dot_general(a, b, (((1,), (1,)), ((), ())), preferred_element_type=F32)


def _adaln_kernel(c_ref, w_ref, b_ref, o_ref):
    o_ref[...] = _dot(c_ref[...], w_ref[...]) + b_ref[...]


def _adaln(c_pad, w_ada, b_ada):
    depth, d, n = w_ada.shape
    tn = 1024
    return pl.pallas_call(
        _adaln_kernel,
        grid=(depth, n // tn),
        in_specs=[pl.BlockSpec(c_pad.shape, lambda l, j: (0, 0)),
                  pl.BlockSpec((None, d, tn), lambda l, j: (l, 0, j)),
                  pl.BlockSpec((None, 1, tn), lambda l, j: (l, 0, j))],
        out_specs=pl.BlockSpec((None, c_pad.shape[0], tn), lambda l, j: (l, 0, j)),
        out_shape=jax.ShapeDtypeStruct((depth, c_pad.shape[0], n), F32),
        compiler_params=_params("parallel", "parallel"),
        name="adaln",
    )(c_pad, w_ada, b_ada.reshape(depth, 1, n))


def _modulated_norm(x, g, sc, sh):
    ms = jnp.mean(x * x, axis=-1, keepdims=True)
    return (x * lax.rsqrt(ms + RMS_EPS) * g) * (1.0 + sc) + sh


def _normmod_kernel(x_ref, g_ref, sc_ref, sh_ref, o_ref):
    o_ref[...] = _modulated_norm(x_ref[...], g_ref[...], sc_ref[...], sh_ref[...]).astype(o_ref.dtype)


def _normmod(x, g, sc, sh, seq, tm=512):
    t, d = x.shape
    per = seq // tm
    return pl.pallas_call(
        _normmod_kernel,
        grid=(t // tm,),
        in_specs=[pl.BlockSpec((tm, d), lambda i: (i, 0)),
                  pl.BlockSpec((1, d), lambda i: (0, 0)),
                  pl.BlockSpec((None, 1, d), lambda i: (i // per, 0, 0)),
                  pl.BlockSpec((None, 1, d), lambda i: (i // per, 0, 0))],
        out_specs=pl.BlockSpec((tm, d), lambda i: (i, 0)),
        out_shape=jax.ShapeDtypeStruct((t, d), BF16),
        compiler_params=_params("parallel"),
        name="normmod",
    )(x, g.reshape(1, d), sc, sh)


def _mm_kernel(*refs, nk, n_extra, epilogue):
    a_ref, b_ref = refs[0], refs[1]
    extras = refs[2:2 + n_extra]
    o_ref = refs[2 + n_extra]
    part = _dot(a_ref[...], b_ref[...])
    if nk == 1:
        epilogue(part, extras, o_ref)
        return
    acc_ref = refs[3 + n_extra]
    k = pl.program_id(2)

    @pl.when(k == 0)
    def _():
        acc_ref[...] = part

    @pl.when(k > 0)
    def _():
        acc_ref[...] += part

    @pl.when(k == nk - 1)
    def _():
        epilogue(acc_ref[...], extras, o_ref)


def _ep_store(acc, extras, o_ref):
    o_ref[...] = acc.astype(o_ref.dtype)


def _matmul(a, b, *, tm, tn, tk=None, out_dtype, epilogue=_ep_store, extras=(), name):
    m, kdim = a.shape
    n = b.shape[1]
    tk = kdim if tk is None else tk
    nk = kdim // tk
    assert m % tm == 0 and n % tn == 0 and kdim % tk == 0
    in_specs = [pl.BlockSpec((tm, tk), lambda i, j, k: (i, k)),
                pl.BlockSpec((tk, tn), lambda i, j, k: (k, j))]
    in_specs += [pl.BlockSpec(bs, im) for (_, bs, im) in extras]
    scratch = [pltpu.VMEM((tm, tn), F32)] if nk > 1 else []
    return pl.pallas_call(
        functools.partial(_mm_kernel, nk=nk, n_extra=len(extras), epilogue=epilogue),
        grid=(m // tm, n // tn, nk),
        in_specs=in_specs,
        out_specs=pl.BlockSpec((tm, tn), lambda i, j, k: (i, j)),
        out_shape=jax.ShapeDtypeStruct((m, n), out_dtype),
        scratch_shapes=scratch,
        compiler_params=_params("parallel", "parallel", "arbitrary"),
        name=name,
    )(a, b, *[e[0] for e in extras])


def _head_rms(blk, gain):
    ms = jnp.mean(blk * blk, axis=-1, keepdims=True)
    return blk * lax.rsqrt(ms + RMS_EPS) * gain


def _ep_headnorm(acc, extras, o_ref):
    g = extras[0][...]
    for c in range(acc.shape[1] // HEAD_DIM):
        sl = slice(c * HEAD_DIM, (c + 1) * HEAD_DIM)
        o_ref[:, sl] = _head_rms(acc[:, sl], g[:, sl]).astype(o_ref.dtype)


def _rotate_pairs(y, cos_t, sin_t, half):
    lane = lax.broadcasted_iota(jnp.int32, y.shape, 1)
    swapped = jnp.where(lane < half, pltpu.roll(y, LANES - half, 1), pltpu.roll(y, half, 1))
    return y * cos_t + swapped * sin_t


def _ep_headnorm_rope(acc, extras, o_ref):
    g = extras[0][...]
    cos_t = extras[1][...]
    sin_t = extras[2][...]
    for c in range(acc.shape[1] // HEAD_DIM):
        sl = slice(c * HEAD_DIM, (c + 1) * HEAD_DIM)
        y = _head_rms(acc[:, sl], g[:, sl])
        o_ref[:, sl] = _rotate_pairs(y, cos_t, sin_t, ROT_DIM // 2).astype(o_ref.dtype)


def _ep_rownorm(acc, extras, o_ref):
    o_ref[...] = _head_rms(acc, extras[0][...]).astype(o_ref.dtype)


def _ep_sigmoid(acc, extras, o_ref):
    o_ref[...] = jax.nn.sigmoid(acc).astype(o_ref.dtype)


def _ep_residual(acc, extras, o_ref):
    o_ref[...] = extras[0][...] + extras[1][...] * acc


def _rope_tables(seq, rot, fill):
    half = rot // 2
    inv = ROPE_THETA ** (-jnp.arange(half, dtype=F32) * (2.0 / rot))
    ang = jnp.arange(seq, dtype=jnp.int32).astype(F32)[:, None] * inv[None, :]
    cos, sin = jnp.cos(ang), jnp.sin(ang)
    pad = LANES - rot
    cos_t = jnp.concatenate([cos, cos, jnp.full((seq, pad), fill, F32)], axis=1)
    sin_t = jnp.concatenate([-sin, sin, jnp.zeros((seq, pad), F32)], axis=1)
    return cos_t, sin_t


def _na_bias_table(rpb, rows):
    h = rpb.shape[0]
    kh = min(NA_KH, rows)
    qc = np.arange(GRID_W)
    kc = np.arange(GRID_W)
    cstart = np.clip(qc - NA_KW // 2, 0, GRID_W - NA_KW)
    ok = (kc[None, :] >= cstart[:, None]) & (kc[None, :] < cstart[:, None] + NA_KW)
    dc = np.clip(kc[None, :] - qc[:, None] + NA_KW - 1, 0, 2 * NA_KW - 2)
    by_col = rpb.astype(F32)[:, :, dc] + jnp.where(ok, 0.0, NEG_INF).astype(F32)
    tab = jnp.stack([by_col[:, v:v + kh] for v in range(NA_KH)], axis=1)
    return tab.transpose(0, 1, 3, 2, 4).reshape(h, NA_KH, GRID_W, kh * GRID_W)


def _na_kernel(q_ref, k_ref, v_ref, b_ref, o_ref, *, rows, kh):
    scale = 1.0 / math.sqrt(HEAD_DIM)
    win = kh * GRID_W

    def body(r, carry):
        rs = jnp.clip(r - NA_KH // 2, 0, rows - kh)
        variant = rs - r + NA_KH - 1
        q0 = pl.multiple_of(r * GRID_W, GRID_W)
        k0 = pl.multiple_of(rs * GRID_W, GRID_W)
        q = q_ref[pl.ds(q0, GRID_W), :]
        k = k_ref[pl.ds(k0, win), :]
        v = v_ref[pl.ds(k0, win), :]
        s = _dot_nt(q, k) * scale + b_ref[variant]
        m = jnp.max(s, axis=-1, keepdims=True)
        p = jnp.exp(s - m)
        den = jnp.sum(p, axis=-1, keepdims=True)
        o = _dot(p.astype(BF16), v) / den
        o_ref[pl.ds(q0, GRID_W), :] = o.astype(o_ref.dtype)
        return carry

    lax.fori_loop(0, rows, body, 0, unroll=4)


def _na_attention(qk, v, bias, batch, seq):
    rows = seq // GRID_W
    kh = min(NA_KH, rows)
    t = qk.shape[0]
    blk = (seq, HEAD_DIM)
    return pl.pallas_call(
        functools.partial(_na_kernel, rows=rows, kh=kh),
        grid=(batch, NA_HEADS),
        in_specs=[pl.BlockSpec(blk, lambda b, h: (b, h)),
                  pl.BlockSpec(blk, lambda b, h: (b, NA_HEADS + h)),
                  pl.BlockSpec(blk, lambda b, h: (b, h)),
                  pl.BlockSpec((None,) + bias.shape[1:], lambda b, h: (h, 0, 0, 0))],
        out_specs=pl.BlockSpec(blk, lambda b, h: (b, h)),
        out_shape=jax.ShapeDtypeStruct((t, NA_W), BF16),
        compiler_params=_params("parallel", "parallel"),
        name="na_attention",
    )(qk, qk, v, bias)


DIL_QCHUNK = 2 * DIL_RADIUS
DIL_KWIN = 4 * DIL_RADIUS


def _dil_kernel(q_ref, k_ref, v_ref, o_ref, lse_ref, *, length):
    scale = 1.0 / math.sqrt(HEAD_DIM)

    def body(j, carry):
        q0 = pl.multiple_of(j * DIL_QCHUNK, DIL_QCHUNK)
        k0 = pl.multiple_of(jnp.clip(j * DIL_QCHUNK - DIL_RADIUS, 0, length - DIL_KWIN), DIL_RADIUS)
        q = q_ref[pl.ds(q0, DIL_QCHUNK), :]
        k = k_ref[pl.ds(k0, DIL_KWIN), :]
        v = v_ref[pl.ds(k0, DIL_KWIN), :]
        s = _dot_nt(q, k) * scale
        qi = q0 + lax.broadcasted_iota(jnp.int32, s.shape, 0)
        ki = k0 + lax.broadcasted_iota(jnp.int32, s.shape, 1)
        s = s + jnp.where(jnp.abs(ki - qi) <= DIL_RADIUS, 0.0, NEG_INF)
        m = jnp.max(s, axis=-1, keepdims=True)
        e = jnp.exp(s - m)
        den = jnp.sum(e, axis=-1, keepdims=True)
        o_ref[pl.ds(q0, DIL_QCHUNK), :] = _dot(e.astype(BF16), v) / den
        lse_ref[pl.ds(q0, DIL_QCHUNK), :] = m + jnp.log(den)
        return carry

    lax.fori_loop(0, length // DIL_QCHUNK, body, 0, unroll=2)


def _dil_group(q, k, v):
    streams, length, hd = q.shape
    assert length % DIL_QCHUNK == 0 and length >= DIL_KWIN
    spec = pl.BlockSpec((None, length, hd), lambda s: (s, 0, 0))
    return pl.pallas_call(
        functools.partial(_dil_kernel, length=length),
        grid=(streams,),
        in_specs=[spec, spec, spec],
        out_specs=[spec, pl.BlockSpec((None, length, 1), lambda s: (s, 0, 0))],
        out_shape=[jax.ShapeDtypeStruct((streams, length, hd), F32),
                   jax.ShapeDtypeStruct((streams, length, 1), F32)],
        compiler_params=_params("parallel"),
        name="dilated_attention",
    )(q, k, v)


def _dil_mix_kernel(o0_ref, o1_ref, o2_ref, l0_ref, l1_ref, l2_ref, o_ref):
    l0, l1, l2 = l0_ref[...], l1_ref[...], l2_ref[...]
    mx = jnp.maximum(jnp.maximum(l0, l1), l2)
    e0, e1, e2 = jnp.exp(l0 - mx), jnp.exp(l1 - mx), jnp.exp(l2 - mx)
    z = e0 + e1 + e2
    w0, w1, w2 = e0 / z, e1 / z, e2 / z
    for h in range(DIL_HEADS_PER_GROUP):
        sl = slice(h * HEAD_DIM, (h + 1) * HEAD_DIM)
        hs = slice(h, h + 1)
        o_ref[:, sl] = (w0[:, hs] * o0_ref[:, sl] + w1[:, hs] * o1_ref[:, sl]
                        + w2[:, hs] * o2_ref[:, sl]).astype(o_ref.dtype)


def _dil_mix(outs, lses, tm=512):
    t = outs[0].shape[0]
    ospec = pl.BlockSpec((tm, DIL_OUT), lambda i: (i, 0))
    lspec = pl.BlockSpec((tm, DIL_HEADS_PER_GROUP), lambda i: (i, 0))
    return pl.pallas_call(
        _dil_mix_kernel,
        grid=(t // tm,),
        in_specs=[ospec] * 3 + [lspec] * 3,
        out_specs=ospec,
        out_shape=jax.ShapeDtypeStruct((t, DIL_OUT), BF16),
        compiler_params=_params("parallel"),
        name="dilated_mix",
    )(*outs, *lses)


def _dilated_attention(qk, v, batch, seq):
    hg = DIL_HEADS_PER_GROUP
    outs, lses = [], []
    for g, (window, dil) in enumerate(DIL_GROUPS):
        assert window // (2 * dil) == DIL_RADIUS and seq % dil == 0
        length = seq // dil

        def lattice(t2d, col0):
            tt = t2d[:, col0:col0 + hg * HEAD_DIM].reshape(batch, length, dil, hg, HEAD_DIM)
            return tt.transpose(0, 3, 2, 1, 4).reshape(batch * hg * dil, length, HEAD_DIM)

        o, lse = _dil_group(lattice(qk, g * hg * HEAD_DIM),
                            lattice(qk, DIL_W + g * hg * HEAD_DIM),
                            lattice(v, g * hg * HEAD_DIM))
        o = o.reshape(batch, hg, dil, length, HEAD_DIM).transpose(0, 3, 2, 1, 4)
        lse = lse.reshape(batch, hg, dil, length).transpose(0, 3, 2, 1)
        outs.append(o.reshape(batch * seq, hg * HEAD_DIM))
        lses.append(lse.reshape(batch * seq, hg))
    return _dil_mix(outs, lses)


def _mla_q_kernel(a_ref, w_ref, g_ref, cos_ref, sin_ref, o_ref):
    acc = _dot(a_ref[...], w_ref[...])
    ms = jnp.sum(acc * acc, axis=-1, keepdims=True) * (1.0 / MLA_QK)
    y = acc * lax.rsqrt(ms + RMS_EPS) * g_ref[...]
    o_ref[:, :MLA_NOPE] = y[:, :MLA_NOPE].astype(o_ref.dtype)
    o_ref[:, MLA_NOPE:] = _rotate_pairs(y[:, MLA_NOPE:], cos_ref[...], sin_ref[...],
                                        MLA_ROPE // 2).astype(o_ref.dtype)


def _mla_q_proj(cq_n, w_uq_h, gain, cos_t, sin_t, seq, tm=512):
    t, rank = cq_n.shape
    per = seq // tm
    return pl.pallas_call(
        _mla_q_kernel,
        grid=(t // tm, MLA_HEADS),
        in_specs=[pl.BlockSpec((tm, rank), lambda i, h: (i, 0)),
                  pl.BlockSpec((None, rank, MLA_QK_PAD), lambda i, h: (h, 0, 0)),
                  pl.BlockSpec((1, MLA_QK_PAD), lambda i, h: (0, 0)),
                  pl.BlockSpec((tm, LANES), lambda i, h: (i % per, 0)),
                  pl.BlockSpec((tm, LANES), lambda i, h: (i % per, 0))],
        out_specs=pl.BlockSpec((None, tm, MLA_QK_PAD), lambda i, h: (h, i, 0)),
        out_shape=jax.ShapeDtypeStruct((MLA_HEADS, t, MLA_QK_PAD), BF16),
        compiler_params=_params("parallel", "parallel"),
        name="mla_q_proj",
    )(cq_n, w_uq_h, gain, cos_t, sin_t)


def _mla_kv_kernel(a_ref, w_ref, kr_ref, g0_ref, g1_ref, cos_ref, sin_ref, k_ref, v_ref):
    acc = _dot(a_ref[...], w_ref[...])
    kn = acc[:, :MLA_NOPE]
    kr = kr_ref[...]
    ms = (jnp.sum(kn * kn, axis=-1, keepdims=True)
          + jnp.sum(kr * kr, axis=-1, keepdims=True)) * (1.0 / MLA_QK)
    inv = lax.rsqrt(ms + RMS_EPS)
    k_ref[:, :MLA_NOPE] = (kn * inv * g0_ref[...]).astype(k_ref.dtype)
    k_ref[:, MLA_NOPE:] = _rotate_pairs(kr * inv * g1_ref[...], cos_ref[...], sin_ref[...],
                                        MLA_ROPE // 2).astype(k_ref.dtype)
    v_ref[:, :MLA_V] = acc[:, MLA_NOPE:].astype(v_ref.dtype)
    v_ref[:, MLA_V:] = jnp.ones((acc.shape[0], MLA_V), v_ref.dtype)


def _mla_kv_proj(ckv_n, w_ukv_h, k_r, g0, g1, cos_t, sin_t, seq, tm=512):
    t, rank = ckv_n.shape
    per = seq // tm
    return pl.pallas_call(
        _mla_kv_kernel,
        grid=(t // tm, MLA_HEADS),
        in_specs=[pl.BlockSpec((tm, rank), lambda i, h: (i, 0)),
                  pl.BlockSpec((None, rank, MLA_NOPE + MLA_V), lambda i, h: (h, 0, 0)),
                  pl.BlockSpec((tm, LANES), lambda i, h: (i, 0)),
                  pl.BlockSpec((1, LANES), lambda i, h: (0, 0)),
                  pl.BlockSpec((1, LANES), lambda i, h: (0, 0)),
                  pl.BlockSpec((tm, LANES), lambda i, h: (i % per, 0)),
                  pl.BlockSpec((tm, LANES), lambda i, h: (i % per, 0))],
        out_specs=[pl.BlockSpec((None, tm, MLA_QK_PAD), lambda i, h: (h, i, 0)),
                   pl.BlockSpec((None, tm, 2 * MLA_V), lambda i, h: (h, i, 0))],
        out_shape=[jax.ShapeDtypeStruct((MLA_HEADS, t, MLA_QK_PAD), BF16),
                   jax.ShapeDtypeStruct((MLA_HEADS, t, 2 * MLA_V), BF16)],
        compiler_params=_params("parallel", "parallel"),
        name="mla_kv_proj",
    )(ckv_n, w_ukv_h, k_r, g0, g1, cos_t, sin_t)


def _mla_attn_kernel(q_ref, k_ref, v_ref, o_ref):
    s = _dot_nt(q_ref[...], k_ref[...])
    m = jnp.max(s, axis=-1, keepdims=True)
    p = jnp.exp(s - m).astype(BF16)
    acc = _dot(p, v_ref[...])
    o_ref[...] = (acc[:, :MLA_V] / acc[:, MLA_V:]).astype(o_ref.dtype)


def _mla_attention(q, k, v, batch, seq, tq=256):
    t = q.shape[1]
    nq = seq // tq
    return pl.pallas_call(
        _mla_attn_kernel,
        grid=(batch, MLA_HEADS, nq),
        in_specs=[pl.BlockSpec((None, tq, MLA_QK_PAD), lambda b, h, i: (h, b * nq + i, 0)),
                  pl.BlockSpec((None, seq, MLA_QK_PAD), lambda b, h, i: (h, b, 0)),
                  pl.BlockSpec((None, seq, 2 * MLA_V), lambda b, h, i: (h, b, 0))],
        out_specs=pl.BlockSpec((tq, MLA_V), lambda b, h, i: (b * nq + i, h)),
        out_shape=jax.ShapeDtypeStruct((t, MLA_HEADS * MLA_V), BF16),
        compiler_params=_params("parallel", "parallel", "arbitrary"),
        name="mla_attention",
    )(q, k, v)


def _merge_kernel(ona_ref, odl_ref, omla_ref, wna_ref, wdl_ref, wmla_ref, g0_ref, g1_ref, g2_ref, o_ref):
    acc = g0_ref[...].astype(F32) * _dot(ona_ref[...], wna_ref[...])
    acc = acc + g1_ref[...].astype(F32) * _dot(odl_ref[...], wdl_ref[...])
    acc = acc + g2_ref[...].astype(F32) * _dot(omla_ref[...], wmla_ref[...])
    o_ref[...] = acc.astype(o_ref.dtype)


def _merge(o_na, o_dl, o_mla, w_na, w_dl, w_mla, gates, tm=512, tn=512):
    t = o_na.shape[0]
    d = w_na.shape[1]
    nj = d // tn

    def act(a):
        return pl.BlockSpec((tm, a.shape[1]), lambda i, j: (i, 0))

    def wgt(w):
        return pl.BlockSpec((w.shape[0], tn), lambda i, j: (0, j))

    def gate(idx):
        return pl.BlockSpec((tm, tn), lambda i, j: (i, idx * nj + j))

    return pl.pallas_call(
        _merge_kernel,
        grid=(t // tm, nj),
        in_specs=[act(o_na), act(o_dl), act(o_mla), wgt(w_na), wgt(w_dl), wgt(w_mla),
                  gate(0), gate(1), gate(2)],
        out_specs=pl.BlockSpec((tm, tn), lambda i, j: (i, j)),
        out_shape=jax.ShapeDtypeStruct((t, d), BF16),
        compiler_params=_params("parallel", "parallel"),
        name="branch_merge",
    )(o_na, o_dl, o_mla, w_na, w_dl, w_mla, gates, gates, gates)


def _gu_kernel(h_ref, wg_ref, wu_ref, o_ref):
    h = h_ref[...]
    g = _dot(h, wg_ref[...])
    u = _dot(h, wu_ref[...])
    o_ref[...] = (g * jax.nn.sigmoid(g) * u).astype(o_ref.dtype)


def _swiglu_up(h, w_gu, tm=512, tn=512):
    t, d = h.shape
    ff = w_gu.shape[1] // 2
    nj = ff // tn
    return pl.pallas_call(
        _gu_kernel,
        grid=(t // tm, nj),
        in_specs=[pl.BlockSpec((tm, d), lambda i, j: (i, 0)),
                  pl.BlockSpec((d, tn), lambda i, j: (0, j)),
                  pl.BlockSpec((d, tn), lambda i, j: (0, nj + j))],
        out_specs=pl.BlockSpec((tm, tn), lambda i, j: (i, j)),
        out_shape=jax.ShapeDtypeStruct((t, ff), BF16),
        compiler_params=_params("parallel", "parallel"),
        name="swiglu_up",
    )(h, w_gu, w_gu)


MOE_TM = 512
TOP_K = 2


def _moe_routing(route, tokens):
    e_flat = jnp.concatenate([route[:, N_EXPERTS], route[:, N_EXPERTS + 1]]).astype(jnp.int32)
    onehot = (e_flat[:, None] == jnp.arange(N_EXPERTS, dtype=jnp.int32)[None, :]).astype(jnp.int32)
    csum = jnp.cumsum(onehot, axis=0)
    rank = jnp.sum((csum - onehot) * onehot, axis=1)
    padded = ((csum[-1] + MOE_TM - 1) // MOE_TM) * MOE_TM
    ends = jnp.cumsum(padded)
    slot = jnp.sum(onehot * (ends - padded)[None, :], axis=1) + rank
    n_tiles = (TOP_K * tokens) // MOE_TM + N_EXPERTS
    tile_start = jnp.arange(n_tiles, dtype=jnp.int32) * MOE_TM
    tile_expert = jnp.minimum(jnp.sum(tile_start[:, None] >= ends[None, :], axis=1), N_EXPERTS - 1)
    te = jnp.concatenate([tile_expert, ends[-1:] // MOE_TM]).astype(jnp.int32)
    return slot.astype(jnp.int32), te


def _dispatch_kernel(slot_ref, x_ref, g_ref, sc_ref, sh_ref, dst_in_ref, dst_ref, h_ref, sem, *, rows, tokens):
    del dst_in_ref
    base = pl.program_id(0) * rows
    h_ref[...] = _modulated_norm(x_ref[...], g_ref[...], sc_ref[...], sh_ref[...])

    def row_copy(r, choice):
        slot = slot_ref[choice * tokens + base + r]
        return pltpu.make_async_copy(h_ref.at[pl.ds(r, 1), :], dst_ref.at[pl.ds(slot, 1), :], sem)

    def start(r, carry):
        row_copy(r, 0).start()
        row_copy(r, 1).start()
        return carry

    def wait(r, carry):
        row_copy(r, 0).wait()
        row_copy(r, 1).wait()
        return carry

    lax.fori_loop(0, rows, start, 0, unroll=8)
    lax.fori_loop(0, rows, wait, 0, unroll=8)


def _moe_dispatch(x, g, sc, sh, slot, seq, n_rows, rows=256):
    t, d = x.shape
    per = seq // rows
    return pl.pallas_call(
        functools.partial(_dispatch_kernel, rows=rows, tokens=t),
        grid_spec=pltpu.PrefetchScalarGridSpec(
            num_scalar_prefetch=1,
            grid=(t // rows,),
            in_specs=[pl.BlockSpec((rows, d), lambda i, s: (i, 0)),
                      pl.BlockSpec((1, d), lambda i, s: (0, 0)),
                      pl.BlockSpec((None, 1, d), lambda i, s: (i // per, 0, 0)),
                      pl.BlockSpec((None, 1, d), lambda i, s: (i // per, 0, 0)),
                      pl.BlockSpec(memory_space=pl.ANY)],
            out_specs=pl.BlockSpec(memory_space=pl.ANY),
            scratch_shapes=[pltpu.VMEM((rows, d), F32), pltpu.SemaphoreType.DMA(())]),
        out_shape=jax.ShapeDtypeStruct((n_rows, d), F32),
        input_output_aliases={5: 0},
        compiler_params=_params("arbitrary"),
        name="moe_dispatch",
    )(slot, x, g.reshape(1, d), sc, sh, jnp.zeros((n_rows, d), F32))


def _moe_up_kernel(te_ref, xs_ref, wg_ref, wu_ref, o_ref, hb_ref, *, n_tiles):
    used = pl.program_id(0) < te_ref[n_tiles]

    @pl.when(used & (pl.program_id(1) == 0))
    def _():
        hb_ref[...] = xs_ref[...].astype(BF16)

    @pl.when(used)
    def _():
        h = hb_ref[...]
        g = _dot(h, wg_ref[...])
        u = _dot(h, wu_ref[...])
        o_ref[...] = (g * jax.nn.sigmoid(g) * u).astype(o_ref.dtype)

    @pl.when(jnp.logical_not(used))
    def _():
        o_ref[...] = jnp.zeros(o_ref.shape, o_ref.dtype)


def _moe_up(xs, w_gu, te, tn=512):
    n_rows, d = xs.shape
    ff = w_gu.shape[2] // 2
    nj = ff // tn
    n_tiles = n_rows // MOE_TM

    def last_used(i, te_ref):
        return jnp.minimum(i, te_ref[n_tiles] - 1)

    return pl.pallas_call(
        functools.partial(_moe_up_kernel, n_tiles=n_tiles),
        grid_spec=pltpu.PrefetchScalarGridSpec(
            num_scalar_prefetch=1,
            grid=(n_tiles, nj),
            in_specs=[pl.BlockSpec((MOE_TM, d), lambda i, j, te_ref: (last_used(i, te_ref), 0)),
                      pl.BlockSpec((None, d, tn), lambda i, j, te_ref: (te_ref[i], 0, j)),
                      pl.BlockSpec((None, d, tn), lambda i, j, te_ref: (te_ref[i], 0, nj + j))],
            out_specs=pl.BlockSpec((MOE_TM, tn), lambda i, j, te_ref: (i, j)),
            scratch_shapes=[pltpu.VMEM((MOE_TM, d), BF16)]),
        out_shape=jax.ShapeDtypeStruct((n_rows, ff), BF16),
        compiler_params=_params("parallel", "arbitrary"),
        name="moe_up",
    )(te, xs, w_gu, w_gu)


def _moe_down_kernel(te_ref, a_ref, w_ref, o_ref, *, n_tiles):
    used = pl.program_id(0) < te_ref[n_tiles]

    @pl.when(used)
    def _():
        o_ref[...] = _dot(a_ref[...], w_ref[...])

    @pl.when(jnp.logical_not(used))
    def _():
        o_ref[...] = jnp.zeros(o_ref.shape, o_ref.dtype)


def _moe_down(act, w_down, te, tn=512):
    n_rows, ff = act.shape
    d = w_down.shape[2]
    n_tiles = n_rows // MOE_TM

    def last_used(i, te_ref):
        return jnp.minimum(i, te_ref[n_tiles] - 1)

    return pl.pallas_call(
        functools.partial(_moe_down_kernel, n_tiles=n_tiles),
        grid_spec=pltpu.PrefetchScalarGridSpec(
            num_scalar_prefetch=1,
            grid=(n_tiles, d // tn),
            in_specs=[pl.BlockSpec((MOE_TM, ff), lambda i, j, te_ref: (last_used(i, te_ref), 0)),
                      pl.BlockSpec((None, ff, tn), lambda i, j, te_ref: (te_ref[i], 0, j))],
            out_specs=pl.BlockSpec((MOE_TM, tn), lambda i, j, te_ref: (i, j))),
        out_shape=jax.ShapeDtypeStruct((n_rows, d), F32),
        compiler_params=_params("parallel", "arbitrary"),
        name="moe_down",
    )(te, act, w_down)


def _combine_kernel(slot_ref, x_ref, gate_ref, w_ref, ys_ref, o_ref, buf_ref, sem, *, rows, tokens):
    base = pl.program_id(0) * rows

    def row_copy(r, choice):
        slot = slot_ref[choice * tokens + base + r]
        return pltpu.make_async_copy(ys_ref.at[pl.ds(slot, 1), :], buf_ref.at[choice, pl.ds(r, 1), :], sem)

    def start(r, carry):
        row_copy(r, 0).start()
        row_copy(r, 1).start()
        return carry

    def wait(r, carry):
        row_copy(r, 0).wait()
        row_copy(r, 1).wait()
        return carry

    lax.fori_loop(0, rows, start, 0, unroll=8)
    lax.fori_loop(0, rows, wait, 0, unroll=8)
    w = w_ref[...]
    y = w[:, 0:1] * buf_ref[0] + w[:, 1:2] * buf_ref[1]
    o_ref[...] = x_ref[...] + gate_ref[...] * y


def _moe_combine(x, gate, w12, ys, slot, seq, rows=256):
    t, d = x.shape
    per = seq // rows
    return pl.pallas_call(
        functools.partial(_combine_kernel, rows=rows, tokens=t),
        grid_spec=pltpu.PrefetchScalarGridSpec(
            num_scalar_prefetch=1,
            grid=(t // rows,),
            in_specs=[pl.BlockSpec((rows, d), lambda i, s: (i, 0)),
                      pl.BlockSpec((None, 1, d), lambda i, s: (i // per, 0, 0)),
                      pl.BlockSpec((rows, TOP_K), lambda i, s: (i, 0)),
                      pl.BlockSpec(memory_space=pl.ANY)],
            out_specs=pl.BlockSpec((rows, d), lambda i, s: (i, 0)),
            scratch_shapes=[pltpu.VMEM((TOP_K, rows, d), F32), pltpu.SemaphoreType.DMA(())]),
        out_shape=jax.ShapeDtypeStruct((t, d), F32),
        compiler_params=_params("arbitrary"),
        name="moe_combine",
    )(slot, x, gate, w12, ys)


def _router_kernel(x_ref, g_ref, sc_ref, sh_ref, w_ref, o_ref):
    h = _modulated_norm(x_ref[...], g_ref[...], sc_ref[...], sh_ref[...])
    logits = jnp.dot(h, w_ref[...], preferred_element_type=F32, precision=lax.Precision.HIGHEST)
    lane = lax.broadcasted_iota(jnp.int32, logits.shape, 1).astype(F32)
    lg = jnp.where(lane < N_EXPERTS, logits, NEG_INF)
    m1 = jnp.max(lg, axis=-1, keepdims=True)
    i1 = jnp.min(jnp.where(lg == m1, lane, float(LANES)), axis=-1, keepdims=True)
    lg2 = jnp.where(lane == i1, NEG_INF, lg)
    m2 = jnp.max(lg2, axis=-1, keepdims=True)
    i2 = jnp.min(jnp.where(lg2 == m2, lane, float(LANES)), axis=-1, keepdims=True)
    e2 = jnp.exp(m2 - m1)
    z = 1.0 + e2
    out = jnp.where(lane == N_EXPERTS, i1, 0.0) + jnp.where(lane == N_EXPERTS + 1, i2, 0.0)
    out = out + jnp.where(lane == N_EXPERTS + 2, 1.0 / z, 0.0) + jnp.where(lane == N_EXPERTS + 3, e2 / z, 0.0)
    o_ref[...] = out


def _router(x, g, sc, sh, w_router_pad, seq, tm=512):
    t, d = x.shape
    per = seq // tm
    return pl.pallas_call(
        _router_kernel,
        grid=(t // tm,),
        in_specs=[pl.BlockSpec((tm, d), lambda i: (i, 0)),
                  pl.BlockSpec((1, d), lambda i: (0, 0)),
                  pl.BlockSpec((None, 1, d), lambda i: (i // per, 0, 0)),
                  pl.BlockSpec((None, 1, d), lambda i: (i // per, 0, 0)),
                  pl.BlockSpec((d, LANES), lambda i: (0, 0))],
        out_specs=pl.BlockSpec((tm, LANES), lambda i: (i, 0)),
        out_shape=jax.ShapeDtypeStruct((t, LANES), F32),
        compiler_params=_params("parallel"),
        name="router",
    )(x, g.reshape(1, d), sc, sh, w_router_pad)


def _residual_matmul(a, w, x, gate, seq, name, tm=512, tn=512, tk=None):
    per = seq // tm
    return _matmul(a, w, tm=tm, tn=tn, tk=tk, out_dtype=F32, epilogue=_ep_residual,
                   extras=[(x, (tm, tn), lambda i, j, k: (i, j)),
                           (gate, (None, 1, tn), lambda i, j, k: (i // per, 0, j))],
                   name=name)


def _token_mixer(h, x, gate, lw, batch, seq, tables):
    d = h.shape[1]
    w_in = lw["w_in"]
    tm = 512
    per = seq // tm
    cos_d, sin_d, cos_m, sin_m = tables

    def cols(lo, hi):
        return w_in[:, lo:hi].astype(BF16)

    def tile_gain(gq, gk, heads):
        return jnp.concatenate([jnp.tile(gq, heads), jnp.tile(gk, heads)]).reshape(1, -1).astype(F32)

    o = 0
    na_qk = _matmul(h, cols(o, o + 2 * NA_W), tm=tm, tn=512, out_dtype=BF16, epilogue=_ep_headnorm,
                    extras=[(tile_gain(lw["na_q_norm"], lw["na_k_norm"], NA_HEADS), (1, 512),
                             lambda i, j, k: (0, j))], name="na_qk_proj")
    o += 2 * NA_W
    na_v = _matmul(h, cols(o, o + NA_W), tm=tm, tn=512, out_dtype=BF16, name="na_v_proj")
    o += NA_W
    dl_qk = _matmul(h, cols(o, o + 2 * DIL_W), tm=tm, tn=512, out_dtype=BF16, epilogue=_ep_headnorm_rope,
                    extras=[(tile_gain(lw["dil_q_norm"], lw["dil_k_norm"], DIL_HEADS), (1, 512),
                             lambda i, j, k: (0, j)),
                            (cos_d, (tm, LANES), lambda i, j, k: (i % per, 0)),
                            (sin_d, (tm, LANES), lambda i, j, k: (i % per, 0))], name="dil_qk_proj")
    o += 2 * DIL_W
    dl_v = _matmul(h, cols(o, o + DIL_W), tm=tm, tn=512, out_dtype=BF16, name="dil_v_proj")
    o += DIL_W
    cq_n = _matmul(h, cols(o, o + MLA_Q_RANK), tm=tm, tn=MLA_Q_RANK, out_dtype=BF16, epilogue=_ep_rownorm,
                   extras=[(lw["mla_q_a_norm"].reshape(1, -1), (1, MLA_Q_RANK), lambda i, j, k: (0, 0))],
                   name="mla_cq_proj")
    o += MLA_Q_RANK
    ckv_n = _matmul(h, cols(o, o + MLA_KV_RANK), tm=tm, tn=MLA_KV_RANK, out_dtype=BF16, epilogue=_ep_rownorm,
                    extras=[(lw["mla_kv_a_norm"].reshape(1, -1), (1, MLA_KV_RANK), lambda i, j, k: (0, 0))],
                    name="mla_ckv_proj")
    o += MLA_KV_RANK
    w_kr = jnp.pad(w_in[:, o:o + MLA_ROPE], ((0, 0), (0, LANES - MLA_ROPE))).astype(BF16)
    k_r = _matmul(h, w_kr, tm=tm, tn=LANES, out_dtype=F32, name="mla_kr_proj")
    o += MLA_ROPE
    gates = _matmul(h, cols(o, o + 3 * d), tm=tm, tn=512, out_dtype=BF16, epilogue=_ep_sigmoid,
                    name="gate_proj")

    bias = _na_bias_table(lw["na_rpb"], seq // GRID_W)
    o_na = _na_attention(na_qk, na_v, bias, batch, seq)
    o_dl = _dilated_attention(dl_qk, dl_v, batch, seq)
    w_uq = lw["mla_w_uq"].reshape(MLA_Q_RANK, MLA_HEADS, MLA_QK).transpose(1, 0, 2)
    w_uq = jnp.pad(w_uq, ((0, 0), (0, 0), (0, MLA_QK_PAD - MLA_QK))).astype(BF16)
    w_ukv = lw["mla_w_ukv"].reshape(MLA_KV_RANK, MLA_HEADS, MLA_NOPE + MLA_V).transpose(1, 0, 2).astype(BF16)
    gq = jnp.pad(lw["mla_q_norm"] * (1.0 / math.sqrt(MLA_QK)), (0, MLA_QK_PAD - MLA_QK)).reshape(1, MLA_QK_PAD)
    gk = lw["mla_k_norm"]
    gk0 = gk[:MLA_NOPE].reshape(1, LANES)
    gk1 = jnp.pad(gk[MLA_NOPE:], (0, LANES - MLA_ROPE)).reshape(1, LANES)
    q_m = _mla_q_proj(cq_n, w_uq, gq, cos_m, sin_m, seq)
    k_m, v_m = _mla_kv_proj(ckv_n, w_ukv, k_r, gk0, gk1, cos_m, sin_m, seq)
    o_mla = _mla_attention(q_m, k_m, v_m, batch, seq)

    merged = _merge(o_na, o_dl, o_mla, lw["w_branch_na"].astype(BF16), lw["w_branch_dil"].astype(BF16),
                    lw["w_branch_mla"].astype(BF16), gates)
    return _residual_matmul(merged, lw["w_out"].astype(BF16), x, gate, seq, "out_proj")


def kernel(x, c, w_ada, b_ada, norm_mix, norm_ffn, w_in, na_q_norm, na_k_norm, na_rpb, dil_q_norm, dil_k_norm, mla_q_a_norm, mla_w_uq, mla_kv_a_norm, mla_w_ukv, mla_q_norm, mla_k_norm, w_branch_na, w_branch_dil, w_branch_mla, w_out, ffn_w_gu, ffn_w_down, moe_router, moe_w_gu, moe_w_down):
    batch, seq, d = x.shape
    depth = w_ada.shape[0]
    assert seq % 512 == 0 and seq % GRID_W == 0
    t = batch * seq
    xf = x.reshape(t, d)

    c_pad = jnp.pad(c, ((0, (-batch) % 8), (0, 0)))
    mod = _adaln(c_pad, w_ada, b_ada)[:, :batch].reshape(depth, batch, 6, 1, d)
    tables = _rope_tables(seq, ROT_DIM, 1.0) + _rope_tables(seq, MLA_ROPE, 1.0)

    for l in range(depth):
        sh1, sc1, g1, sh2, sc2, g2 = [mod[l, :, i] for i in range(6)]
        lw = dict(w_in=w_in[l], na_q_norm=na_q_norm[l], na_k_norm=na_k_norm[l], na_rpb=na_rpb[l],
                  dil_q_norm=dil_q_norm[l], dil_k_norm=dil_k_norm[l], mla_q_a_norm=mla_q_a_norm[l],
                  mla_w_uq=mla_w_uq[l], mla_kv_a_norm=mla_kv_a_norm[l], mla_w_ukv=mla_w_ukv[l],
                  mla_q_norm=mla_q_norm[l], mla_k_norm=mla_k_norm[l], w_branch_na=w_branch_na[l],
                  w_branch_dil=w_branch_dil[l], w_branch_mla=w_branch_mla[l], w_out=w_out[l])
        h = _normmod(xf, norm_mix[l], sc1, sh1, seq)
        xf = _token_mixer(h, xf, g1, lw, batch, seq, tables)

        if l % 2 == 0:
            h = _normmod(xf, norm_ffn[l], sc2, sh2, seq)
            act = _swiglu_up(h, ffn_w_gu[l // 2].astype(BF16))
            xf = _residual_matmul(act, ffn_w_down[l // 2].astype(BF16), xf, g2, seq, "down_proj")
        else:
            w_r = jnp.pad(moe_router[l // 2], ((0, 0), (0, LANES - N_EXPERTS)))
            route = _router(xf, norm_ffn[l], sc2, sh2, w_r, seq)
            slot, te = _moe_routing(route, t)
            n_rows = TOP_K * t + N_EXPERTS * MOE_TM
            xs = _moe_dispatch(xf, norm_ffn[l], sc2, sh2, slot, seq, n_rows)
            act = _moe_up(xs, moe_w_gu[l // 2].astype(BF16), te)
            ys = _moe_down(act, moe_w_down[l // 2].astype(BF16), te)
            xf = _moe_combine(xf, g2, route[:, N_EXPERTS + 2:N_EXPERTS + 4], ys, slot, seq)
    return xf.reshape(batch, seq, d)
```

```python
import functools
import math

import numpy as np
import jax
import jax.numpy as jnp
from jax import lax
from jax.experimental import pallas as pl
from jax.experimental.pallas import tpu as pltpu

GRID_W = 64
HEAD_DIM = 128
ROPE_THETA = 500000.0
ROT_DIM = HEAD_DIM // 4
RMS_EPS = 1e-6
NEG_INF = -1e30
NA_HEADS = 8
NA_KH = 8
NA_KW = 16
DIL_GROUPS = ((128, 1), (512, 4), (2048, 16))
DIL_HEADS_PER_GROUP = 4
DIL_HEADS = DIL_HEADS_PER_GROUP * len(DIL_GROUPS)
DIL_RADIUS = 64
MLA_HEADS = 8
MLA_NOPE = 128
MLA_ROPE = 64
MLA_V = 128
MLA_Q_RANK = 768
MLA_KV_RANK = 512
MLA_QK = MLA_NOPE + MLA_ROPE
MLA_QK_PAD = 256
N_EXPERTS = 8
LANES = 128

NA_W = NA_HEADS * HEAD_DIM
DIL_W = DIL_HEADS * HEAD_DIM
DIL_OUT = DIL_HEADS_PER_GROUP * HEAD_DIM

VMEM_LIMIT_BYTES = 48 * 1024 * 1024

F32 = jnp.float32
BF16 = jnp.bfloat16


def _params(*sem):
    return pltpu.CompilerParams(dimension_semantics=sem, vmem_limit_bytes=VMEM_LIMIT_BYTES)


def _dot(a, b):
    return jnp.dot(a, b, preferred_element_type=F32)


def _dot_nt(a, b):
    return lax.dot_general(a, b, (((1,), (1,)), ((), ())), preferred_element_type=F32)


def _adaln_kernel(c_ref, w_ref, b_ref, o_ref):
    o_ref[...] = _dot(c_ref[...], w_ref[...]) + b_ref[...]


def _adaln(c_pad, w_ada, b_ada):
    depth, d, n = w_ada.shape
    tn = 1024
    return pl.pallas_call(
        _adaln_kernel,
        grid=(depth, n // tn),
        in_specs=[pl.BlockSpec(c_pad.shape, lambda l, j: (0, 0)),
                  pl.BlockSpec((None, d, tn), lambda l, j: (l, 0, j)),
                  pl.BlockSpec((None, 1, tn), lambda l, j: (l, 0, j))],
        out_specs=pl.BlockSpec((None, c_pad.shape[0], tn), lambda l, j: (l, 0, j)),
        out_shape=jax.ShapeDtypeStruct((depth, c_pad.shape[0], n), F32),
        compiler_params=_params("parallel", "parallel"),
        name="adaln",
    )(c_pad, w_ada, b_ada.reshape(depth, 1, n))


def _modulated_norm(x, g, sc, sh):
    ms = jnp.mean(x * x, axis=-1, keepdims=True)
    return (x * lax.rsqrt(ms + RMS_EPS) * g) * (1.0 + sc) + sh


def _normmod_kernel(x_ref, g_ref, sc_ref, sh_ref, o_ref):
    o_ref[...] = _modulated_norm(x_ref[...], g_ref[...], sc_ref[...], sh_ref[...]).astype(o_ref.dtype)


def _normmod(x, g, sc, sh, seq, tm=512):
    t, d = x.shape
    per = seq // tm
    return pl.pallas_call(
        _normmod_kernel,
        grid=(t // tm,),
        in_specs=[pl.BlockSpec((tm, d), lambda i: (i, 0)),
                  pl.BlockSpec((1, d), lambda i: (0, 0)),
                  pl.BlockSpec((None, 1, d), lambda i: (i // per, 0, 0)),
                  pl.BlockSpec((None, 1, d), lambda i: (i // per, 0, 0))],
        out_specs=pl.BlockSpec((tm, d), lambda i: (i, 0)),
        out_shape=jax.ShapeDtypeStruct((t, d), BF16),
        compiler_params=_params("parallel"),
        name="normmod",
    )(x, g.reshape(1, d), sc, sh)


def _mm_kernel(*refs, n_extra, epilogue, split):
    a_ref, b_ref = refs[0], refs[1]
    extras = refs[2:2 + n_extra]
    o_ref = refs[2 + n_extra]
    a = a_ref[...]
    for c in range(b_ref.shape[1] // split):
        cols = slice(c * split, (c + 1) * split)
        epilogue(_dot(a, b_ref[:, cols]), extras, o_ref, cols)


def _ep_store(acc, extras, o_ref, cols):
    o_ref[:, cols] = acc.astype(o_ref.dtype)


def _matmul(a, b, *, tm, tn, out_dtype, epilogue=_ep_store, extras=(), split=None, name):
    m, kdim = a.shape
    n = b.shape[1]
    split = tn if split is None else split
    assert m % tm == 0 and n % tn == 0 and tn % split == 0
    in_specs = [pl.BlockSpec((tm, kdim), lambda i, j: (i, 0)),
                pl.BlockSpec((kdim, tn), lambda i, j: (0, j))]
    in_specs += [pl.BlockSpec(bs, im) for (_, bs, im) in extras]
    return pl.pallas_call(
        functools.partial(_mm_kernel, n_extra=len(extras), epilogue=epilogue, split=split),
        grid=(m // tm, n // tn),
        in_specs=in_specs,
        out_specs=pl.BlockSpec((tm, tn), lambda i, j: (i, j)),
        out_shape=jax.ShapeDtypeStruct((m, n), out_dtype),
        compiler_params=_params("parallel", "parallel"),
        name=name,
    )(a, b, *[e[0] for e in extras])


def _head_rms(blk, gain):
    ms = jnp.mean(blk * blk, axis=-1, keepdims=True)
    return blk * lax.rsqrt(ms + RMS_EPS) * gain


def _head_slices(cols):
    return [slice(c, c + HEAD_DIM) for c in range(cols.start, cols.stop, HEAD_DIM)]


def _ep_headnorm(acc, extras, o_ref, cols):
    for i, sl in enumerate(_head_slices(cols)):
        blk = acc[:, i * HEAD_DIM:(i + 1) * HEAD_DIM]
        o_ref[:, sl] = _head_rms(blk, extras[0][:, sl]).astype(o_ref.dtype)


def _rotate_pairs(y, cos_t, sin_t, half):
    lane = lax.broadcasted_iota(jnp.int32, y.shape, 1)
    swapped = jnp.where(lane < half, pltpu.roll(y, LANES - half, 1), pltpu.roll(y, half, 1))
    return y * cos_t + swapped * sin_t


def _pair_swap_matrix(width, half):
    p = np.zeros((width, width), np.float32)
    for base in range(0, width, HEAD_DIM):
        for i in range(half):
            p[base + i + half, base + i] = 1.0
            p[base + i, base + i + half] = 1.0
    return jnp.asarray(p, BF16)


def _ep_headnorm_rope(acc, extras, o_ref, cols):
    cos_t = extras[1][...]
    sin_t = extras[2][...]
    heads = _head_slices(cols)
    y = jnp.concatenate([_head_rms(acc[:, i * HEAD_DIM:(i + 1) * HEAD_DIM], extras[0][:, sl])
                         for i, sl in enumerate(heads)], axis=1)
    swapped = _dot(y.astype(BF16), extras[3][...])
    for i, sl in enumerate(heads):
        loc = slice(i * HEAD_DIM, (i + 1) * HEAD_DIM)
        o_ref[:, sl] = (y[:, loc] * cos_t + swapped[:, loc] * sin_t).astype(o_ref.dtype)


def _ep_rownorm(acc, extras, o_ref, cols):
    o_ref[:, cols] = _head_rms(acc, extras[0][:, cols]).astype(o_ref.dtype)


def _ep_sigmoid(acc, extras, o_ref, cols):
    o_ref[:, cols] = jax.nn.sigmoid(acc).astype(o_ref.dtype)


def _ep_residual(acc, extras, o_ref, cols):
    o_ref[:, cols] = extras[0][:, cols] + extras[1][:, cols] * acc


def _rope_tables(seq, rot, fill):
    half = rot // 2
    inv = ROPE_THETA ** (-jnp.arange(half, dtype=F32) * (2.0 / rot))
    ang = jnp.arange(seq, dtype=jnp.int32).astype(F32)[:, None] * inv[None, :]
    cos, sin = jnp.cos(ang), jnp.sin(ang)
    pad = LANES - rot
    cos_t = jnp.concatenate([cos, cos, jnp.full((seq, pad), fill, F32)], axis=1)
    sin_t = jnp.concatenate([-sin, sin, jnp.zeros((seq, pad), F32)], axis=1)
    return cos_t, sin_t


NA_RB = 4
NA_KROWS = NA_KH + NA_RB


def _na_geometry(rows):
    kh = min(NA_KH, rows)
    assert rows % NA_RB == 0 and rows >= NA_KROWS
    patterns, ids, kstarts = [], [], []
    for r in range(0, rows, NA_RB):
        ks = int(np.clip(r - NA_KH // 2, 0, rows - NA_KROWS))
        pat = tuple((int(np.clip(r + a - NA_KH // 2, 0, rows - kh)) - ks, ks - (r + a)) for a in range(NA_RB))
        if pat not in patterns:
            patterns.append(pat)
        ids.append(patterns.index(pat))
        kstarts.append(ks)
    return patterns, np.array(ids + kstarts, np.int32)


def _na_bias_table(rpb, rows, patterns):
    h = rpb.shape[0]
    kh = min(NA_KH, rows)
    qc = np.arange(GRID_W)
    kc = np.arange(GRID_W)
    cstart = np.clip(qc - NA_KW // 2, 0, GRID_W - NA_KW)
    ok = (kc[None, :] >= cstart[:, None]) & (kc[None, :] < cstart[:, None] + NA_KW)
    dc = np.clip(kc[None, :] - qc[:, None] + NA_KW - 1, 0, 2 * NA_KW - 2)
    by_col = rpb.astype(F32)[:, :, dc] + jnp.where(ok, 0.0, NEG_INF).astype(F32)
    kr = np.arange(NA_KROWS)
    dr = np.zeros((len(patterns), NA_RB, NA_KROWS), np.int32)
    row_ok = np.zeros((len(patterns), NA_RB, NA_KROWS), bool)
    for p, pat in enumerate(patterns):
        for a, (start, offset) in enumerate(pat):
            row_ok[p, a] = (kr >= start) & (kr < start + kh)
            dr[p, a] = np.clip(kr + offset + NA_KH - 1, 0, 2 * NA_KH - 2)
    tab = jnp.take(by_col, jnp.asarray(dr.reshape(-1)), axis=1)
    tab = tab.reshape(h, len(patterns), NA_RB, NA_KROWS, GRID_W, GRID_W)
    tab = jnp.where(row_ok[None, :, :, :, None, None], tab, NEG_INF)
    tab = tab.transpose(0, 1, 2, 4, 3, 5)
    return tab.reshape(h, len(patterns), NA_RB * GRID_W, NA_KROWS * GRID_W)


def _na_kernel(geo_ref, q_ref, k_ref, v_ref, b_ref, o_ref, *, n_blocks):
    scale = 1.0 / math.sqrt(HEAD_DIM)
    qn = NA_RB * GRID_W
    kn = NA_KROWS * GRID_W

    def body(blk, carry):
        q0 = pl.multiple_of(blk * qn, qn)
        k0 = pl.multiple_of(geo_ref[n_blocks + blk] * GRID_W, GRID_W)
        q = q_ref[pl.ds(q0, qn), :]
        k = k_ref[pl.ds(k0, kn), :]
        v = v_ref[pl.ds(k0, kn), :]
        s = _dot_nt(q, k) * scale + b_ref[geo_ref[blk]]
        m = jnp.max(s, axis=-1, keepdims=True)
        p = jnp.exp(s - m)
        den = jnp.sum(p, axis=-1, keepdims=True)
        o = _dot(p.astype(BF16), v) / den
        o_ref[pl.ds(q0, qn), :] = o.astype(o_ref.dtype)
        return carry

    lax.fori_loop(0, n_blocks, body, 0, unroll=2)


def _na_attention(qk, v, rpb, batch, seq):
    rows = seq // GRID_W
    patterns, geo = _na_geometry(rows)
    bias = _na_bias_table(rpb, rows, patterns)
    t = qk.shape[0]
    blk = (seq, HEAD_DIM)
    return pl.pallas_call(
        functools.partial(_na_kernel, n_blocks=rows // NA_RB),
        grid_spec=pltpu.PrefetchScalarGridSpec(
            num_scalar_prefetch=1,
            grid=(batch, NA_HEADS),
            in_specs=[pl.BlockSpec(blk, lambda b, h, geo_ref: (b, h)),
                      pl.BlockSpec(blk, lambda b, h, geo_ref: (b, NA_HEADS + h)),
                      pl.BlockSpec(blk, lambda b, h, geo_ref: (b, h)),
                      pl.BlockSpec((None,) + bias.shape[1:], lambda b, h, geo_ref: (h, 0, 0, 0))],
            out_specs=pl.BlockSpec(blk, lambda b, h, geo_ref: (b, h))),
        out_shape=jax.ShapeDtypeStruct((t, NA_W), BF16),
        compiler_params=_params("parallel", "parallel"),
        name="na_attention",
    )(jnp.asarray(geo), qk, qk, v, bias)


DIL_QCHUNK = 2 * DIL_RADIUS
DIL_KWIN = 4 * DIL_RADIUS
DIL_STEP_CHUNKS = 32


def _dil_kernel(q_ref, k_ref, v_ref, o_ref, lse_ref, *, length):
    scale = 1.0 / math.sqrt(HEAD_DIM)
    n_chunk = length // DIL_QCHUNK

    def body(it, carry):
        st = it // n_chunk
        j = it % n_chunk
        q0 = pl.multiple_of(j * DIL_QCHUNK, DIL_QCHUNK)
        k0 = pl.multiple_of(jnp.clip(j * DIL_QCHUNK - DIL_RADIUS, 0, length - DIL_KWIN), DIL_RADIUS)
        q = q_ref[st, pl.ds(q0, DIL_QCHUNK), :]
        k = k_ref[st, pl.ds(k0, DIL_KWIN), :]
        v = v_ref[st, pl.ds(k0, DIL_KWIN), :]
        s = _dot_nt(q, k) * scale
        qi = q0 + lax.broadcasted_iota(jnp.int32, s.shape, 0)
        ki = k0 + lax.broadcasted_iota(jnp.int32, s.shape, 1)
        s = s + jnp.where(jnp.abs(ki - qi) <= DIL_RADIUS, 0.0, NEG_INF)
        m = jnp.max(s, axis=-1, keepdims=True)
        e = jnp.exp(s - m)
        den = jnp.sum(e, axis=-1, keepdims=True)
        o_ref[st, pl.ds(q0, DIL_QCHUNK), :] = _dot(e.astype(BF16), v) / den
        lse_ref[st, pl.ds(q0, DIL_QCHUNK), :] = m + jnp.log(den)
        return carry

    lax.fori_loop(0, q_ref.shape[0] * n_chunk, body, 0, unroll=4)


def _dil_group(q, k, v):
    streams, length, hd = q.shape
    assert length % DIL_QCHUNK == 0 and length >= DIL_KWIN
    sb = max(1, DIL_STEP_CHUNKS // (length // DIL_QCHUNK))
    assert streams % sb == 0
    spec = pl.BlockSpec((sb, length, hd), lambda s: (s, 0, 0))
    return pl.pallas_call(
        functools.partial(_dil_kernel, length=length),
        grid=(streams // sb,),
        in_specs=[spec, spec, spec],
        out_specs=[spec, pl.BlockSpec((sb, length, 1), lambda s: (s, 0, 0))],
        out_shape=[jax.ShapeDtypeStruct((streams, length, hd), F32),
                   jax.ShapeDtypeStruct((streams, length, 1), F32)],
        compiler_params=_params("parallel"),
        name="dilated_attention",
    )(q, k, v)


def _dil_mix_kernel(o0_ref, o1_ref, o2_ref, l0_ref, l1_ref, l2_ref, o_ref):
    l0, l1, l2 = l0_ref[...], l1_ref[...], l2_ref[...]
    mx = jnp.maximum(jnp.maximum(l0, l1), l2)
    e0, e1, e2 = jnp.exp(l0 - mx), jnp.exp(l1 - mx), jnp.exp(l2 - mx)
    z = e0 + e1 + e2
    w0, w1, w2 = e0 / z, e1 / z, e2 / z
    for h in range(DIL_HEADS_PER_GROUP):
        sl = slice(h * HEAD_DIM, (h + 1) * HEAD_DIM)
        hs = slice(h, h + 1)
        o_ref[:, sl] = (w0[:, hs] * o0_ref[:, sl] + w1[:, hs] * o1_ref[:, sl]
                        + w2[:, hs] * o2_ref[:, sl]).astype(o_ref.dtype)


def _dil_mix(outs, lses, tm=512):
    t = outs[0].shape[0]
    ospec = pl.BlockSpec((tm, DIL_OUT), lambda i: (i, 0))
    lspec = pl.BlockSpec((tm, DIL_HEADS_PER_GROUP), lambda i: (i, 0))
    return pl.pallas_call(
        _dil_mix_kernel,
        grid=(t // tm,),
        in_specs=[ospec] * 3 + [lspec] * 3,
        out_specs=ospec,
        out_shape=jax.ShapeDtypeStruct((t, DIL_OUT), BF16),
        compiler_params=_params("parallel"),
        name="dilated_mix",
    )(*outs, *lses)


def _dilated_attention(qk, v, batch, seq):
    hg = DIL_HEADS_PER_GROUP
    outs, lses = [], []
    for g, (window, dil) in enumerate(DIL_GROUPS):
        assert window // (2 * dil) == DIL_RADIUS and seq % dil == 0
        length = seq // dil

        def lattice(t2d, col0):
            tt = t2d[:, col0:col0 + hg * HEAD_DIM].reshape(batch, length, dil, hg, HEAD_DIM)
            return tt.transpose(0, 3, 2, 1, 4).reshape(batch * hg * dil, length, HEAD_DIM)

        o, lse = _dil_group(lattice(qk, g * hg * HEAD_DIM),
                            lattice(qk, DIL_W + g * hg * HEAD_DIM),
                            lattice(v, g * hg * HEAD_DIM))
        o = o.reshape(batch, hg, dil, length, HEAD_DIM).transpose(0, 3, 2, 1, 4)
        lse = lse.reshape(batch, hg, dil, length).transpose(0, 3, 2, 1)
        outs.append(o.reshape(batch * seq, hg * HEAD_DIM))
        lses.append(lse.reshape(batch * seq, hg))
    return _dil_mix(outs, lses)


def _mla_q_kernel(a_ref, w_ref, g_ref, cos_ref, sin_ref, o_ref):
    a = a_ref[...]
    gain = g_ref[...]
    cos_t, sin_t = cos_ref[...], sin_ref[...]
    for h in range(MLA_HEADS):
        acc = _dot(a, w_ref[h])
        ms = jnp.sum(acc * acc, axis=-1, keepdims=True) * (1.0 / MLA_QK)
        y = acc * lax.rsqrt(ms + RMS_EPS) * gain
        o_ref[h, :, :MLA_NOPE] = y[:, :MLA_NOPE].astype(o_ref.dtype)
        o_ref[h, :, MLA_NOPE:] = _rotate_pairs(y[:, MLA_NOPE:], cos_t, sin_t,
                                               MLA_ROPE // 2).astype(o_ref.dtype)


def _mla_q_proj(cq_n, w_uq_h, gain, cos_t, sin_t, seq, tm=512):
    t, rank = cq_n.shape
    per = seq // tm
    return pl.pallas_call(
        _mla_q_kernel,
        grid=(t // tm,),
        in_specs=[pl.BlockSpec((tm, rank), lambda i: (i, 0)),
                  pl.BlockSpec((MLA_HEADS, rank, MLA_QK_PAD), lambda i: (0, 0, 0)),
                  pl.BlockSpec((1, MLA_QK_PAD), lambda i: (0, 0)),
                  pl.BlockSpec((tm, LANES), lambda i: (i % per, 0)),
                  pl.BlockSpec((tm, LANES), lambda i: (i % per, 0))],
        out_specs=pl.BlockSpec((MLA_HEADS, tm, MLA_QK_PAD), lambda i: (0, i, 0)),
        out_shape=jax.ShapeDtypeStruct((MLA_HEADS, t, MLA_QK_PAD), BF16),
        compiler_params=_params("parallel"),
        name="mla_q_proj",
    )(cq_n, w_uq_h, gain, cos_t, sin_t)


def _mla_kv_kernel(a_ref, w_ref, kr_ref, g0_ref, g1_ref, cos_ref, sin_ref, k_ref, v_ref):
    a = a_ref[...]
    kr = kr_ref[...]
    kr_ss = jnp.sum(kr * kr, axis=-1, keepdims=True)
    g0, g1 = g0_ref[...], g1_ref[...]
    cos_t, sin_t = cos_ref[...], sin_ref[...]
    ones = jnp.ones((a.shape[0], MLA_V), v_ref.dtype)
    for h in range(MLA_HEADS):
        acc = _dot(a, w_ref[h])
        kn = acc[:, :MLA_NOPE]
        ms = (jnp.sum(kn * kn, axis=-1, keepdims=True) + kr_ss) * (1.0 / MLA_QK)
        inv = lax.rsqrt(ms + RMS_EPS)
        k_ref[h, :, :MLA_NOPE] = (kn * inv * g0).astype(k_ref.dtype)
        k_ref[h, :, MLA_NOPE:] = _rotate_pairs(kr * inv * g1, cos_t, sin_t,
                                               MLA_ROPE // 2).astype(k_ref.dtype)
        v_ref[h, :, :MLA_V] = acc[:, MLA_NOPE:].astype(v_ref.dtype)
        v_ref[h, :, MLA_V:] = ones


def _mla_kv_proj(ckv_n, w_ukv_h, k_r, g0, g1, cos_t, sin_t, seq, tm=512):
    t, rank = ckv_n.shape
    per = seq // tm
    return pl.pallas_call(
        _mla_kv_kernel,
        grid=(t // tm,),
        in_specs=[pl.BlockSpec((tm, rank), lambda i: (i, 0)),
                  pl.BlockSpec((MLA_HEADS, rank, MLA_NOPE + MLA_V), lambda i: (0, 0, 0)),
                  pl.BlockSpec((tm, LANES), lambda i: (i, 0)),
                  pl.BlockSpec((1, LANES), lambda i: (0, 0)),
                  pl.BlockSpec((1, LANES), lambda i: (0, 0)),
                  pl.BlockSpec((tm, LANES), lambda i: (i % per, 0)),
                  pl.BlockSpec((tm, LANES), lambda i: (i % per, 0))],
        out_specs=[pl.BlockSpec((MLA_HEADS, tm, MLA_QK_PAD), lambda i: (0, i, 0)),
                   pl.BlockSpec((MLA_HEADS, tm, 2 * MLA_V), lambda i: (0, i, 0))],
        out_shape=[jax.ShapeDtypeStruct((MLA_HEADS, t, MLA_QK_PAD), BF16),
                   jax.ShapeDtypeStruct((MLA_HEADS, t, 2 * MLA_V), BF16)],
        compiler_params=_params("parallel"),
        name="mla_kv_proj",
    )(ckv_n, w_ukv_h, k_r, g0, g1, cos_t, sin_t)


MLA_KV_CHUNKS = 8


def _mla_attn_kernel(q_ref, k_ref, v_ref, o_ref):
    q = q_ref[...]
    tk = k_ref.shape[0] // MLA_KV_CHUNKS
    m = acc = None
    for c in range(MLA_KV_CHUNKS):
        s = _dot_nt(q, k_ref[c * tk:(c + 1) * tk, :])
        m_c = jnp.max(s, axis=-1, keepdims=True)
        if c == 0:
            m = m_c
            acc = _dot(jnp.exp(s - m).astype(BF16), v_ref[:tk, :])
        else:
            m_new = jnp.maximum(m, m_c)
            acc = jnp.exp(m - m_new) * acc + _dot(jnp.exp(s - m_new).astype(BF16),
                                                  v_ref[c * tk:(c + 1) * tk, :])
            m = m_new
    o_ref[...] = (acc[:, :MLA_V] / acc[:, MLA_V:]).astype(o_ref.dtype)


def _mla_attention(q, k, v, batch, seq, tq=512):
    t = q.shape[1]
    nq = seq // tq
    return pl.pallas_call(
        _mla_attn_kernel,
        grid=(batch, MLA_HEADS, nq),
        in_specs=[pl.BlockSpec((None, tq, MLA_QK_PAD), lambda b, h, i: (h, b * nq + i, 0)),
                  pl.BlockSpec((None, seq, MLA_QK_PAD), lambda b, h, i: (h, b, 0)),
                  pl.BlockSpec((None, seq, 2 * MLA_V), lambda b, h, i: (h, b, 0))],
        out_specs=pl.BlockSpec((tq, MLA_V), lambda b, h, i: (b * nq + i, h)),
        out_shape=jax.ShapeDtypeStruct((t, MLA_HEADS * MLA_V), BF16),
        compiler_params=_params("parallel", "parallel", "arbitrary"),
        name="mla_attention",
    )(q, k, v)


def _merge_kernel(ona_ref, odl_ref, omla_ref, wna_ref, wdl_ref, wmla_ref, g0_ref, g1_ref, g2_ref, o_ref):
    acc = g0_ref[...].astype(F32) * _dot(ona_ref[...], wna_ref[...])
    acc = acc + g1_ref[...].astype(F32) * _dot(odl_ref[...], wdl_ref[...])
    acc = acc + g2_ref[...].astype(F32) * _dot(omla_ref[...], wmla_ref[...])
    o_ref[...] = acc.astype(o_ref.dtype)


def _merge(o_na, o_dl, o_mla, w_na, w_dl, w_mla, gates, tm=512, tn=512):
    t = o_na.shape[0]
    d = w_na.shape[1]
    nj = d // tn

    def act(a):
        return pl.BlockSpec((tm, a.shape[1]), lambda i, j: (i, 0))

    def wgt(w):
        return pl.BlockSpec((w.shape[0], tn), lambda i, j: (0, j))

    def gate(idx):
        return pl.BlockSpec((tm, tn), lambda i, j: (i, idx * nj + j))

    return pl.pallas_call(
        _merge_kernel,
        grid=(t // tm, nj),
        in_specs=[act(o_na), act(o_dl), act(o_mla), wgt(w_na), wgt(w_dl), wgt(w_mla),
                  gate(0), gate(1), gate(2)],
        out_specs=pl.BlockSpec((tm, tn), lambda i, j: (i, j)),
        out_shape=jax.ShapeDtypeStruct((t, d), BF16),
        compiler_params=_params("parallel", "parallel"),
        name="branch_merge",
    )(o_na, o_dl, o_mla, w_na, w_dl, w_mla, gates, gates, gates)


GU_SPLIT = 256


def _swiglu_pieces(h, wg_ref, wu_ref, o_ref):
    for c in range(o_ref.shape[1] // GU_SPLIT):
        cols = slice(c * GU_SPLIT, (c + 1) * GU_SPLIT)
        g = _dot(h, wg_ref[:, cols].astype(BF16))
        u = _dot(h, wu_ref[:, cols].astype(BF16))
        o_ref[:, cols] = (g * jax.nn.sigmoid(g) * u).astype(o_ref.dtype)


def _gu_kernel(h_ref, wg_ref, wu_ref, o_ref):
    _swiglu_pieces(h_ref[...], wg_ref, wu_ref, o_ref)


def _swiglu_up(h, w_gu, tm=1024, tn=512):
    t, d = h.shape
    ff = w_gu.shape[1] // 2
    nj = ff // tn
    return pl.pallas_call(
        _gu_kernel,
        grid=(t // tm, nj),
        in_specs=[pl.BlockSpec((tm, d), lambda i, j: (i, 0)),
                  pl.BlockSpec((d, tn), lambda i, j: (0, j)),
                  pl.BlockSpec((d, tn), lambda i, j: (0, nj + j))],
        out_specs=pl.BlockSpec((tm, tn), lambda i, j: (i, j)),
        out_shape=jax.ShapeDtypeStruct((t, ff), BF16),
        compiler_params=_params("parallel", "parallel"),
        name="swiglu_up",
    )(h, w_gu, w_gu)


MOE_TM = 1024
TOP_K = 2


def _moe_routing(route, tokens):
    e_flat = jnp.concatenate([route[:, N_EXPERTS], route[:, N_EXPERTS + 1]]).astype(jnp.int32)
    onehot = (e_flat[:, None] == jnp.arange(N_EXPERTS, dtype=jnp.int32)[None, :]).astype(jnp.int32)
    csum = jnp.cumsum(onehot, axis=0)
    rank = jnp.sum((csum - onehot) * onehot, axis=1)
    padded = ((csum[-1] + MOE_TM - 1) // MOE_TM) * MOE_TM
    ends = jnp.cumsum(padded)
    slot = jnp.sum(onehot * (ends - padded)[None, :], axis=1) + rank
    n_tiles = (TOP_K * tokens) // MOE_TM + N_EXPERTS
    tile_start = jnp.arange(n_tiles, dtype=jnp.int32) * MOE_TM
    tile_expert = jnp.minimum(jnp.sum(tile_start[:, None] >= ends[None, :], axis=1), N_EXPERTS - 1)
    te = jnp.concatenate([tile_expert, ends[-1:] // MOE_TM]).astype(jnp.int32)
    return slot.astype(jnp.int32), te


def _dispatch_kernel(slot_ref, x_ref, g_ref, sc_ref, sh_ref, dst_in_ref, dst_ref, h_ref, sem, *, rows, tokens):
    del dst_in_ref
    base = pl.program_id(0) * rows
    h_ref[...] = _modulated_norm(x_ref[...], g_ref[...], sc_ref[...], sh_ref[...])

    def row_copy(r, choice):
        slot = slot_ref[choice * tokens + base + r]
        return pltpu.make_async_copy(h_ref.at[pl.ds(r, 1), :], dst_ref.at[pl.ds(slot, 1), :], sem)

    def start(r, carry):
        row_copy(r, 0).start()
        row_copy(r, 1).start()
        return carry

    def wait(r, carry):
        row_copy(r, 0).wait()
        row_copy(r, 1).wait()
        return carry

    lax.fori_loop(0, rows, start, 0, unroll=8)
    lax.fori_loop(0, rows, wait, 0, unroll=8)


def _moe_dispatch(x, g, sc, sh, slot, seq, n_rows, rows=256):
    t, d = x.shape
    per = seq // rows
    return pl.pallas_call(
        functools.partial(_dispatch_kernel, rows=rows, tokens=t),
        grid_spec=pltpu.PrefetchScalarGridSpec(
            num_scalar_prefetch=1,
            grid=(t // rows,),
            in_specs=[pl.BlockSpec((rows, d), lambda i, s: (i, 0)),
                      pl.BlockSpec((1, d), lambda i, s: (0, 0)),
                      pl.BlockSpec((None, 1, d), lambda i, s: (i // per, 0, 0)),
                      pl.BlockSpec((None, 1, d), lambda i, s: (i // per, 0, 0)),
                      pl.BlockSpec(memory_space=pl.ANY)],
            out_specs=pl.BlockSpec(memory_space=pl.ANY),
            scratch_shapes=[pltpu.VMEM((rows, d), F32), pltpu.SemaphoreType.DMA(())]),
        out_shape=jax.ShapeDtypeStruct((n_rows, d), F32),
        input_output_aliases={5: 0},
        compiler_params=_params("arbitrary"),
        name="moe_dispatch",
    )(slot, x, g.reshape(1, d), sc, sh, jnp.zeros((n_rows, d), F32))


def _moe_up_kernel(te_ref, xs_ref, wg_ref, wu_ref, o_ref, hb_ref, *, n_tiles):
    used = pl.program_id(0) < te_ref[n_tiles]

    @pl.when(used & (pl.program_id(1) == 0))
    def _():
        hb_ref[...] = xs_ref[...].astype(BF16)

    @pl.when(used)
    def _():
        _swiglu_pieces(hb_ref[...], wg_ref, wu_ref, o_ref)

    @pl.when(jnp.logical_not(used))
    def _():
        o_ref[...] = jnp.zeros(o_ref.shape, o_ref.dtype)


def _moe_up(xs, w_gu, te, tn=512):
    n_rows, d = xs.shape
    ff = w_gu.shape[2] // 2
    nj = ff // tn
    n_tiles = n_rows // MOE_TM

    def last_used(i, te_ref):
        return jnp.minimum(i, te_ref[n_tiles] - 1)

    return pl.pallas_call(
        functools.partial(_moe_up_kernel, n_tiles=n_tiles),
        grid_spec=pltpu.PrefetchScalarGridSpec(
            num_scalar_prefetch=1,
            grid=(n_tiles, nj),
            in_specs=[pl.BlockSpec((MOE_TM, d), lambda i, j, te_ref: (last_used(i, te_ref), 0)),
                      pl.BlockSpec((None, d, tn), lambda i, j, te_ref: (te_ref[i], 0, j)),
                      pl.BlockSpec((None, d, tn), lambda i, j, te_ref: (te_ref[i], 0, nj + j))],
            out_specs=pl.BlockSpec((MOE_TM, tn), lambda i, j, te_ref: (i, j)),
            scratch_shapes=[pltpu.VMEM((MOE_TM, d), BF16)]),
        out_shape=jax.ShapeDtypeStruct((n_rows, ff), BF16),
        compiler_params=_params("parallel", "arbitrary"),
        name="moe_up",
    )(te, xs, w_gu, w_gu)


def _moe_down_kernel(te_ref, a_ref, w_ref, o_ref, *, n_tiles):
    used = pl.program_id(0) < te_ref[n_tiles]

    @pl.when(used)
    def _():
        o_ref[...] = _dot(a_ref[...], w_ref[...])

    @pl.when(jnp.logical_not(used))
    def _():
        o_ref[...] = jnp.zeros(o_ref.shape, o_ref.dtype)


def _moe_down(act, w_down, te, tn=256):
    n_rows, ff = act.shape
    d = w_down.shape[2]
    n_tiles = n_rows // MOE_TM

    def last_used(i, te_ref):
        return jnp.minimum(i, te_ref[n_tiles] - 1)

    return pl.pallas_call(
        functools.partial(_moe_down_kernel, n_tiles=n_tiles),
        grid_spec=pltpu.PrefetchScalarGridSpec(
            num_scalar_prefetch=1,
            grid=(n_tiles, d // tn),
            in_specs=[pl.BlockSpec((MOE_TM, ff), lambda i, j, te_ref: (last_used(i, te_ref), 0)),
                      pl.BlockSpec((None, ff, tn), lambda i, j, te_ref: (te_ref[i], 0, j))],
            out_specs=pl.BlockSpec((MOE_TM, tn), lambda i, j, te_ref: (i, j))),
        out_shape=jax.ShapeDtypeStruct((n_rows, d), F32),
        compiler_params=_params("parallel", "arbitrary"),
        name="moe_down",
    )(te, act, w_down)


def _combine_kernel(slot_ref, x_ref, gate_ref, w_ref, ys_ref, o_ref, buf_ref, sem, *, rows, tokens):
    base = pl.program_id(0) * rows

    def row_copy(r, choice):
        slot = slot_ref[choice * tokens + base + r]
        return pltpu.make_async_copy(ys_ref.at[pl.ds(slot, 1), :], buf_ref.at[choice, pl.ds(r, 1), :], sem)

    def start(r, carry):
        row_copy(r, 0).start()
        row_copy(r, 1).start()
        return carry

    def wait(r, carry):
        row_copy(r, 0).wait()
        row_copy(r, 1).wait()
        return carry

    lax.fori_loop(0, rows, start, 0, unroll=8)
    lax.fori_loop(0, rows, wait, 0, unroll=8)
    w = w_ref[...]
    y = w[:, 0:1] * buf_ref[0] + w[:, 1:2] * buf_ref[1]
    o_ref[...] = x_ref[...] + gate_ref[...] * y


def _moe_combine(x, gate, w12, ys, slot, seq, rows=256):
    t, d = x.shape
    per = seq // rows
    return pl.pallas_call(
        functools.partial(_combine_kernel, rows=rows, tokens=t),
        grid_spec=pltpu.PrefetchScalarGridSpec(
            num_scalar_prefetch=1,
            grid=(t // rows,),
            in_specs=[pl.BlockSpec((rows, d), lambda i, s: (i, 0)),
                      pl.BlockSpec((None, 1, d), lambda i, s: (i // per, 0, 0)),
                      pl.BlockSpec((rows, TOP_K), lambda i, s: (i, 0)),
                      pl.BlockSpec(memory_space=pl.ANY)],
            out_specs=pl.BlockSpec((rows, d), lambda i, s: (i, 0)),
            scratch_shapes=[pltpu.VMEM((TOP_K, rows, d), F32), pltpu.SemaphoreType.DMA(())]),
        out_shape=jax.ShapeDtypeStruct((t, d), F32),
        compiler_params=_params("arbitrary"),
        name="moe_combine",
    )(slot, x, gate, w12, ys)


def _router_kernel(x_ref, g_ref, sc_ref, sh_ref, w_ref, o_ref):
    h = _modulated_norm(x_ref[...], g_ref[...], sc_ref[...], sh_ref[...])
    logits = jnp.dot(h, w_ref[...], preferred_element_type=F32, precision=lax.Precision.HIGHEST)
    lane = lax.broadcasted_iota(jnp.int32, logits.shape, 1).astype(F32)
    lg = jnp.where(lane < N_EXPERTS, logits, NEG_INF)
    m1 = jnp.max(lg, axis=-1, keepdims=True)
    i1 = jnp.min(jnp.where(lg == m1, lane, float(LANES)), axis=-1, keepdims=True)
    lg2 = jnp.where(lane == i1, NEG_INF, lg)
    m2 = jnp.max(lg2, axis=-1, keepdims=True)
    i2 = jnp.min(jnp.where(lg2 == m2, lane, float(LANES)), axis=-1, keepdims=True)
    e2 = jnp.exp(m2 - m1)
    z = 1.0 + e2
    out = jnp.where(lane == N_EXPERTS, i1, 0.0) + jnp.where(lane == N_EXPERTS + 1, i2, 0.0)
    out = out + jnp.where(lane == N_EXPERTS + 2, 1.0 / z, 0.0) + jnp.where(lane == N_EXPERTS + 3, e2 / z, 0.0)
    o_ref[...] = out


def _router(x, g, sc, sh, w_router_pad, seq, tm=512):
    t, d = x.shape
    per = seq // tm
    return pl.pallas_call(
        _router_kernel,
        grid=(t // tm,),
        in_specs=[pl.BlockSpec((tm, d), lambda i: (i, 0)),
                  pl.BlockSpec((1, d), lambda i: (0, 0)),
                  pl.BlockSpec((None, 1, d), lambda i: (i // per, 0, 0)),
                  pl.BlockSpec((None, 1, d), lambda i: (i // per, 0, 0)),
                  pl.BlockSpec((d, LANES), lambda i: (0, 0))],
        out_specs=pl.BlockSpec((tm, LANES), lambda i: (i, 0)),
        out_shape=jax.ShapeDtypeStruct((t, LANES), F32),
        compiler_params=_params("parallel"),
        name="router",
    )(x, g.reshape(1, d), sc, sh, w_router_pad)


MM_TM = 1024
MM_TN = 512
MM_SPLIT = 256


def _residual_matmul(a, w, x, gate, seq, name, tm=MM_TM, tn=MM_TN):
    per = seq // tm
    return _matmul(a, w, tm=tm, tn=tn, out_dtype=F32, epilogue=_ep_residual, split=MM_SPLIT,
                   extras=[(x, (tm, tn), lambda i, j: (i, j)),
                           (gate, (None, 1, tn), lambda i, j: (i // per, 0, j))],
                   name=name)


def _token_mixer(h, x, gate, lw, batch, seq, tables):
    d = h.shape[1]
    w_in = lw["w_in"]
    tm, tn = MM_TM, MM_TN
    per = seq // tm
    cos_d, sin_d, cos_m, sin_m = tables

    def cols(lo, hi):
        return w_in[:, lo:hi].astype(BF16)

    def tile_gain(gq, gk, heads):
        return jnp.concatenate([jnp.tile(gq, heads), jnp.tile(gk, heads)]).reshape(1, -1).astype(F32)

    o = 0
    na_qk = _matmul(h, cols(o, o + 2 * NA_W), tm=tm, tn=tn, out_dtype=BF16, epilogue=_ep_headnorm,
                    split=MM_SPLIT,
                    extras=[(tile_gain(lw["na_q_norm"], lw["na_k_norm"], NA_HEADS), (1, tn),
                             lambda i, j: (0, j))], name="na_qk_proj")
    o += 2 * NA_W
    na_v = _matmul(h, cols(o, o + NA_W), tm=tm, tn=tn, out_dtype=BF16, split=MM_SPLIT, name="na_v_proj")
    o += NA_W
    dl_qk = _matmul(h, cols(o, o + 2 * DIL_W), tm=tm, tn=tn, out_dtype=BF16, epilogue=_ep_headnorm_rope,
                    split=MM_SPLIT,
                    extras=[(tile_gain(lw["dil_q_norm"], lw["dil_k_norm"], DIL_HEADS), (1, tn),
                             lambda i, j: (0, j)),
                            (cos_d, (tm, LANES), lambda i, j: (i % per, 0)),
                            (sin_d, (tm, LANES), lambda i, j: (i % per, 0)),
                            (_pair_swap_matrix(MM_SPLIT, ROT_DIM // 2), (MM_SPLIT, MM_SPLIT),
                             lambda i, j: (0, 0))], name="dil_qk_proj")
    o += 2 * DIL_W
    dl_v = _matmul(h, cols(o, o + DIL_W), tm=tm, tn=tn, out_dtype=BF16, split=MM_SPLIT, name="dil_v_proj")
    o += DIL_W
    cq_n = _matmul(h, cols(o, o + MLA_Q_RANK), tm=tm, tn=MLA_Q_RANK, out_dtype=BF16, epilogue=_ep_rownorm,
                   extras=[(lw["mla_q_a_norm"].reshape(1, -1), (1, MLA_Q_RANK), lambda i, j: (0, 0))],
                   name="mla_cq_proj")
    o += MLA_Q_RANK
    ckv_n = _matmul(h, cols(o, o + MLA_KV_RANK), tm=tm, tn=MLA_KV_RANK, out_dtype=BF16, epilogue=_ep_rownorm,
                    extras=[(lw["mla_kv_a_norm"].reshape(1, -1), (1, MLA_KV_RANK), lambda i, j: (0, 0))],
                    name="mla_ckv_proj")
    o += MLA_KV_RANK
    w_kr = jnp.pad(w_in[:, o:o + MLA_ROPE], ((0, 0), (0, LANES - MLA_ROPE))).astype(BF16)
    k_r = _matmul(h, w_kr, tm=tm, tn=LANES, out_dtype=F32, name="mla_kr_proj")
    o += MLA_ROPE
    gates = _matmul(h, cols(o, o + 3 * d), tm=tm, tn=tn, out_dtype=BF16, epilogue=_ep_sigmoid,
                    split=MM_SPLIT, name="gate_proj")

    o_na = _na_attention(na_qk, na_v, lw["na_rpb"], batch, seq)
    o_dl = _dilated_attention(dl_qk, dl_v, batch, seq)
    w_uq = lw["mla_w_uq"].reshape(MLA_Q_RANK, MLA_HEADS, MLA_QK).transpose(1, 0, 2)
    w_uq = jnp.pad(w_uq, ((0, 0), (0, 0), (0, MLA_QK_PAD - MLA_QK))).astype(BF16)
    w_ukv = lw["mla_w_ukv"].reshape(MLA_KV_RANK, MLA_HEADS, MLA_NOPE + MLA_V).transpose(1, 0, 2).astype(BF16)
    gq = jnp.pad(lw["mla_q_norm"] * (1.0 / math.sqrt(MLA_QK)), (0, MLA_QK_PAD - MLA_QK)).reshape(1, MLA_QK_PAD)
    gk = lw["mla_k_norm"]
    gk0 = gk[:MLA_NOPE].reshape(1, LANES)
    gk1 = jnp.pad(gk[MLA_NOPE:], (0, LANES - MLA_ROPE)).reshape(1, LANES)
    q_m = _mla_q_proj(cq_n, w_uq, gq, cos_m, sin_m, seq)
    k_m, v_m = _mla_kv_proj(ckv_n, w_ukv, k_r, gk0, gk1, cos_m, sin_m, seq)
    o_mla = _mla_attention(q_m, k_m, v_m, batch, seq)

    merged = _merge(o_na, o_dl, o_mla, lw["w_branch_na"].astype(BF16), lw["w_branch_dil"].astype(BF16),
                    lw["w_branch_mla"].astype(BF16), gates)
    return _residual_matmul(merged, lw["w_out"].astype(BF16), x, gate, seq, "out_proj")


def kernel(x, c, w_ada, b_ada, norm_mix, norm_ffn, w_in, na_q_norm, na_k_norm, na_rpb, dil_q_norm, dil_k_norm, mla_q_a_norm, mla_w_uq, mla_kv_a_norm, mla_w_ukv, mla_q_norm, mla_k_norm, w_branch_na, w_branch_dil, w_branch_mla, w_out, ffn_w_gu, ffn_w_down, moe_router, moe_w_gu, moe_w_down):
    batch, seq, d = x.shape
    depth = w_ada.shape[0]
    assert seq % MM_TM == 0 and seq % GRID_W == 0
    t = batch * seq
    xf = x.reshape(t, d)

    c_pad = jnp.pad(c, ((0, (-batch) % 8), (0, 0)))
    mod = _adaln(c_pad, w_ada, b_ada)[:, :batch].reshape(depth, batch, 6, 1, d)
    tables = _rope_tables(seq, ROT_DIM, 1.0) + _rope_tables(seq, MLA_ROPE, 1.0)

    for l in range(depth):
        sh1, sc1, g1, sh2, sc2, g2 = [mod[l, :, i] for i in range(6)]
        lw = dict(w_in=w_in[l], na_q_norm=na_q_norm[l], na_k_norm=na_k_norm[l], na_rpb=na_rpb[l],
                  dil_q_norm=dil_q_norm[l], dil_k_norm=dil_k_norm[l], mla_q_a_norm=mla_q_a_norm[l],
                  mla_w_uq=mla_w_uq[l], mla_kv_a_norm=mla_kv_a_norm[l], mla_w_ukv=mla_w_ukv[l],
                  mla_q_norm=mla_q_norm[l], mla_k_norm=mla_k_norm[l], w_branch_na=w_branch_na[l],
                  w_branch_dil=w_branch_dil[l], w_branch_mla=w_branch_mla[l], w_out=w_out[l])
        h = _normmod(xf, norm_mix[l], sc1, sh1, seq)
        xf = _token_mixer(h, xf, g1, lw, batch, seq, tables)

        if l % 2 == 0:
            h = _normmod(xf, norm_ffn[l], sc2, sh2, seq)
            act = _swiglu_up(h, ffn_w_gu[l // 2].astype(BF16))
            xf = _residual_matmul(act, ffn_w_down[l // 2].astype(BF16), xf, g2, seq, "down_proj", tm=512)
        else:
            w_r = jnp.pad(moe_router[l // 2], ((0, 0), (0, LANES - N_EXPERTS)))
            route = _router(xf, norm_ffn[l], sc2, sh2, w_r, seq)
            slot, te = _moe_routing(route, t)
            n_rows = TOP_K * t + N_EXPERTS * MOE_TM
            xs = _moe_dispatch(xf, norm_ffn[l], sc2, sh2, slot, seq, n_rows)
            act = _moe_up(xs, moe_w_gu[l // 2], te)
            ys = _moe_down(act, moe_w_down[l // 2].astype(BF16), te)
            xf = _moe_combine(xf, g2, route[:, N_EXPERTS + 2:N_EXPERTS + 4], ys, slot, seq)
    return xf.reshape(batch, seq, d)
```

```python
import functools
import math

import numpy as np
import jax
import jax.numpy as jnp
from jax import lax
from jax.experimental import pallas as pl
from jax.experimental.pallas import tpu as pltpu

GRID_W = 64
HEAD_DIM = 128
ROPE_THETA = 500000.0
ROT_DIM = HEAD_DIM // 4
RMS_EPS = 1e-6
NEG_INF = -1e30
NA_HEADS = 8
NA_KH = 8
NA_KW = 16
DIL_GROUPS = ((128, 1), (512, 4), (2048, 16))
DIL_HEADS_PER_GROUP = 4
DIL_HEADS = DIL_HEADS_PER_GROUP * len(DIL_GROUPS)
DIL_RADIUS = 64
MLA_HEADS = 8
MLA_NOPE = 128
MLA_ROPE = 64
MLA_V = 128
MLA_Q_RANK = 768
MLA_KV_RANK = 512
MLA_QK = MLA_NOPE + MLA_ROPE
MLA_QK_PAD = 256
N_EXPERTS = 8
LANES = 128

NA_W = NA_HEADS * HEAD_DIM
DIL_W = DIL_HEADS * HEAD_DIM
DIL_OUT = DIL_HEADS_PER_GROUP * HEAD_DIM

VMEM_LIMIT_BYTES = 48 * 1024 * 1024

F32 = jnp.float32
BF16 = jnp.bfloat16


def _params(*sem):
    return pltpu.CompilerParams(dimension_semantics=sem, vmem_limit_bytes=VMEM_LIMIT_BYTES)


def _dot(a, b):
    return jnp.dot(a, b, preferred_element_type=F32)


def _dot_nt(a, b):
    return lax.dot_general(a, b, (((1,), (1,)), ((), ())), preferred_element_type=F32)


def _adaln_kernel(c_ref, w_ref, b_ref, o_ref):
    o_ref[...] = _dot(c_ref[...], w_ref[...]) + b_ref[...]


def _adaln(c_pad, w_ada, b_ada):
    depth, d, n = w_ada.shape
    tn = 1024
    return pl.pallas_call(
        _adaln_kernel,
        grid=(depth, n // tn),
        in_specs=[pl.BlockSpec(c_pad.shape, lambda l, j: (0, 0)),
                  pl.BlockSpec((None, d, tn), lambda l, j: (l, 0, j)),
                  pl.BlockSpec((None, 1, tn), lambda l, j: (l, 0, j))],
        out_specs=pl.BlockSpec((None, c_pad.shape[0], tn), lambda l, j: (l, 0, j)),
        out_shape=jax.ShapeDtypeStruct((depth, c_pad.shape[0], n), F32),
        compiler_params=_params("parallel", "parallel"),
        name="adaln",
    )(c_pad, w_ada, b_ada.reshape(depth, 1, n))


def _modulated_norm(x, g, sc, sh):
    ms = jnp.mean(x * x, axis=-1, keepdims=True)
    return (x * lax.rsqrt(ms + RMS_EPS) * g) * (1.0 + sc) + sh


def _normmod_kernel(x_ref, g_ref, sc_ref, sh_ref, o_ref):
    o_ref[...] = _modulated_norm(x_ref[...], g_ref[...], sc_ref[...], sh_ref[...]).astype(o_ref.dtype)


def _normmod(x, g, sc, sh, seq, tm=512):
    t, d = x.shape
    per = seq // tm
    return pl.pallas_call(
        _normmod_kernel,
        grid=(t // tm,),
        in_specs=[pl.BlockSpec((tm, d), lambda i: (i, 0)),
                  pl.BlockSpec((1, d), lambda i: (0, 0)),
                  pl.BlockSpec((None, 1, d), lambda i: (i // per, 0, 0)),
                  pl.BlockSpec((None, 1, d), lambda i: (i // per, 0, 0))],
        out_specs=pl.BlockSpec((tm, d), lambda i: (i, 0)),
        out_shape=jax.ShapeDtypeStruct((t, d), BF16),
        compiler_params=_params("parallel"),
        name="normmod",
    )(x, g.reshape(1, d), sc, sh)


def _mm_kernel(*refs, n_extra, epilogue, split):
    a_ref, b_ref = refs[0], refs[1]
    extras = refs[2:2 + n_extra]
    o_ref = refs[2 + n_extra]
    a = a_ref[...]
    for c in range(b_ref.shape[1] // split):
        cols = slice(c * split, (c + 1) * split)
        epilogue(_dot(a, b_ref[:, cols]), extras, o_ref, cols)


def _ep_store(acc, extras, o_ref, cols):
    o_ref[:, cols] = acc.astype(o_ref.dtype)


def _matmul(a, b, *, tm, tn, out_dtype, epilogue=_ep_store, extras=(), split=None, name):
    m, kdim = a.shape
    n = b.shape[1]
    split = tn if split is None else split
    assert m % tm == 0 and n % tn == 0 and tn % split == 0
    in_specs = [pl.BlockSpec((tm, kdim), lambda i, j: (i, 0)),
                pl.BlockSpec((kdim, tn), lambda i, j: (0, j))]
    in_specs += [pl.BlockSpec(bs, im) for (_, bs, im) in extras]
    return pl.pallas_call(
        functools.partial(_mm_kernel, n_extra=len(extras), epilogue=epilogue, split=split),
        grid=(m // tm, n // tn),
        in_specs=in_specs,
        out_specs=pl.BlockSpec((tm, tn), lambda i, j: (i, j)),
        out_shape=jax.ShapeDtypeStruct((m, n), out_dtype),
        compiler_params=_params("parallel", "parallel"),
        name=name,
    )(a, b, *[e[0] for e in extras])


def _head_rms(blk, gain):
    ms = jnp.mean(blk * blk, axis=-1, keepdims=True)
    return blk * lax.rsqrt(ms + RMS_EPS) * gain


def _head_slices(cols):
    return [slice(c, c + HEAD_DIM) for c in range(cols.start, cols.stop, HEAD_DIM)]


def _ep_headnorm(acc, extras, o_ref, cols):
    for i, sl in enumerate(_head_slices(cols)):
        blk = acc[:, i * HEAD_DIM:(i + 1) * HEAD_DIM]
        o_ref[:, sl] = _head_rms(blk, extras[0][:, sl]).astype(o_ref.dtype)


def _rotate_pairs(y, cos_t, sin_t, half):
    lane = lax.broadcasted_iota(jnp.int32, y.shape, 1)
    swapped = jnp.where(lane < half, pltpu.roll(y, LANES - half, 1), pltpu.roll(y, half, 1))
    return y * cos_t + swapped * sin_t


def _pair_swap_matrix(width, half):
    p = np.zeros((width, width), np.float32)
    for base in range(0, width, HEAD_DIM):
        for i in range(half):
            p[base + i + half, base + i] = 1.0
            p[base + i, base + i + half] = 1.0
    return jnp.asarray(p, BF16)


def _ep_headnorm_rope(acc, extras, o_ref, cols):
    cos_t = extras[1][...]
    sin_t = extras[2][...]
    heads = _head_slices(cols)
    y = jnp.concatenate([_head_rms(acc[:, i * HEAD_DIM:(i + 1) * HEAD_DIM], extras[0][:, sl])
                         for i, sl in enumerate(heads)], axis=1)
    swapped = _dot(y.astype(BF16), extras[3][...])
    for i, sl in enumerate(heads):
        loc = slice(i * HEAD_DIM, (i + 1) * HEAD_DIM)
        o_ref[:, sl] = (y[:, loc] * cos_t + swapped[:, loc] * sin_t).astype(o_ref.dtype)


def _ep_rownorm(acc, extras, o_ref, cols):
    o_ref[:, cols] = _head_rms(acc, extras[0][:, cols]).astype(o_ref.dtype)


def _ep_sigmoid(acc, extras, o_ref, cols):
    o_ref[:, cols] = jax.nn.sigmoid(acc).astype(o_ref.dtype)


def _ep_residual(acc, extras, o_ref, cols):
    o_ref[:, cols] = extras[0][:, cols] + extras[1][:, cols] * acc


def _rope_tables(seq, rot, fill):
    half = rot // 2
    inv = ROPE_THETA ** (-jnp.arange(half, dtype=F32) * (2.0 / rot))
    ang = jnp.arange(seq, dtype=jnp.int32).astype(F32)[:, None] * inv[None, :]
    cos, sin = jnp.cos(ang), jnp.sin(ang)
    pad = LANES - rot
    cos_t = jnp.concatenate([cos, cos, jnp.full((seq, pad), fill, F32)], axis=1)
    sin_t = jnp.concatenate([-sin, sin, jnp.zeros((seq, pad), F32)], axis=1)
    return cos_t, sin_t


NA_RB = 4
NA_KROWS = NA_KH + NA_RB


def _na_geometry(rows):
    kh = min(NA_KH, rows)
    assert rows % NA_RB == 0 and rows >= NA_KROWS
    patterns, ids, kstarts = [], [], []
    for r in range(0, rows, NA_RB):
        ks = int(np.clip(r - NA_KH // 2, 0, rows - NA_KROWS))
        pat = tuple((int(np.clip(r + a - NA_KH // 2, 0, rows - kh)) - ks, ks - (r + a)) for a in range(NA_RB))
        if pat not in patterns:
            patterns.append(pat)
        ids.append(patterns.index(pat))
        kstarts.append(ks)
    return patterns, np.array(ids + kstarts, np.int32)


def _na_bias_table(rpb, rows, patterns):
    h = rpb.shape[0]
    kh = min(NA_KH, rows)
    qc = np.arange(GRID_W)
    kc = np.arange(GRID_W)
    cstart = np.clip(qc - NA_KW // 2, 0, GRID_W - NA_KW)
    ok = (kc[None, :] >= cstart[:, None]) & (kc[None, :] < cstart[:, None] + NA_KW)
    dc = np.clip(kc[None, :] - qc[:, None] + NA_KW - 1, 0, 2 * NA_KW - 2)
    by_col = rpb.astype(F32)[:, :, dc] + jnp.where(ok, 0.0, NEG_INF).astype(F32)
    kr = np.arange(NA_KROWS)
    dr = np.zeros((len(patterns), NA_RB, NA_KROWS), np.int32)
    row_ok = np.zeros((len(patterns), NA_RB, NA_KROWS), bool)
    for p, pat in enumerate(patterns):
        for a, (start, offset) in enumerate(pat):
            row_ok[p, a] = (kr >= start) & (kr < start + kh)
            dr[p, a] = np.clip(kr + offset + NA_KH - 1, 0, 2 * NA_KH - 2)
    tab = jnp.take(by_col, jnp.asarray(dr.reshape(-1)), axis=1)
    tab = tab.reshape(h, len(patterns), NA_RB, NA_KROWS, GRID_W, GRID_W)
    tab = jnp.where(row_ok[None, :, :, :, None, None], tab, NEG_INF)
    tab = tab.transpose(0, 1, 2, 4, 3, 5)
    return tab.reshape(h, len(patterns), NA_RB * GRID_W, NA_KROWS * GRID_W)


def _na_kernel(geo_ref, q_ref, k_ref, v_ref, b_ref, o_ref, *, n_blocks):
    scale = 1.0 / math.sqrt(HEAD_DIM)
    qn = NA_RB * GRID_W
    kn = NA_KROWS * GRID_W

    def body(blk, carry):
        q0 = pl.multiple_of(blk * qn, qn)
        k0 = pl.multiple_of(geo_ref[n_blocks + blk] * GRID_W, GRID_W)
        q = q_ref[pl.ds(q0, qn), :]
        k = k_ref[pl.ds(k0, kn), :]
        v = v_ref[pl.ds(k0, kn), :]
        s = _dot_nt(q, k) * scale + b_ref[geo_ref[blk]]
        m = jnp.max(s, axis=-1, keepdims=True)
        p = jnp.exp(s - m)
        den = jnp.sum(p, axis=-1, keepdims=True)
        o = _dot(p.astype(BF16), v) / den
        o_ref[pl.ds(q0, qn), :] = o.astype(o_ref.dtype)
        return carry

    lax.fori_loop(0, n_blocks, body, 0, unroll=2)


def _na_attention(qk, v, rpb, batch, seq):
    rows = seq // GRID_W
    patterns, geo = _na_geometry(rows)
    bias = _na_bias_table(rpb, rows, patterns)
    t = qk.shape[0]
    blk = (seq, HEAD_DIM)
    return pl.pallas_call(
        functools.partial(_na_kernel, n_blocks=rows // NA_RB),
        grid_spec=pltpu.PrefetchScalarGridSpec(
            num_scalar_prefetch=1,
            grid=(batch, NA_HEADS),
            in_specs=[pl.BlockSpec(blk, lambda b, h, geo_ref: (b, h)),
                      pl.BlockSpec(blk, lambda b, h, geo_ref: (b, NA_HEADS + h)),
                      pl.BlockSpec(blk, lambda b, h, geo_ref: (b, h)),
                      pl.BlockSpec((None,) + bias.shape[1:], lambda b, h, geo_ref: (h, 0, 0, 0))],
            out_specs=pl.BlockSpec(blk, lambda b, h, geo_ref: (b, h))),
        out_shape=jax.ShapeDtypeStruct((t, NA_W), BF16),
        compiler_params=_params("parallel", "parallel"),
        name="na_attention",
    )(jnp.asarray(geo), qk, qk, v, bias)


DIL_QCHUNK = 2 * DIL_RADIUS
DIL_KWIN = 4 * DIL_RADIUS


def _dil_kernel(q_ref, k_ref, v_ref, o_ref, lse_ref, *, seq, dil):
    scale = 1.0 / math.sqrt(HEAD_DIM)
    length = seq // dil
    n_chunk = length // DIL_QCHUNK

    def rows(residue, first, count):
        return pl.ds(residue + first * dil, count, stride=dil)

    def body(it, carry):
        residue = it // n_chunk
        q0 = (it % n_chunk) * DIL_QCHUNK
        k0 = jnp.clip(q0 - DIL_RADIUS, 0, length - DIL_KWIN)
        q = q_ref[rows(residue, q0, DIL_QCHUNK), :].astype(BF16)
        k = k_ref[rows(residue, k0, DIL_KWIN), :].astype(BF16)
        v = v_ref[rows(residue, k0, DIL_KWIN), :].astype(BF16)
        s = _dot_nt(q, k) * scale
        qi = q0 + lax.broadcasted_iota(jnp.int32, s.shape, 0)
        ki = k0 + lax.broadcasted_iota(jnp.int32, s.shape, 1)
        s = s + jnp.where(jnp.abs(ki - qi) <= DIL_RADIUS, 0.0, NEG_INF)
        m = jnp.max(s, axis=-1, keepdims=True)
        e = jnp.exp(s - m)
        den = jnp.sum(e, axis=-1, keepdims=True)
        o_ref[rows(residue, q0, DIL_QCHUNK), :] = _dot(e.astype(BF16), v) / den
        lse_ref[rows(residue, q0, DIL_QCHUNK), :] = m + jnp.log(den)
        return carry

    lax.fori_loop(0, dil * n_chunk, body, 0, unroll=4)


def _dil_group(qk, v, group, dil, batch, seq):
    length = seq // dil
    assert length % DIL_QCHUNK == 0 and length >= DIL_KWIN
    hg = DIL_HEADS_PER_GROUP
    blk = (seq, HEAD_DIM)
    return pl.pallas_call(
        functools.partial(_dil_kernel, seq=seq, dil=dil),
        grid=(batch, hg),
        in_specs=[pl.BlockSpec(blk, lambda b, h: (b, group * hg + h)),
                  pl.BlockSpec(blk, lambda b, h: (b, DIL_HEADS + group * hg + h)),
                  pl.BlockSpec(blk, lambda b, h: (b, group * hg + h))],
        out_specs=[pl.BlockSpec(blk, lambda b, h: (b, h)),
                   pl.BlockSpec((None, seq, 1), lambda b, h: (b * hg + h, 0, 0))],
        out_shape=[jax.ShapeDtypeStruct((batch * seq, DIL_OUT), F32),
                   jax.ShapeDtypeStruct((batch * hg, seq, 1), F32)],
        compiler_params=_params("parallel", "parallel"),
        name="dilated_attention",
    )(qk, qk, v)


def _dil_mix_kernel(o0_ref, o1_ref, o2_ref, l0_ref, l1_ref, l2_ref, o_ref):
    l0, l1, l2 = l0_ref[...], l1_ref[...], l2_ref[...]
    mx = jnp.maximum(jnp.maximum(l0, l1), l2)
    e0, e1, e2 = jnp.exp(l0 - mx), jnp.exp(l1 - mx), jnp.exp(l2 - mx)
    z = e0 + e1 + e2
    w0, w1, w2 = e0 / z, e1 / z, e2 / z
    for h in range(DIL_HEADS_PER_GROUP):
        sl = slice(h * HEAD_DIM, (h + 1) * HEAD_DIM)
        hs = slice(h, h + 1)
        o_ref[:, sl] = (w0[:, hs] * o0_ref[:, sl] + w1[:, hs] * o1_ref[:, sl]
                        + w2[:, hs] * o2_ref[:, sl]).astype(o_ref.dtype)


def _dil_mix(outs, lses, tm=512):
    t = outs[0].shape[0]
    ospec = pl.BlockSpec((tm, DIL_OUT), lambda i: (i, 0))
    lspec = pl.BlockSpec((tm, DIL_HEADS_PER_GROUP), lambda i: (i, 0))
    return pl.pallas_call(
        _dil_mix_kernel,
        grid=(t // tm,),
        in_specs=[ospec] * 3 + [lspec] * 3,
        out_specs=ospec,
        out_shape=jax.ShapeDtypeStruct((t, DIL_OUT), BF16),
        compiler_params=_params("parallel"),
        name="dilated_mix",
    )(*outs, *lses)


def _dilated_attention(qk, v, batch, seq):
    hg = DIL_HEADS_PER_GROUP
    outs, lses = [], []
    for g, (window, dil) in enumerate(DIL_GROUPS):
        assert window // (2 * dil) == DIL_RADIUS and seq % dil == 0
        o, lse = _dil_group(qk, v, g, dil, batch, seq)
        outs.append(o)
        lses.append(lse.reshape(batch, hg, seq).transpose(0, 2, 1).reshape(batch * seq, hg))
    return _dil_mix(outs, lses)


def _mla_q_kernel(a_ref, w_ref, g_ref, cos_ref, sin_ref, o_ref):
    a = a_ref[...]
    gain = g_ref[...]
    cos_t, sin_t = cos_ref[...], sin_ref[...]
    for h in range(MLA_HEADS):
        acc = _dot(a, w_ref[h])
        ms = jnp.sum(acc * acc, axis=-1, keepdims=True) * (1.0 / MLA_QK)
        y = acc * lax.rsqrt(ms + RMS_EPS) * gain
        o_ref[h, :, :MLA_NOPE] = y[:, :MLA_NOPE].astype(o_ref.dtype)
        o_ref[h, :, MLA_NOPE:] = _rotate_pairs(y[:, MLA_NOPE:], cos_t, sin_t,
                                               MLA_ROPE // 2).astype(o_ref.dtype)


def _mla_q_proj(cq_n, w_uq_h, gain, cos_t, sin_t, seq, tm=512):
    t, rank = cq_n.shape
    per = seq // tm
    return pl.pallas_call(
        _mla_q_kernel,
        grid=(t // tm,),
        in_specs=[pl.BlockSpec((tm, rank), lambda i: (i, 0)),
                  pl.BlockSpec((MLA_HEADS, rank, MLA_QK_PAD), lambda i: (0, 0, 0)),
                  pl.BlockSpec((1, MLA_QK_PAD), lambda i: (0, 0)),
                  pl.BlockSpec((tm, LANES), lambda i: (i % per, 0)),
                  pl.BlockSpec((tm, LANES), lambda i: (i % per, 0))],
        out_specs=pl.BlockSpec((MLA_HEADS, tm, MLA_QK_PAD), lambda i: (0, i, 0)),
        out_shape=jax.ShapeDtypeStruct((MLA_HEADS, t, MLA_QK_PAD), BF16),
        compiler_params=_params("parallel"),
        name="mla_q_proj",
    )(cq_n, w_uq_h, gain, cos_t, sin_t)


def _mla_kv_kernel(a_ref, w_ref, kr_ref, g0_ref, g1_ref, cos_ref, sin_ref, k_ref, v_ref):
    a = a_ref[...]
    kr = kr_ref[...]
    kr_ss = jnp.sum(kr * kr, axis=-1, keepdims=True)
    g0, g1 = g0_ref[...], g1_ref[...]
    cos_t, sin_t = cos_ref[...], sin_ref[...]
    ones = jnp.ones((a.shape[0], MLA_V), v_ref.dtype)
    for h in range(MLA_HEADS):
        acc = _dot(a, w_ref[h])
        kn = acc[:, :MLA_NOPE]
        ms = (jnp.sum(kn * kn, axis=-1, keepdims=True) + kr_ss) * (1.0 / MLA_QK)
        inv = lax.rsqrt(ms + RMS_EPS)
        k_ref[h, :, :MLA_NOPE] = (kn * inv * g0).astype(k_ref.dtype)
        k_ref[h, :, MLA_NOPE:] = _rotate_pairs(kr * inv * g1, cos_t, sin_t,
                                               MLA_ROPE // 2).astype(k_ref.dtype)
        v_ref[h, :, :MLA_V] = acc[:, MLA_NOPE:].astype(v_ref.dtype)
        v_ref[h, :, MLA_V:] = ones


def _mla_kv_proj(ckv_n, w_ukv_h, k_r, g0, g1, cos_t, sin_t, seq, tm=512):
    t, rank = ckv_n.shape
    per = seq // tm
    return pl.pallas_call(
        _mla_kv_kernel,
        grid=(t // tm,),
        in_specs=[pl.BlockSpec((tm, rank), lambda i: (i, 0)),
                  pl.BlockSpec((MLA_HEADS, rank, MLA_NOPE + MLA_V), lambda i: (0, 0, 0)),
                  pl.BlockSpec((tm, LANES), lambda i: (i, 0)),
                  pl.BlockSpec((1, LANES), lambda i: (0, 0)),
                  pl.BlockSpec((1, LANES), lambda i: (0, 0)),
                  pl.BlockSpec((tm, LANES), lambda i: (i % per, 0)),
                  pl.BlockSpec((tm, LANES), lambda i: (i % per, 0))],
        out_specs=[pl.BlockSpec((MLA_HEADS, tm, MLA_QK_PAD), lambda i: (0, i, 0)),
                   pl.BlockSpec((MLA_HEADS, tm, 2 * MLA_V), lambda i: (0, i, 0))],
        out_shape=[jax.ShapeDtypeStruct((MLA_HEADS, t, MLA_QK_PAD), BF16),
                   jax.ShapeDtypeStruct((MLA_HEADS, t, 2 * MLA_V), BF16)],
        compiler_params=_params("parallel"),
        name="mla_kv_proj",
    )(ckv_n, w_ukv_h, k_r, g0, g1, cos_t, sin_t)


MLA_KV_CHUNKS = 8


def _mla_attn_kernel(q_ref, k_ref, v_ref, o_ref):
    q = q_ref[...]
    tk = k_ref.shape[0] // MLA_KV_CHUNKS
    m = acc = None
    for c in range(MLA_KV_CHUNKS):
        s = _dot_nt(q, k_ref[c * tk:(c + 1) * tk, :])
        m_c = jnp.max(s, axis=-1, keepdims=True)
        if c == 0:
            m = m_c
            acc = _dot(jnp.exp(s - m).astype(BF16), v_ref[:tk, :])
        else:
            m_new = jnp.maximum(m, m_c)
            acc = jnp.exp(m - m_new) * acc + _dot(jnp.exp(s - m_new).astype(BF16),
                                                  v_ref[c * tk:(c + 1) * tk, :])
            m = m_new
    o_ref[...] = (acc[:, :MLA_V] / acc[:, MLA_V:]).astype(o_ref.dtype)


def _mla_attention(q, k, v, batch, seq, tq=512):
    t = q.shape[1]
    nq = seq // tq
    return pl.pallas_call(
        _mla_attn_kernel,
        grid=(batch, MLA_HEADS, nq),
        in_specs=[pl.BlockSpec((None, tq, MLA_QK_PAD), lambda b, h, i: (h, b * nq + i, 0)),
                  pl.BlockSpec((None, seq, MLA_QK_PAD), lambda b, h, i: (h, b, 0)),
                  pl.BlockSpec((None, seq, 2 * MLA_V), lambda b, h, i: (h, b, 0))],
        out_specs=pl.BlockSpec((tq, MLA_V), lambda b, h, i: (b * nq + i, h)),
        out_shape=jax.ShapeDtypeStruct((t, MLA_HEADS * MLA_V), BF16),
        compiler_params=_params("parallel", "parallel", "arbitrary"),
        name="mla_attention",
    )(q, k, v)


def _merge_kernel(ona_ref, odl_ref, omla_ref, wna_ref, wdl_ref, wmla_ref, g0_ref, g1_ref, g2_ref, o_ref):
    ona, odl, omla = ona_ref[...], odl_ref[...], omla_ref[...]
    for c in range(o_ref.shape[1] // GU_SPLIT):
        cols = slice(c * GU_SPLIT, (c + 1) * GU_SPLIT)
        acc = g0_ref[:, cols].astype(F32) * _dot(ona, wna_ref[:, cols])
        acc = acc + g1_ref[:, cols].astype(F32) * _dot(odl, wdl_ref[:, cols])
        acc = acc + g2_ref[:, cols].astype(F32) * _dot(omla, wmla_ref[:, cols])
        o_ref[:, cols] = acc.astype(o_ref.dtype)


def _merge(o_na, o_dl, o_mla, w_na, w_dl, w_mla, gates, tm=1024, tn=512):
    t = o_na.shape[0]
    d = w_na.shape[1]
    nj = d // tn

    def act(a):
        return pl.BlockSpec((tm, a.shape[1]), lambda i, j: (i, 0))

    def wgt(w):
        return pl.BlockSpec((w.shape[0], tn), lambda i, j: (0, j))

    def gate(idx):
        return pl.BlockSpec((tm, tn), lambda i, j: (i, idx * nj + j))

    return pl.pallas_call(
        _merge_kernel,
        grid=(t // tm, nj),
        in_specs=[act(o_na), act(o_dl), act(o_mla), wgt(w_na), wgt(w_dl), wgt(w_mla),
                  gate(0), gate(1), gate(2)],
        out_specs=pl.BlockSpec((tm, tn), lambda i, j: (i, j)),
        out_shape=jax.ShapeDtypeStruct((t, d), BF16),
        compiler_params=_params("parallel", "parallel"),
        name="branch_merge",
    )(o_na, o_dl, o_mla, w_na, w_dl, w_mla, gates, gates, gates)


GU_SPLIT = 256


def _swiglu_pieces(h, wg_ref, wu_ref, o_ref):
    for c in range(o_ref.shape[1] // GU_SPLIT):
        cols = slice(c * GU_SPLIT, (c + 1) * GU_SPLIT)
        g = _dot(h, wg_ref[:, cols].astype(BF16))
        u = _dot(h, wu_ref[:, cols].astype(BF16))
        o_ref[:, cols] = (g * jax.nn.sigmoid(g) * u).astype(o_ref.dtype)


def _gu_kernel(h_ref, wg_ref, wu_ref, o_ref):
    _swiglu_pieces(h_ref[...], wg_ref, wu_ref, o_ref)


def _swiglu_up(h, w_gu, tm=1024, tn=512):
    t, d = h.shape
    ff = w_gu.shape[1] // 2
    nj = ff // tn
    return pl.pallas_call(
        _gu_kernel,
        grid=(t // tm, nj),
        in_specs=[pl.BlockSpec((tm, d), lambda i, j: (i, 0)),
                  pl.BlockSpec((d, tn), lambda i, j: (0, j)),
                  pl.BlockSpec((d, tn), lambda i, j: (0, nj + j))],
        out_specs=pl.BlockSpec((tm, tn), lambda i, j: (i, j)),
        out_shape=jax.ShapeDtypeStruct((t, ff), BF16),
        compiler_params=_params("parallel", "parallel"),
        name="swiglu_up",
    )(h, w_gu, w_gu)


MOE_TM = 1024
TOP_K = 2


def _moe_routing(route, tokens):
    e_flat = jnp.concatenate([route[:, N_EXPERTS], route[:, N_EXPERTS + 1]]).astype(jnp.int32)
    onehot = (e_flat[:, None] == jnp.arange(N_EXPERTS, dtype=jnp.int32)[None, :]).astype(jnp.int32)
    csum = jnp.cumsum(onehot, axis=0)
    rank = jnp.sum((csum - onehot) * onehot, axis=1)
    padded = ((csum[-1] + MOE_TM - 1) // MOE_TM) * MOE_TM
    ends = jnp.cumsum(padded)
    slot = jnp.sum(onehot * (ends - padded)[None, :], axis=1) + rank
    n_tiles = (TOP_K * tokens) // MOE_TM + N_EXPERTS
    tile_start = jnp.arange(n_tiles, dtype=jnp.int32) * MOE_TM
    tile_expert = jnp.minimum(jnp.sum(tile_start[:, None] >= ends[None, :], axis=1), N_EXPERTS - 1)
    te = jnp.concatenate([tile_expert, ends[-1:] // MOE_TM]).astype(jnp.int32)
    return slot.astype(jnp.int32), te


def _dispatch_kernel(slot_ref, x_ref, g_ref, sc_ref, sh_ref, dst_in_ref, dst_ref, h_ref, sem, *, rows, tokens):
    del dst_in_ref
    base = pl.program_id(0) * rows
    h_ref[...] = _modulated_norm(x_ref[...], g_ref[...], sc_ref[...], sh_ref[...])

    def row_copy(r, choice):
        slot = slot_ref[choice * tokens + base + r]
        return pltpu.make_async_copy(h_ref.at[pl.ds(r, 1), :], dst_ref.at[pl.ds(slot, 1), :], sem)

    def start(r, carry):
        row_copy(r, 0).start()
        row_copy(r, 1).start()
        return carry

    def wait(r, carry):
        row_copy(r, 0).wait()
        row_copy(r, 1).wait()
        return carry

    lax.fori_loop(0, rows, start, 0, unroll=8)
    lax.fori_loop(0, rows, wait, 0, unroll=8)


def _moe_dispatch(x, g, sc, sh, slot, seq, n_rows, rows=256):
    t, d = x.shape
    per = seq // rows
    return pl.pallas_call(
        functools.partial(_dispatch_kernel, rows=rows, tokens=t),
        grid_spec=pltpu.PrefetchScalarGridSpec(
            num_scalar_prefetch=1,
            grid=(t // rows,),
            in_specs=[pl.BlockSpec((rows, d), lambda i, s: (i, 0)),
                      pl.BlockSpec((1, d), lambda i, s: (0, 0)),
                      pl.BlockSpec((None, 1, d), lambda i, s: (i // per, 0, 0)),
                      pl.BlockSpec((None, 1, d), lambda i, s: (i // per, 0, 0)),
                      pl.BlockSpec(memory_space=pl.ANY)],
            out_specs=pl.BlockSpec(memory_space=pl.ANY),
            scratch_shapes=[pltpu.VMEM((rows, d), F32), pltpu.SemaphoreType.DMA(())]),
        out_shape=jax.ShapeDtypeStruct((n_rows, d), F32),
        input_output_aliases={5: 0},
        compiler_params=_params("arbitrary"),
        name="moe_dispatch",
    )(slot, x, g.reshape(1, d), sc, sh, jnp.zeros((n_rows, d), F32))


def _moe_up_kernel(te_ref, xs_ref, wg_ref, wu_ref, o_ref, hb_ref, *, n_tiles):
    used = pl.program_id(0) < te_ref[n_tiles]

    @pl.when(used & (pl.program_id(1) == 0))
    def _():
        hb_ref[...] = xs_ref[...].astype(BF16)

    @pl.when(used)
    def _():
        _swiglu_pieces(hb_ref[...], wg_ref, wu_ref, o_ref)

    @pl.when(jnp.logical_not(used))
    def _():
        o_ref[...] = jnp.zeros(o_ref.shape, o_ref.dtype)


def _moe_up(xs, w_gu, te, tn=512):
    n_rows, d = xs.shape
    ff = w_gu.shape[2] // 2
    nj = ff // tn
    n_tiles = n_rows // MOE_TM

    def last_used(i, te_ref):
        return jnp.minimum(i, te_ref[n_tiles] - 1)

    return pl.pallas_call(
        functools.partial(_moe_up_kernel, n_tiles=n_tiles),
        grid_spec=pltpu.PrefetchScalarGridSpec(
            num_scalar_prefetch=1,
            grid=(n_tiles, nj),
            in_specs=[pl.BlockSpec((MOE_TM, d), lambda i, j, te_ref: (last_used(i, te_ref), 0)),
                      pl.BlockSpec((None, d, tn), lambda i, j, te_ref: (te_ref[i], 0, j)),
                      pl.BlockSpec((None, d, tn), lambda i, j, te_ref: (te_ref[i], 0, nj + j))],
            out_specs=pl.BlockSpec((MOE_TM, tn), lambda i, j, te_ref: (i, j)),
            scratch_shapes=[pltpu.VMEM((MOE_TM, d), BF16)]),
        out_shape=jax.ShapeDtypeStruct((n_rows, ff), BF16),
        compiler_params=_params("parallel", "arbitrary"),
        name="moe_up",
    )(te, xs, w_gu, w_gu)


def _moe_down_kernel(te_ref, a_ref, w_ref, o_ref, *, n_tiles):
    used = pl.program_id(0) < te_ref[n_tiles]

    @pl.when(used)
    def _():
        o_ref[...] = _dot(a_ref[...], w_ref[...])

    @pl.when(jnp.logical_not(used))
    def _():
        o_ref[...] = jnp.zeros(o_ref.shape, o_ref.dtype)


def _moe_down(act, w_down, te, tn=256):
    n_rows, ff = act.shape
    d = w_down.shape[2]
    n_tiles = n_rows // MOE_TM

    def last_used(i, te_ref):
        return jnp.minimum(i, te_ref[n_tiles] - 1)

    return pl.pallas_call(
        functools.partial(_moe_down_kernel, n_tiles=n_tiles),
        grid_spec=pltpu.PrefetchScalarGridSpec(
            num_scalar_prefetch=1,
            grid=(n_tiles, d // tn),
            in_specs=[pl.BlockSpec((MOE_TM, ff), lambda i, j, te_ref: (last_used(i, te_ref), 0)),
                      pl.BlockSpec((None, ff, tn), lambda i, j, te_ref: (te_ref[i], 0, j))],
            out_specs=pl.BlockSpec((MOE_TM, tn), lambda i, j, te_ref: (i, j))),
        out_shape=jax.ShapeDtypeStruct((n_rows, d), F32),
        compiler_params=_params("parallel", "arbitrary"),
        name="moe_down",
    )(te, act, w_down)


def _combine_kernel(slot_ref, x_ref, gate_ref, w_ref, ys_ref, o_ref, buf_ref, sem, *, rows, tokens):
    base = pl.program_id(0) * rows

    def row_copy(r, choice):
        slot = slot_ref[choice * tokens + base + r]
        return pltpu.make_async_copy(ys_ref.at[pl.ds(slot, 1), :], buf_ref.at[choice, pl.ds(r, 1), :], sem)

    def start(r, carry):
        row_copy(r, 0).start()
        row_copy(r, 1).start()
        return carry

    def wait(r, carry):
        row_copy(r, 0).wait()
        row_copy(r, 1).wait()
        return carry

    lax.fori_loop(0, rows, start, 0, unroll=8)
    lax.fori_loop(0, rows, wait, 0, unroll=8)
    w = w_ref[...]
    y = w[:, 0:1] * buf_ref[0] + w[:, 1:2] * buf_ref[1]
    o_ref[...] = x_ref[...] + gate_ref[...] * y


def _moe_combine(x, gate, w12, ys, slot, seq, rows=256):
    t, d = x.shape
    per = seq // rows
    return pl.pallas_call(
        functools.partial(_combine_kernel, rows=rows, tokens=t),
        grid_spec=pltpu.PrefetchScalarGridSpec(
            num_scalar_prefetch=1,
            grid=(t // rows,),
            in_specs=[pl.BlockSpec((rows, d), lambda i, s: (i, 0)),
                      pl.BlockSpec((None, 1, d), lambda i, s: (i // per, 0, 0)),
                      pl.BlockSpec((rows, TOP_K), lambda i, s: (i, 0)),
                      pl.BlockSpec(memory_space=pl.ANY)],
            out_specs=pl.BlockSpec((rows, d), lambda i, s: (i, 0)),
            scratch_shapes=[pltpu.VMEM((TOP_K, rows, d), F32), pltpu.SemaphoreType.DMA(())]),
        out_shape=jax.ShapeDtypeStruct((t, d), F32),
        compiler_params=_params("arbitrary"),
        name="moe_combine",
    )(slot, x, gate, w12, ys)


def _router_kernel(x_ref, g_ref, sc_ref, sh_ref, w_ref, o_ref):
    h = _modulated_norm(x_ref[...], g_ref[...], sc_ref[...], sh_ref[...])
    logits = jnp.dot(h, w_ref[...], preferred_element_type=F32, precision=lax.Precision.HIGHEST)
    lane = lax.broadcasted_iota(jnp.int32, logits.shape, 1).astype(F32)
    lg = jnp.where(lane < N_EXPERTS, logits, NEG_INF)
    m1 = jnp.max(lg, axis=-1, keepdims=True)
    i1 = jnp.min(jnp.where(lg == m1, lane, float(LANES)), axis=-1, keepdims=True)
    lg2 = jnp.where(lane == i1, NEG_INF, lg)
    m2 = jnp.max(lg2, axis=-1, keepdims=True)
    i2 = jnp.min(jnp.where(lg2 == m2, lane, float(LANES)), axis=-1, keepdims=True)
    e2 = jnp.exp(m2 - m1)
    z = 1.0 + e2
    out = jnp.where(lane == N_EXPERTS, i1, 0.0) + jnp.where(lane == N_EXPERTS + 1, i2, 0.0)
    out = out + jnp.where(lane == N_EXPERTS + 2, 1.0 / z, 0.0) + jnp.where(lane == N_EXPERTS + 3, e2 / z, 0.0)
    o_ref[...] = out


def _router(x, g, sc, sh, w_router_pad, seq, tm=512):
    t, d = x.shape
    per = seq // tm
    return pl.pallas_call(
        _router_kernel,
        grid=(t // tm,),
        in_specs=[pl.BlockSpec((tm, d), lambda i: (i, 0)),
                  pl.BlockSpec((1, d), lambda i: (0, 0)),
                  pl.BlockSpec((None, 1, d), lambda i: (i // per, 0, 0)),
                  pl.BlockSpec((None, 1, d), lambda i: (i // per, 0, 0)),
                  pl.BlockSpec((d, LANES), lambda i: (0, 0))],
        out_specs=pl.BlockSpec((tm, LANES), lambda i: (i, 0)),
        out_shape=jax.ShapeDtypeStruct((t, LANES), F32),
        compiler_params=_params("parallel"),
        name="router",
    )(x, g.reshape(1, d), sc, sh, w_router_pad)


MM_TM = 1024
MM_TN = 512
MM_SPLIT = 256


def _residual_matmul(a, w, x, gate, seq, name, tm=MM_TM, tn=MM_TN):
    per = seq // tm
    return _matmul(a, w, tm=tm, tn=tn, out_dtype=F32, epilogue=_ep_residual, split=MM_SPLIT,
                   extras=[(x, (tm, tn), lambda i, j: (i, j)),
                           (gate, (None, 1, tn), lambda i, j: (i // per, 0, j))],
                   name=name)


def _token_mixer(h, x, gate, lw, batch, seq, tables):
    d = h.shape[1]
    w_in = lw["w_in"]
    tm, tn = MM_TM, MM_TN
    per = seq // tm
    cos_d, sin_d, cos_m, sin_m = tables

    def cols(lo, hi):
        return w_in[:, lo:hi].astype(BF16)

    def tile_gain(gq, gk, heads):
        return jnp.concatenate([jnp.tile(gq, heads), jnp.tile(gk, heads)]).reshape(1, -1).astype(F32)

    o = 0
    na_qk = _matmul(h, cols(o, o + 2 * NA_W), tm=tm, tn=tn, out_dtype=BF16, epilogue=_ep_headnorm,
                    split=MM_SPLIT,
                    extras=[(tile_gain(lw["na_q_norm"], lw["na_k_norm"], NA_HEADS), (1, tn),
                             lambda i, j: (0, j))], name="na_qk_proj")
    o += 2 * NA_W
    na_v = _matmul(h, cols(o, o + NA_W), tm=tm, tn=tn, out_dtype=BF16, split=MM_SPLIT, name="na_v_proj")
    o += NA_W
    dl_qk = _matmul(h, cols(o, o + 2 * DIL_W), tm=tm, tn=tn, out_dtype=F32, epilogue=_ep_headnorm_rope,
                    split=MM_SPLIT,
                    extras=[(tile_gain(lw["dil_q_norm"], lw["dil_k_norm"], DIL_HEADS), (1, tn),
                             lambda i, j: (0, j)),
                            (cos_d, (tm, LANES), lambda i, j: (i % per, 0)),
                            (sin_d, (tm, LANES), lambda i, j: (i % per, 0)),
                            (_pair_swap_matrix(MM_SPLIT, ROT_DIM // 2), (MM_SPLIT, MM_SPLIT),
                             lambda i, j: (0, 0))], name="dil_qk_proj")
    o += 2 * DIL_W
    dl_v = _matmul(h, cols(o, o + DIL_W), tm=tm, tn=tn, out_dtype=F32, split=MM_SPLIT, name="dil_v_proj")
    o += DIL_W
    cq_n = _matmul(h, cols(o, o + MLA_Q_RANK), tm=tm, tn=MLA_Q_RANK, out_dtype=BF16, epilogue=_ep_rownorm,
                   extras=[(lw["mla_q_a_norm"].reshape(1, -1), (1, MLA_Q_RANK), lambda i, j: (0, 0))],
                   name="mla_cq_proj")
    o += MLA_Q_RANK
    ckv_n = _matmul(h, cols(o, o + MLA_KV_RANK), tm=tm, tn=MLA_KV_RANK, out_dtype=BF16, epilogue=_ep_rownorm,
                    extras=[(lw["mla_kv_a_norm"].reshape(1, -1), (1, MLA_KV_RANK), lambda i, j: (0, 0))],
                    name="mla_ckv_proj")
    o += MLA_KV_RANK
    w_kr = jnp.pad(w_in[:, o:o + MLA_ROPE], ((0, 0), (0, LANES - MLA_ROPE))).astype(BF16)
    k_r = _matmul(h, w_kr, tm=tm, tn=LANES, out_dtype=F32, name="mla_kr_proj")
    o += MLA_ROPE
    gates = _matmul(h, cols(o, o + 3 * d), tm=tm, tn=tn, out_dtype=BF16, epilogue=_ep_sigmoid,
                    split=MM_SPLIT, name="gate_proj")

    o_na = _na_attention(na_qk, na_v, lw["na_rpb"], batch, seq)
    o_dl = _dilated_attention(dl_qk, dl_v, batch, seq)
    w_uq = lw["mla_w_uq"].reshape(MLA_Q_RANK, MLA_HEADS, MLA_QK).transpose(1, 0, 2)
    w_uq = jnp.pad(w_uq, ((0, 0), (0, 0), (0, MLA_QK_PAD - MLA_QK))).astype(BF16)
    w_ukv = lw["mla_w_ukv"].reshape(MLA_KV_RANK, MLA_HEADS, MLA_NOPE + MLA_V).transpose(1, 0, 2).astype(BF16)
    gq = jnp.pad(lw["mla_q_norm"] * (1.0 / math.sqrt(MLA_QK)), (0, MLA_QK_PAD - MLA_QK)).reshape(1, MLA_QK_PAD)
    gk = lw["mla_k_norm"]
    gk0 = gk[:MLA_NOPE].reshape(1, LANES)
    gk1 = jnp.pad(gk[MLA_NOPE:], (0, LANES - MLA_ROPE)).reshape(1, LANES)
    q_m = _mla_q_proj(cq_n, w_uq, gq, cos_m, sin_m, seq)
    k_m, v_m = _mla_kv_proj(ckv_n, w_ukv, k_r, gk0, gk1, cos_m, sin_m, seq)
    o_mla = _mla_attention(q_m, k_m, v_m, batch, seq)

    merged = _merge(o_na, o_dl, o_mla, lw["w_branch_na"].astype(BF16), lw["w_branch_dil"].astype(BF16),
                    lw["w_branch_mla"].astype(BF16), gates)
    return _residual_matmul(merged, lw["w_out"].astype(BF16), x, gate, seq, "out_proj")


def kernel(x, c, w_ada, b_ada, norm_mix, norm_ffn, w_in, na_q_norm, na_k_norm, na_rpb, dil_q_norm, dil_k_norm, mla_q_a_norm, mla_w_uq, mla_kv_a_norm, mla_w_ukv, mla_q_norm, mla_k_norm, w_branch_na, w_branch_dil, w_branch_mla, w_out, ffn_w_gu, ffn_w_down, moe_router, moe_w_gu, moe_w_down):
    batch, seq, d = x.shape
    depth = w_ada.shape[0]
    assert seq % MM_TM == 0 and seq % GRID_W == 0
    t = batch * seq
    xf = x.reshape(t, d)

    c_pad = jnp.pad(c, ((0, (-batch) % 8), (0, 0)))
    mod = _adaln(c_pad, w_ada, b_ada)[:, :batch].reshape(depth, batch, 6, 1, d)
    tables = _rope_tables(seq, ROT_DIM, 1.0) + _rope_tables(seq, MLA_ROPE, 1.0)

    for l in range(depth):
        sh1, sc1, g1, sh2, sc2, g2 = [mod[l, :, i] for i in range(6)]
        lw = dict(w_in=w_in[l], na_q_norm=na_q_norm[l], na_k_norm=na_k_norm[l], na_rpb=na_rpb[l],
                  dil_q_norm=dil_q_norm[l], dil_k_norm=dil_k_norm[l], mla_q_a_norm=mla_q_a_norm[l],
                  mla_w_uq=mla_w_uq[l], mla_kv_a_norm=mla_kv_a_norm[l], mla_w_ukv=mla_w_ukv[l],
                  mla_q_norm=mla_q_norm[l], mla_k_norm=mla_k_norm[l], w_branch_na=w_branch_na[l],
                  w_branch_dil=w_branch_dil[l], w_branch_mla=w_branch_mla[l], w_out=w_out[l])
        h = _normmod(xf, norm_mix[l], sc1, sh1, seq)
        xf = _token_mixer(h, xf, g1, lw, batch, seq, tables)

        if l % 2 == 0:
            h = _normmod(xf, norm_ffn[l], sc2, sh2, seq)
            act = _swiglu_up(h, ffn_w_gu[l // 2].astype(BF16))
            xf = _residual_matmul(act, ffn_w_down[l // 2].astype(BF16), xf, g2, seq, "down_proj", tm=512)
        else:
            w_r = jnp.pad(moe_router[l // 2], ((0, 0), (0, LANES - N_EXPERTS)))
            route = _router(xf, norm_ffn[l], sc2, sh2, w_r, seq)
            slot, te = _moe_routing(route, t)
            n_rows = TOP_K * t + N_EXPERTS * MOE_TM
            xs = _moe_dispatch(xf, norm_ffn[l], sc2, sh2, slot, seq, n_rows)
            act = _moe_up(xs, moe_w_gu[l // 2], te)
            ys = _moe_down(act, moe_w_down[l // 2].astype(BF16), te)
            xf = _moe_combine(xf, g2, route[:, N_EXPERTS + 2:N_EXPERTS + 4], ys, slot, seq)
    return xf.reshape(batch, seq, d)
```

```python
import functools
import math

import numpy as np
import jax
import jax.numpy as jnp
from jax import lax
from jax.experimental import pallas as pl
from jax.experimental.pallas import tpu as pltpu

GRID_W = 64
HEAD_DIM = 128
ROPE_THETA = 500000.0
ROT_DIM = HEAD_DIM // 4
RMS_EPS = 1e-6
NEG_INF = -1e30
NA_HEADS = 8
NA_KH = 8
NA_KW = 16
DIL_GROUPS = ((128, 1), (512, 4), (2048, 16))
DIL_HEADS_PER_GROUP = 4
DIL_HEADS = DIL_HEADS_PER_GROUP * len(DIL_GROUPS)
DIL_RADIUS = 64
MLA_HEADS = 8
MLA_NOPE = 128
MLA_ROPE = 64
MLA_V = 128
MLA_Q_RANK = 768
MLA_KV_RANK = 512
MLA_QK = MLA_NOPE + MLA_ROPE
MLA_QK_PAD = 256
N_EXPERTS = 8
LANES = 128

NA_W = NA_HEADS * HEAD_DIM
DIL_W = DIL_HEADS * HEAD_DIM
DIL_OUT = DIL_HEADS_PER_GROUP * HEAD_DIM

VMEM_LIMIT_BYTES = 48 * 1024 * 1024

F32 = jnp.float32
BF16 = jnp.bfloat16
LOG2E = math.log2(math.e)
LN2 = math.log(2.0)


def _params(*sem):
    return pltpu.CompilerParams(dimension_semantics=sem, vmem_limit_bytes=VMEM_LIMIT_BYTES)


def _dot(a, b):
    return jnp.dot(a, b, preferred_element_type=F32)


def _dot_nt(a, b):
    return lax.dot_general(a, b, (((1,), (1,)), ((), ())), preferred_element_type=F32)


def _adaln_kernel(c_ref, w_ref, b_ref, o_ref):
    o_ref[...] = _dot(c_ref[...], w_ref[...]) + b_ref[...]


def _adaln(c_pad, w_ada, b_ada):
    depth, d, n = w_ada.shape
    tn = 1024
    return pl.pallas_call(
        _adaln_kernel,
        grid=(depth, n // tn),
        in_specs=[pl.BlockSpec(c_pad.shape, lambda l, j: (0, 0)),
                  pl.BlockSpec((None, d, tn), lambda l, j: (l, 0, j)),
                  pl.BlockSpec((None, 1, tn), lambda l, j: (l, 0, j))],
        out_specs=pl.BlockSpec((None, c_pad.shape[0], tn), lambda l, j: (l, 0, j)),
        out_shape=jax.ShapeDtypeStruct((depth, c_pad.shape[0], n), F32),
        compiler_params=_params("parallel", "parallel"),
        name="adaln",
    )(c_pad, w_ada, b_ada.reshape(depth, 1, n))


def _modulated_norm(x, g, sc, sh):
    ms = jnp.mean(x * x, axis=-1, keepdims=True)
    return (x * lax.rsqrt(ms + RMS_EPS) * g) * (1.0 + sc) + sh


def _normmod_kernel(x_ref, g_ref, sc_ref, sh_ref, o_ref):
    o_ref[...] = _modulated_norm(x_ref[...], g_ref[...], sc_ref[...], sh_ref[...]).astype(o_ref.dtype)


def _normmod(x, g, sc, sh, seq, tm=512):
    t, d = x.shape
    per = seq // tm
    return pl.pallas_call(
        _normmod_kernel,
        grid=(t // tm,),
        in_specs=[pl.BlockSpec((tm, d), lambda i: (i, 0)),
                  pl.BlockSpec((1, d), lambda i: (0, 0)),
                  pl.BlockSpec((None, 1, d), lambda i: (i // per, 0, 0)),
                  pl.BlockSpec((None, 1, d), lambda i: (i // per, 0, 0))],
        out_specs=pl.BlockSpec((tm, d), lambda i: (i, 0)),
        out_shape=jax.ShapeDtypeStruct((t, d), BF16),
        compiler_params=_params("parallel"),
        name="normmod",
    )(x, g.reshape(1, d), sc, sh)


def _mm_kernel(*refs, n_extra, epilogue, split):
    a_ref, b_ref = refs[0], refs[1]
    extras = refs[2:2 + n_extra]
    o_ref = refs[2 + n_extra]
    a = a_ref[...]
    for c in range(b_ref.shape[1] // split):
        cols = slice(c * split, (c + 1) * split)
        epilogue(_dot(a, b_ref[:, cols]), extras, o_ref, cols)


def _ep_store(acc, extras, o_ref, cols):
    o_ref[:, cols] = acc.astype(o_ref.dtype)


def _matmul(a, b, *, tm, tn, out_dtype, epilogue=_ep_store, extras=(), split=None, name):
    m, kdim = a.shape
    n = b.shape[1]
    split = tn if split is None else split
    assert m % tm == 0 and n % tn == 0 and tn % split == 0
    in_specs = [pl.BlockSpec((tm, kdim), lambda i, j: (i, 0)),
                pl.BlockSpec((kdim, tn), lambda i, j: (0, j))]
    in_specs += [pl.BlockSpec(bs, im) for (_, bs, im) in extras]
    return pl.pallas_call(
        functools.partial(_mm_kernel, n_extra=len(extras), epilogue=epilogue, split=split),
        grid=(m // tm, n // tn),
        in_specs=in_specs,
        out_specs=pl.BlockSpec((tm, tn), lambda i, j: (i, j)),
        out_shape=jax.ShapeDtypeStruct((m, n), out_dtype),
        compiler_params=_params("parallel", "parallel"),
        name=name,
    )(a, b, *[e[0] for e in extras])


def _head_rms(blk, gain):
    ms = jnp.mean(blk * blk, axis=-1, keepdims=True)
    return blk * lax.rsqrt(ms + RMS_EPS) * gain


def _head_slices(cols):
    return [slice(c, c + HEAD_DIM) for c in range(cols.start, cols.stop, HEAD_DIM)]


def _ep_headnorm(acc, extras, o_ref, cols):
    for i, sl in enumerate(_head_slices(cols)):
        blk = acc[:, i * HEAD_DIM:(i + 1) * HEAD_DIM]
        o_ref[:, sl] = _head_rms(blk, extras[0][:, sl]).astype(o_ref.dtype)


def _rotate_pairs(y, cos_t, sin_t, half):
    lane = lax.broadcasted_iota(jnp.int32, y.shape, 1)
    swapped = jnp.where(lane < half, pltpu.roll(y, LANES - half, 1), pltpu.roll(y, half, 1))
    return y * cos_t + swapped * sin_t


def _pair_swap_matrix(width, half):
    p = np.zeros((width, width), np.float32)
    for base in range(0, width, HEAD_DIM):
        for i in range(half):
            p[base + i + half, base + i] = 1.0
            p[base + i, base + i + half] = 1.0
    return jnp.asarray(p, BF16)


def _ep_headnorm_rope(acc, extras, o_ref, cols):
    cos_t = extras[1][...]
    sin_t = extras[2][...]
    heads = _head_slices(cols)
    y = jnp.concatenate([_head_rms(acc[:, i * HEAD_DIM:(i + 1) * HEAD_DIM], extras[0][:, sl])
                         for i, sl in enumerate(heads)], axis=1)
    swapped = _dot(y.astype(BF16), extras[3][...])
    for i, sl in enumerate(heads):
        loc = slice(i * HEAD_DIM, (i + 1) * HEAD_DIM)
        o_ref[:, sl] = (y[:, loc] * cos_t + swapped[:, loc] * sin_t).astype(o_ref.dtype)


def _ep_rownorm(acc, extras, o_ref, cols):
    o_ref[:, cols] = _head_rms(acc, extras[0][:, cols]).astype(o_ref.dtype)


def _ep_sigmoid(acc, extras, o_ref, cols):
    o_ref[:, cols] = jax.nn.sigmoid(acc).astype(o_ref.dtype)


def _ep_residual(acc, extras, o_ref, cols):
    o_ref[:, cols] = extras[0][:, cols] + extras[1][:, cols] * acc


def _rope_tables(seq, rot, fill):
    half = rot // 2
    inv = ROPE_THETA ** (-jnp.arange(half, dtype=F32) * (2.0 / rot))
    ang = jnp.arange(seq, dtype=jnp.int32).astype(F32)[:, None] * inv[None, :]
    cos, sin = jnp.cos(ang), jnp.sin(ang)
    pad = LANES - rot
    cos_t = jnp.concatenate([cos, cos, jnp.full((seq, pad), fill, F32)], axis=1)
    sin_t = jnp.concatenate([-sin, sin, jnp.zeros((seq, pad), F32)], axis=1)
    return cos_t, sin_t


NA_RB = 4
NA_KROWS = NA_KH + NA_RB


def _na_geometry(rows):
    kh = min(NA_KH, rows)
    assert rows % NA_RB == 0 and rows >= NA_KROWS
    patterns, ids, kstarts = [], [], []
    for r in range(0, rows, NA_RB):
        ks = int(np.clip(r - NA_KH // 2, 0, rows - NA_KROWS))
        pat = tuple((int(np.clip(r + a - NA_KH // 2, 0, rows - kh)) - ks, ks - (r + a)) for a in range(NA_RB))
        if pat not in patterns:
            patterns.append(pat)
        ids.append(patterns.index(pat))
        kstarts.append(ks)
    return patterns, np.array(ids + kstarts, np.int32)


def _na_bias_table(rpb, rows, patterns):
    h = rpb.shape[0]
    kh = min(NA_KH, rows)
    qc = np.arange(GRID_W)
    kc = np.arange(GRID_W)
    cstart = np.clip(qc - NA_KW // 2, 0, GRID_W - NA_KW)
    ok = (kc[None, :] >= cstart[:, None]) & (kc[None, :] < cstart[:, None] + NA_KW)
    dc = np.clip(kc[None, :] - qc[:, None] + NA_KW - 1, 0, 2 * NA_KW - 2)
    by_col = rpb.astype(F32)[:, :, dc] * LOG2E + jnp.where(ok, 0.0, NEG_INF).astype(F32)
    kr = np.arange(NA_KROWS)
    dr = np.zeros((len(patterns), NA_RB, NA_KROWS), np.int32)
    row_ok = np.zeros((len(patterns), NA_RB, NA_KROWS), bool)
    for p, pat in enumerate(patterns):
        for a, (start, offset) in enumerate(pat):
            row_ok[p, a] = (kr >= start) & (kr < start + kh)
            dr[p, a] = np.clip(kr + offset + NA_KH - 1, 0, 2 * NA_KH - 2)
    tab = jnp.take(by_col, jnp.asarray(dr.reshape(-1)), axis=1)
    tab = tab.reshape(h, len(patterns), NA_RB, NA_KROWS, GRID_W, GRID_W)
    tab = jnp.where(row_ok[None, :, :, :, None, None], tab, NEG_INF)
    tab = tab.transpose(0, 1, 2, 4, 3, 5)
    return tab.reshape(h, len(patterns), NA_RB * GRID_W, NA_KROWS * GRID_W)


def _na_kernel(geo_ref, q_ref, k_ref, v_ref, b_ref, o_ref, *, n_blocks):
    qn = NA_RB * GRID_W
    kn = NA_KROWS * GRID_W
    ones = jnp.ones((kn, HEAD_DIM), BF16)

    def body(blk, carry):
        q0 = pl.multiple_of(blk * qn, qn)
        k0 = pl.multiple_of(geo_ref[n_blocks + blk] * GRID_W, GRID_W)
        q = q_ref[pl.ds(q0, qn), :]
        k = k_ref[pl.ds(k0, kn), :]
        v1 = jnp.concatenate([v_ref[pl.ds(k0, kn), :], ones], axis=1)
        s = _dot_nt(q, k) + b_ref[geo_ref[blk]]
        m = jnp.max(s, axis=-1, keepdims=True)
        acc = _dot(jnp.exp2(s - m).astype(BF16), v1)
        o_ref[pl.ds(q0, qn), :] = (acc[:, :HEAD_DIM] / acc[:, HEAD_DIM:]).astype(o_ref.dtype)
        return carry

    lax.fori_loop(0, n_blocks, body, 0, unroll=2)


def _na_attention(qk, v, rpb, batch, seq):
    rows = seq // GRID_W
    patterns, geo = _na_geometry(rows)
    bias = _na_bias_table(rpb, rows, patterns)
    t = qk.shape[0]
    blk = (seq, HEAD_DIM)
    return pl.pallas_call(
        functools.partial(_na_kernel, n_blocks=rows // NA_RB),
        grid_spec=pltpu.PrefetchScalarGridSpec(
            num_scalar_prefetch=1,
            grid=(batch, NA_HEADS),
            in_specs=[pl.BlockSpec(blk, lambda b, h, geo_ref: (b, h)),
                      pl.BlockSpec(blk, lambda b, h, geo_ref: (b, NA_HEADS + h)),
                      pl.BlockSpec(blk, lambda b, h, geo_ref: (b, h)),
                      pl.BlockSpec((None,) + bias.shape[1:], lambda b, h, geo_ref: (h, 0, 0, 0))],
            out_specs=pl.BlockSpec(blk, lambda b, h, geo_ref: (b, h))),
        out_shape=jax.ShapeDtypeStruct((t, NA_W), BF16),
        compiler_params=_params("parallel", "parallel"),
        name="na_attention",
    )(jnp.asarray(geo), qk, qk, v, bias)


DIL_QCHUNK = 2 * DIL_RADIUS
DIL_KWIN = 4 * DIL_RADIUS


def _dil_window_mask():
    r = np.arange(DIL_QCHUNK)[None, :, None]
    c = np.arange(DIL_KWIN)[None, None, :]
    off = (np.arange(3) * DIL_RADIUS)[:, None, None]
    return jnp.asarray(np.where(np.abs(c - r - off) <= DIL_RADIUS, 0.0, NEG_INF), F32)


def _dil_kernel(q_ref, k_ref, v_ref, mask_ref, o_ref, lse_ref, *, seq, dil):
    length = seq // dil
    n_chunk = length // DIL_QCHUNK
    ones = jnp.ones((DIL_KWIN, HEAD_DIM), BF16)

    def rows(residue, first, count):
        return pl.ds(residue + first * dil, count, stride=dil)

    def body(it, carry):
        residue = it // n_chunk
        q0 = (it % n_chunk) * DIL_QCHUNK
        k0 = jnp.clip(q0 - DIL_RADIUS, 0, length - DIL_KWIN)
        q = q_ref[rows(residue, q0, DIL_QCHUNK), :].astype(BF16)
        k = k_ref[rows(residue, k0, DIL_KWIN), :].astype(BF16)
        v1 = jnp.concatenate([v_ref[rows(residue, k0, DIL_KWIN), :].astype(BF16), ones], axis=1)
        s = _dot_nt(q, k) + mask_ref[(q0 - k0) // DIL_RADIUS]
        m = jnp.max(s, axis=-1, keepdims=True)
        acc = _dot(jnp.exp2(s - m).astype(BF16), v1)
        den = acc[:, HEAD_DIM:]
        o_ref[rows(residue, q0, DIL_QCHUNK), :] = acc[:, :HEAD_DIM] / den
        lse_ref[rows(residue, q0, DIL_QCHUNK), :] = m * LN2 + jnp.log(den[:, :1])
        return carry

    lax.fori_loop(0, dil * n_chunk, body, 0, unroll=4)


def _dil_group(qk, v, group, dil, batch, seq):
    length = seq // dil
    assert length % DIL_QCHUNK == 0 and length >= DIL_KWIN
    hg = DIL_HEADS_PER_GROUP
    blk = (seq, HEAD_DIM)
    mask = _dil_window_mask()
    return pl.pallas_call(
        functools.partial(_dil_kernel, seq=seq, dil=dil),
        grid=(batch, hg),
        in_specs=[pl.BlockSpec(blk, lambda b, h: (b, group * hg + h)),
                  pl.BlockSpec(blk, lambda b, h: (b, DIL_HEADS + group * hg + h)),
                  pl.BlockSpec(blk, lambda b, h: (b, group * hg + h)),
                  pl.BlockSpec(mask.shape, lambda b, h: (0, 0, 0))],
        out_specs=[pl.BlockSpec(blk, lambda b, h: (b, h)),
                   pl.BlockSpec((None, seq, 1), lambda b, h: (b * hg + h, 0, 0))],
        out_shape=[jax.ShapeDtypeStruct((batch * seq, DIL_OUT), F32),
                   jax.ShapeDtypeStruct((batch * hg, seq, 1), F32)],
        compiler_params=_params("parallel", "parallel"),
        name="dilated_attention",
    )(qk, qk, v, mask)


def _dil_mix_kernel(o0_ref, o1_ref, o2_ref, l0_ref, l1_ref, l2_ref, o_ref):
    l0, l1, l2 = l0_ref[...], l1_ref[...], l2_ref[...]
    mx = jnp.maximum(jnp.maximum(l0, l1), l2)
    e0, e1, e2 = jnp.exp(l0 - mx), jnp.exp(l1 - mx), jnp.exp(l2 - mx)
    z = e0 + e1 + e2
    w0, w1, w2 = e0 / z, e1 / z, e2 / z
    for h in range(DIL_HEADS_PER_GROUP):
        sl = slice(h * HEAD_DIM, (h + 1) * HEAD_DIM)
        hs = slice(h, h + 1)
        o_ref[:, sl] = (w0[:, hs] * o0_ref[:, sl] + w1[:, hs] * o1_ref[:, sl]
                        + w2[:, hs] * o2_ref[:, sl]).astype(o_ref.dtype)


def _dil_mix(outs, lses, tm=512):
    t = outs[0].shape[0]
    ospec = pl.BlockSpec((tm, DIL_OUT), lambda i: (i, 0))
    lspec = pl.BlockSpec((tm, DIL_HEADS_PER_GROUP), lambda i: (i, 0))
    return pl.pallas_call(
        _dil_mix_kernel,
        grid=(t // tm,),
        in_specs=[ospec] * 3 + [lspec] * 3,
        out_specs=ospec,
        out_shape=jax.ShapeDtypeStruct((t, DIL_OUT), BF16),
        compiler_params=_params("parallel"),
        name="dilated_mix",
    )(*outs, *lses)


def _dilated_attention(qk, v, batch, seq):
    hg = DIL_HEADS_PER_GROUP
    outs, lses = [], []
    for g, (window, dil) in enumerate(DIL_GROUPS):
        assert window // (2 * dil) == DIL_RADIUS and seq % dil == 0
        o, lse = _dil_group(qk, v, g, dil, batch, seq)
        outs.append(o)
        lses.append(lse.reshape(batch, hg, seq).transpose(0, 2, 1).reshape(batch * seq, hg))
    return _dil_mix(outs, lses)


def _mla_q_kernel(a_ref, w_ref, g_ref, cos_ref, sin_ref, o_ref):
    a = a_ref[...]
    gain = g_ref[...]
    cos_t, sin_t = cos_ref[...], sin_ref[...]
    for h in range(MLA_HEADS):
        acc = _dot(a, w_ref[h])
        ms = jnp.sum(acc * acc, axis=-1, keepdims=True) * (1.0 / MLA_QK)
        y = acc * lax.rsqrt(ms + RMS_EPS) * gain
        o_ref[h, :, :MLA_NOPE] = y[:, :MLA_NOPE].astype(o_ref.dtype)
        o_ref[h, :, MLA_NOPE:] = _rotate_pairs(y[:, MLA_NOPE:], cos_t, sin_t,
                                               MLA_ROPE // 2).astype(o_ref.dtype)


def _mla_q_proj(cq_n, w_uq_h, gain, cos_t, sin_t, seq, tm=512):
    t, rank = cq_n.shape
    per = seq // tm
    return pl.pallas_call(
        _mla_q_kernel,
        grid=(t // tm,),
        in_specs=[pl.BlockSpec((tm, rank), lambda i: (i, 0)),
                  pl.BlockSpec((MLA_HEADS, rank, MLA_QK_PAD), lambda i: (0, 0, 0)),
                  pl.BlockSpec((1, MLA_QK_PAD), lambda i: (0, 0)),
                  pl.BlockSpec((tm, LANES), lambda i: (i % per, 0)),
                  pl.BlockSpec((tm, LANES), lambda i: (i % per, 0))],
        out_specs=pl.BlockSpec((MLA_HEADS, tm, MLA_QK_PAD), lambda i: (0, i, 0)),
        out_shape=jax.ShapeDtypeStruct((MLA_HEADS, t, MLA_QK_PAD), BF16),
        compiler_params=_params("parallel"),
        name="mla_q_proj",
    )(cq_n, w_uq_h, gain, cos_t, sin_t)


def _mla_kv_kernel(a_ref, w_ref, kr_ref, g0_ref, g1_ref, cos_ref, sin_ref, k_ref, v_ref):
    a = a_ref[...]
    kr = kr_ref[...]
    kr_ss = jnp.sum(kr * kr, axis=-1, keepdims=True)
    g0, g1 = g0_ref[...], g1_ref[...]
    cos_t, sin_t = cos_ref[...], sin_ref[...]
    ones = jnp.ones((a.shape[0], MLA_V), v_ref.dtype)
    for h in range(MLA_HEADS):
        acc = _dot(a, w_ref[h])
        kn = acc[:, :MLA_NOPE]
        ms = (jnp.sum(kn * kn, axis=-1, keepdims=True) + kr_ss) * (1.0 / MLA_QK)
        inv = lax.rsqrt(ms + RMS_EPS)
        k_ref[h, :, :MLA_NOPE] = (kn * inv * g0).astype(k_ref.dtype)
        k_ref[h, :, MLA_NOPE:] = _rotate_pairs(kr * inv * g1, cos_t, sin_t,
                                               MLA_ROPE // 2).astype(k_ref.dtype)
        v_ref[h, :, :MLA_V] = acc[:, MLA_NOPE:].astype(v_ref.dtype)
        v_ref[h, :, MLA_V:] = ones


def _mla_kv_proj(ckv_n, w_ukv_h, k_r, g0, g1, cos_t, sin_t, seq, tm=512):
    t, rank = ckv_n.shape
    per = seq // tm
    return pl.pallas_call(
        _mla_kv_kernel,
        grid=(t // tm,),
        in_specs=[pl.BlockSpec((tm, rank), lambda i: (i, 0)),
                  pl.BlockSpec((MLA_HEADS, rank, MLA_NOPE + MLA_V), lambda i: (0, 0, 0)),
                  pl.BlockSpec((tm, LANES), lambda i: (i, 0)),
                  pl.BlockSpec((1, LANES), lambda i: (0, 0)),
                  pl.BlockSpec((1, LANES), lambda i: (0, 0)),
                  pl.BlockSpec((tm, LANES), lambda i: (i % per, 0)),
                  pl.BlockSpec((tm, LANES), lambda i: (i % per, 0))],
        out_specs=[pl.BlockSpec((MLA_HEADS, tm, MLA_QK_PAD), lambda i: (0, i, 0)),
                   pl.BlockSpec((MLA_HEADS, tm, 2 * MLA_V), lambda i: (0, i, 0))],
        out_shape=[jax.ShapeDtypeStruct((MLA_HEADS, t, MLA_QK_PAD), BF16),
                   jax.ShapeDtypeStruct((MLA_HEADS, t, 2 * MLA_V), BF16)],
        compiler_params=_params("parallel"),
        name="mla_kv_proj",
    )(ckv_n, w_ukv_h, k_r, g0, g1, cos_t, sin_t)


MLA_KV_CHUNKS = 8


def _mla_attn_kernel(q_ref, k_ref, v_ref, o_ref):
    q = q_ref[...]
    tk = k_ref.shape[0] // MLA_KV_CHUNKS
    m = acc = None
    for c in range(MLA_KV_CHUNKS):
        s = _dot_nt(q, k_ref[c * tk:(c + 1) * tk, :])
        m_c = jnp.max(s, axis=-1, keepdims=True)
        if c == 0:
            m = m_c
            acc = _dot(jnp.exp2(s - m).astype(BF16), v_ref[:tk, :])
        else:
            m_new = jnp.maximum(m, m_c)
            acc = jnp.exp2(m - m_new) * acc + _dot(jnp.exp2(s - m_new).astype(BF16),
                                                   v_ref[c * tk:(c + 1) * tk, :])
            m = m_new
    o_ref[...] = (acc[:, :MLA_V] / acc[:, MLA_V:]).astype(o_ref.dtype)


def _mla_attention(q, k, v, batch, seq, tq=512):
    t = q.shape[1]
    nq = seq // tq
    return pl.pallas_call(
        _mla_attn_kernel,
        grid=(batch, MLA_HEADS, nq),
        in_specs=[pl.BlockSpec((None, tq, MLA_QK_PAD), lambda b, h, i: (h, b * nq + i, 0)),
                  pl.BlockSpec((None, seq, MLA_QK_PAD), lambda b, h, i: (h, b, 0)),
                  pl.BlockSpec((None, seq, 2 * MLA_V), lambda b, h, i: (h, b, 0))],
        out_specs=pl.BlockSpec((tq, MLA_V), lambda b, h, i: (b * nq + i, h)),
        out_shape=jax.ShapeDtypeStruct((t, MLA_HEADS * MLA_V), BF16),
        compiler_params=_params("parallel", "parallel", "arbitrary"),
        name="mla_attention",
    )(q, k, v)


def _merge_kernel(ona_ref, odl_ref, omla_ref, wna_ref, wdl_ref, wmla_ref, g0_ref, g1_ref, g2_ref, o_ref):
    ona, odl, omla = ona_ref[...], odl_ref[...], omla_ref[...]
    for c in range(o_ref.shape[1] // GU_SPLIT):
        cols = slice(c * GU_SPLIT, (c + 1) * GU_SPLIT)
        acc = g0_ref[:, cols].astype(F32) * _dot(ona, wna_ref[:, cols])
        acc = acc + g1_ref[:, cols].astype(F32) * _dot(odl, wdl_ref[:, cols])
        acc = acc + g2_ref[:, cols].astype(F32) * _dot(omla, wmla_ref[:, cols])
        o_ref[:, cols] = acc.astype(o_ref.dtype)


def _merge(o_na, o_dl, o_mla, w_na, w_dl, w_mla, gates, tm=1024, tn=512):
    t = o_na.shape[0]
    d = w_na.shape[1]
    nj = d // tn

    def act(a):
        return pl.BlockSpec((tm, a.shape[1]), lambda i, j: (i, 0))

    def wgt(w):
        return pl.BlockSpec((w.shape[0], tn), lambda i, j: (0, j))

    def gate(idx):
        return pl.BlockSpec((tm, tn), lambda i, j: (i, idx * nj + j))

    return pl.pallas_call(
        _merge_kernel,
        grid=(t // tm, nj),
        in_specs=[act(o_na), act(o_dl), act(o_mla), wgt(w_na), wgt(w_dl), wgt(w_mla),
                  gate(0), gate(1), gate(2)],
        out_specs=pl.BlockSpec((tm, tn), lambda i, j: (i, j)),
        out_shape=jax.ShapeDtypeStruct((t, d), BF16),
        compiler_params=_params("parallel", "parallel"),
        name="branch_merge",
    )(o_na, o_dl, o_mla, w_na, w_dl, w_mla, gates, gates, gates)


GU_SPLIT = 256


def _swiglu_pieces(h, wg_ref, wu_ref, o_ref):
    for c in range(o_ref.shape[1] // GU_SPLIT):
        cols = slice(c * GU_SPLIT, (c + 1) * GU_SPLIT)
        g = _dot(h, wg_ref[:, cols].astype(BF16))
        u = _dot(h, wu_ref[:, cols].astype(BF16))
        o_ref[:, cols] = (g * jax.nn.sigmoid(g) * u).astype(o_ref.dtype)


def _gu_kernel(h_ref, wg_ref, wu_ref, o_ref):
    _swiglu_pieces(h_ref[...], wg_ref, wu_ref, o_ref)


def _swiglu_up(h, w_gu, tm=1024, tn=512):
    t, d = h.shape
    ff = w_gu.shape[1] // 2
    nj = ff // tn
    return pl.pallas_call(
        _gu_kernel,
        grid=(t // tm, nj),
        in_specs=[pl.BlockSpec((tm, d), lambda i, j: (i, 0)),
                  pl.BlockSpec((d, tn), lambda i, j: (0, j)),
                  pl.BlockSpec((d, tn), lambda i, j: (0, nj + j))],
        out_specs=pl.BlockSpec((tm, tn), lambda i, j: (i, j)),
        out_shape=jax.ShapeDtypeStruct((t, ff), BF16),
        compiler_params=_params("parallel", "parallel"),
        name="swiglu_up",
    )(h, w_gu, w_gu)


MOE_TM = 1024
TOP_K = 2


def _moe_routing(route, tokens):
    e_flat = jnp.concatenate([route[:, N_EXPERTS], route[:, N_EXPERTS + 1]]).astype(jnp.int32)
    onehot = (e_flat[:, None] == jnp.arange(N_EXPERTS, dtype=jnp.int32)[None, :]).astype(jnp.int32)
    csum = jnp.cumsum(onehot, axis=0)
    rank = jnp.sum((csum - onehot) * onehot, axis=1)
    padded = ((csum[-1] + MOE_TM - 1) // MOE_TM) * MOE_TM
    ends = jnp.cumsum(padded)
    slot = jnp.sum(onehot * (ends - padded)[None, :], axis=1) + rank
    n_tiles = (TOP_K * tokens) // MOE_TM + N_EXPERTS
    tile_start = jnp.arange(n_tiles, dtype=jnp.int32) * MOE_TM
    tile_expert = jnp.minimum(jnp.sum(tile_start[:, None] >= ends[None, :], axis=1), N_EXPERTS - 1)
    te = jnp.concatenate([tile_expert, ends[-1:] // MOE_TM]).astype(jnp.int32)
    return slot.astype(jnp.int32), te


def _dispatch_kernel(slot_ref, x_ref, g_ref, sc_ref, sh_ref, dst_in_ref, dst_ref, h_ref, sem, *, rows, tokens):
    del dst_in_ref
    base = pl.program_id(0) * rows
    h_ref[...] = _modulated_norm(x_ref[...], g_ref[...], sc_ref[...], sh_ref[...])

    def row_copy(r, choice):
        slot = slot_ref[choice * tokens + base + r]
        return pltpu.make_async_copy(h_ref.at[pl.ds(r, 1), :], dst_ref.at[pl.ds(slot, 1), :], sem)

    def start(r, carry):
        row_copy(r, 0).start()
        row_copy(r, 1).start()
        return carry

    def wait(r, carry):
        row_copy(r, 0).wait()
        row_copy(r, 1).wait()
        return carry

    lax.fori_loop(0, rows, start, 0, unroll=8)
    lax.fori_loop(0, rows, wait, 0, unroll=8)


def _moe_dispatch(x, g, sc, sh, slot, seq, n_rows, rows=256):
    t, d = x.shape
    per = seq // rows
    return pl.pallas_call(
        functools.partial(_dispatch_kernel, rows=rows, tokens=t),
        grid_spec=pltpu.PrefetchScalarGridSpec(
            num_scalar_prefetch=1,
            grid=(t // rows,),
            in_specs=[pl.BlockSpec((rows, d), lambda i, s: (i, 0)),
                      pl.BlockSpec((1, d), lambda i, s: (0, 0)),
                      pl.BlockSpec((None, 1, d), lambda i, s: (i // per, 0, 0)),
                      pl.BlockSpec((None, 1, d), lambda i, s: (i // per, 0, 0)),
                      pl.BlockSpec(memory_space=pl.ANY)],
            out_specs=pl.BlockSpec(memory_space=pl.ANY),
            scratch_shapes=[pltpu.VMEM((rows, d), F32), pltpu.SemaphoreType.DMA(())]),
        out_shape=jax.ShapeDtypeStruct((n_rows, d), F32),
        input_output_aliases={5: 0},
        compiler_params=_params("arbitrary"),
        name="moe_dispatch",
    )(slot, x, g.reshape(1, d), sc, sh, jnp.zeros((n_rows, d), F32))


def _moe_up_kernel(te_ref, xs_ref, wg_ref, wu_ref, o_ref, hb_ref, *, n_tiles):
    used = pl.program_id(0) < te_ref[n_tiles]

    @pl.when(used & (pl.program_id(1) == 0))
    def _():
        hb_ref[...] = xs_ref[...].astype(BF16)

    @pl.when(used)
    def _():
        _swiglu_pieces(hb_ref[...], wg_ref, wu_ref, o_ref)

    @pl.when(jnp.logical_not(used))
    def _():
        o_ref[...] = jnp.zeros(o_ref.shape, o_ref.dtype)


def _moe_up(xs, w_gu, te, tn=512):
    n_rows, d = xs.shape
    ff = w_gu.shape[2] // 2
    nj = ff // tn
    n_tiles = n_rows // MOE_TM

    def last_used(i, te_ref):
        return jnp.minimum(i, te_ref[n_tiles] - 1)

    return pl.pallas_call(
        functools.partial(_moe_up_kernel, n_tiles=n_tiles),
        grid_spec=pltpu.PrefetchScalarGridSpec(
            num_scalar_prefetch=1,
            grid=(n_tiles, nj),
            in_specs=[pl.BlockSpec((MOE_TM, d), lambda i, j, te_ref: (last_used(i, te_ref), 0)),
                      pl.BlockSpec((None, d, tn), lambda i, j, te_ref: (te_ref[i], 0, j)),
                      pl.BlockSpec((None, d, tn), lambda i, j, te_ref: (te_ref[i], 0, nj + j))],
            out_specs=pl.BlockSpec((MOE_TM, tn), lambda i, j, te_ref: (i, j)),
            scratch_shapes=[pltpu.VMEM((MOE_TM, d), BF16)]),
        out_shape=jax.ShapeDtypeStruct((n_rows, ff), BF16),
        compiler_params=_params("parallel", "arbitrary"),
        name="moe_up",
    )(te, xs, w_gu, w_gu)


def _moe_down_kernel(te_ref, a_ref, w_ref, o_ref, *, n_tiles):
    used = pl.program_id(0) < te_ref[n_tiles]

    @pl.when(used)
    def _():
        o_ref[...] = _dot(a_ref[...], w_ref[...])

    @pl.when(jnp.logical_not(used))
    def _():
        o_ref[...] = jnp.zeros(o_ref.shape, o_ref.dtype)


def _moe_down(act, w_down, te, tn=256):
    n_rows, ff = act.shape
    d = w_down.shape[2]
    n_tiles = n_rows // MOE_TM

    def last_used(i, te_ref):
        return jnp.minimum(i, te_ref[n_tiles] - 1)

    return pl.pallas_call(
        functools.partial(_moe_down_kernel, n_tiles=n_tiles),
        grid_spec=pltpu.PrefetchScalarGridSpec(
            num_scalar_prefetch=1,
            grid=(n_tiles, d // tn),
            in_specs=[pl.BlockSpec((MOE_TM, ff), lambda i, j, te_ref: (last_used(i, te_ref), 0)),
                      pl.BlockSpec((None, ff, tn), lambda i, j, te_ref: (te_ref[i], 0, j))],
            out_specs=pl.BlockSpec((MOE_TM, tn), lambda i, j, te_ref: (i, j))),
        out_shape=jax.ShapeDtypeStruct((n_rows, d), F32),
        compiler_params=_params("parallel", "arbitrary"),
        name="moe_down",
    )(te, act, w_down)


def _combine_kernel(slot_ref, x_ref, gate_ref, w_ref, ys_ref, o_ref, buf_ref, sem, *, rows, tokens):
    base = pl.program_id(0) * rows

    def row_copy(r, choice):
        slot = slot_ref[choice * tokens + base + r]
        return pltpu.make_async_copy(ys_ref.at[pl.ds(slot, 1), :], buf_ref.at[choice, pl.ds(r, 1), :], sem)

    def start(r, carry):
        row_copy(r, 0).start()
        row_copy(r, 1).start()
        return carry

    def wait(r, carry):
        row_copy(r, 0).wait()
        row_copy(r, 1).wait()
        return carry

    lax.fori_loop(0, rows, start, 0, unroll=8)
    lax.fori_loop(0, rows, wait, 0, unroll=8)
    w = w_ref[...]
    y = w[:, 0:1] * buf_ref[0] + w[:, 1:2] * buf_ref[1]
    o_ref[...] = x_ref[...] + gate_ref[...] * y


def _moe_combine(x, gate, w12, ys, slot, seq, rows=256):
    t, d = x.shape
    per = seq // rows
    return pl.pallas_call(
        functools.partial(_combine_kernel, rows=rows, tokens=t),
        grid_spec=pltpu.PrefetchScalarGridSpec(
            num_scalar_prefetch=1,
            grid=(t // rows,),
            in_specs=[pl.BlockSpec((rows, d), lambda i, s: (i, 0)),
                      pl.BlockSpec((None, 1, d), lambda i, s: (i // per, 0, 0)),
                      pl.BlockSpec((rows, TOP_K), lambda i, s: (i, 0)),
                      pl.BlockSpec(memory_space=pl.ANY)],
            out_specs=pl.BlockSpec((rows, d), lambda i, s: (i, 0)),
            scratch_shapes=[pltpu.VMEM((TOP_K, rows, d), F32), pltpu.SemaphoreType.DMA(())]),
        out_shape=jax.ShapeDtypeStruct((t, d), F32),
        compiler_params=_params("arbitrary"),
        name="moe_combine",
    )(slot, x, gate, w12, ys)


def _router_kernel(x_ref, g_ref, sc_ref, sh_ref, w_ref, o_ref):
    h = _modulated_norm(x_ref[...], g_ref[...], sc_ref[...], sh_ref[...])
    logits = jnp.dot(h, w_ref[...], preferred_element_type=F32, precision=lax.Precision.HIGHEST)
    lane = lax.broadcasted_iota(jnp.int32, logits.shape, 1).astype(F32)
    lg = jnp.where(lane < N_EXPERTS, logits, NEG_INF)
    m1 = jnp.max(lg, axis=-1, keepdims=True)
    i1 = jnp.min(jnp.where(lg == m1, lane, float(LANES)), axis=-1, keepdims=True)
    lg2 = jnp.where(lane == i1, NEG_INF, lg)
    m2 = jnp.max(lg2, axis=-1, keepdims=True)
    i2 = jnp.min(jnp.where(lg2 == m2, lane, float(LANES)), axis=-1, keepdims=True)
    e2 = jnp.exp(m2 - m1)
    z = 1.0 + e2
    out = jnp.where(lane == N_EXPERTS, i1, 0.0) + jnp.where(lane == N_EXPERTS + 1, i2, 0.0)
    out = out + jnp.where(lane == N_EXPERTS + 2, 1.0 / z, 0.0) + jnp.where(lane == N_EXPERTS + 3, e2 / z, 0.0)
    o_ref[...] = out


def _router(x, g, sc, sh, w_router_pad, seq, tm=512):
    t, d = x.shape
    per = seq // tm
    return pl.pallas_call(
        _router_kernel,
        grid=(t // tm,),
        in_specs=[pl.BlockSpec((tm, d), lambda i: (i, 0)),
                  pl.BlockSpec((1, d), lambda i: (0, 0)),
                  pl.BlockSpec((None, 1, d), lambda i: (i // per, 0, 0)),
                  pl.BlockSpec((None, 1, d), lambda i: (i // per, 0, 0)),
                  pl.BlockSpec((d, LANES), lambda i: (0, 0))],
        out_specs=pl.BlockSpec((tm, LANES), lambda i: (i, 0)),
        out_shape=jax.ShapeDtypeStruct((t, LANES), F32),
        compiler_params=_params("parallel"),
        name="router",
    )(x, g.reshape(1, d), sc, sh, w_router_pad)


MM_TM = 1024
MM_TN = 1024
MM_SPLIT = 256


def _col_tile(n):
    return MM_TN if n % MM_TN == 0 else MM_TN // 2


def _residual_matmul(a, w, x, gate, seq, name, tm=MM_TM, tn=None):
    per = seq // tm
    tn = _col_tile(w.shape[1]) if tn is None else tn
    return _matmul(a, w, tm=tm, tn=tn, out_dtype=F32, epilogue=_ep_residual, split=MM_SPLIT,
                   extras=[(x, (tm, tn), lambda i, j: (i, j)),
                           (gate, (None, 1, tn), lambda i, j: (i // per, 0, j))],
                   name=name)


def _token_mixer(h, x, gate, lw, batch, seq, tables):
    d = h.shape[1]
    w_in = lw["w_in"]
    tm = MM_TM
    per = seq // tm
    cos_d, sin_d, cos_m, sin_m = tables

    def cols(lo, hi):
        return w_in[:, lo:hi].astype(BF16)

    def tile_gain(gq, gk, heads):
        gq = gq * (LOG2E / math.sqrt(HEAD_DIM))
        return jnp.concatenate([jnp.tile(gq, heads), jnp.tile(gk, heads)]).reshape(1, -1).astype(F32)

    o = 0
    tn = _col_tile(2 * NA_W)
    na_qk = _matmul(h, cols(o, o + 2 * NA_W), tm=tm, tn=tn, out_dtype=BF16, epilogue=_ep_headnorm,
                    split=MM_SPLIT,
                    extras=[(tile_gain(lw["na_q_norm"], lw["na_k_norm"], NA_HEADS), (1, tn),
                             lambda i, j: (0, j))], name="na_qk_proj")
    o += 2 * NA_W
    na_v = _matmul(h, cols(o, o + NA_W), tm=tm, tn=_col_tile(NA_W), out_dtype=BF16, split=MM_SPLIT, name="na_v_proj")
    o += NA_W
    tn = _col_tile(2 * DIL_W)
    dl_qk = _matmul(h, cols(o, o + 2 * DIL_W), tm=tm, tn=tn, out_dtype=F32, epilogue=_ep_headnorm_rope,
                    split=MM_SPLIT,
                    extras=[(tile_gain(lw["dil_q_norm"], lw["dil_k_norm"], DIL_HEADS), (1, tn),
                             lambda i, j: (0, j)),
                            (cos_d, (tm, LANES), lambda i, j: (i % per, 0)),
                            (sin_d, (tm, LANES), lambda i, j: (i % per, 0)),
                            (_pair_swap_matrix(MM_SPLIT, ROT_DIM // 2), (MM_SPLIT, MM_SPLIT),
                             lambda i, j: (0, 0))], name="dil_qk_proj")
    o += 2 * DIL_W
    dl_v = _matmul(h, cols(o, o + DIL_W), tm=tm, tn=_col_tile(DIL_W), out_dtype=F32, split=MM_SPLIT, name="dil_v_proj")
    o += DIL_W
    cq_n = _matmul(h, cols(o, o + MLA_Q_RANK), tm=tm, tn=MLA_Q_RANK, out_dtype=BF16, epilogue=_ep_rownorm,
                   extras=[(lw["mla_q_a_norm"].reshape(1, -1), (1, MLA_Q_RANK), lambda i, j: (0, 0))],
                   name="mla_cq_proj")
    o += MLA_Q_RANK
    ckv_n = _matmul(h, cols(o, o + MLA_KV_RANK), tm=tm, tn=MLA_KV_RANK, out_dtype=BF16, epilogue=_ep_rownorm,
                    extras=[(lw["mla_kv_a_norm"].reshape(1, -1), (1, MLA_KV_RANK), lambda i, j: (0, 0))],
                    name="mla_ckv_proj")
    o += MLA_KV_RANK
    w_kr = jnp.pad(w_in[:, o:o + MLA_ROPE], ((0, 0), (0, LANES - MLA_ROPE))).astype(BF16)
    k_r = _matmul(h, w_kr, tm=tm, tn=LANES, out_dtype=F32, name="mla_kr_proj")
    o += MLA_ROPE
    gates = _matmul(h, cols(o, o + 3 * d), tm=tm, tn=_col_tile(3 * d), out_dtype=BF16, epilogue=_ep_sigmoid,
                    split=MM_SPLIT, name="gate_proj")

    o_na = _na_attention(na_qk, na_v, lw["na_rpb"], batch, seq)
    o_dl = _dilated_attention(dl_qk, dl_v, batch, seq)
    w_uq = lw["mla_w_uq"].reshape(MLA_Q_RANK, MLA_HEADS, MLA_QK).transpose(1, 0, 2)
    w_uq = jnp.pad(w_uq, ((0, 0), (0, 0), (0, MLA_QK_PAD - MLA_QK))).astype(BF16)
    w_ukv = lw["mla_w_ukv"].reshape(MLA_KV_RANK, MLA_HEADS, MLA_NOPE + MLA_V).transpose(1, 0, 2).astype(BF16)
    gq = jnp.pad(lw["mla_q_norm"] * (LOG2E / math.sqrt(MLA_QK)), (0, MLA_QK_PAD - MLA_QK)).reshape(1, MLA_QK_PAD)
    gk = lw["mla_k_norm"]
    gk0 = gk[:MLA_NOPE].reshape(1, LANES)
    gk1 = jnp.pad(gk[MLA_NOPE:], (0, LANES - MLA_ROPE)).reshape(1, LANES)
    q_m = _mla_q_proj(cq_n, w_uq, gq, cos_m, sin_m, seq)
    k_m, v_m = _mla_kv_proj(ckv_n, w_ukv, k_r, gk0, gk1, cos_m, sin_m, seq)
    o_mla = _mla_attention(q_m, k_m, v_m, batch, seq)

    merged = _merge(o_na, o_dl, o_mla, lw["w_branch_na"].astype(BF16), lw["w_branch_dil"].astype(BF16),
                    lw["w_branch_mla"].astype(BF16), gates)
    return _residual_matmul(merged, lw["w_out"].astype(BF16), x, gate, seq, "out_proj")


def kernel(x, c, w_ada, b_ada, norm_mix, norm_ffn, w_in, na_q_norm, na_k_norm, na_rpb, dil_q_norm, dil_k_norm, mla_q_a_norm, mla_w_uq, mla_kv_a_norm, mla_w_ukv, mla_q_norm, mla_k_norm, w_branch_na, w_branch_dil, w_branch_mla, w_out, ffn_w_gu, ffn_w_down, moe_router, moe_w_gu, moe_w_down):
    batch, seq, d = x.shape
    depth = w_ada.shape[0]
    assert seq % MM_TM == 0 and seq % GRID_W == 0
    t = batch * seq
    xf = x.reshape(t, d)

    c_pad = jnp.pad(c, ((0, (-batch) % 8), (0, 0)))
    mod = _adaln(c_pad, w_ada, b_ada)[:, :batch].reshape(depth, batch, 6, 1, d)
    tables = _rope_tables(seq, ROT_DIM, 1.0) + _rope_tables(seq, MLA_ROPE, 1.0)

    for l in range(depth):
        sh1, sc1, g1, sh2, sc2, g2 = [mod[l, :, i] for i in range(6)]
        lw = dict(w_in=w_in[l], na_q_norm=na_q_norm[l], na_k_norm=na_k_norm[l], na_rpb=na_rpb[l],
                  dil_q_norm=dil_q_norm[l], dil_k_norm=dil_k_norm[l], mla_q_a_norm=mla_q_a_norm[l],
                  mla_w_uq=mla_w_uq[l], mla_kv_a_norm=mla_kv_a_norm[l], mla_w_ukv=mla_w_ukv[l],
                  mla_q_norm=mla_q_norm[l], mla_k_norm=mla_k_norm[l], w_branch_na=w_branch_na[l],
                  w_branch_dil=w_branch_dil[l], w_branch_mla=w_branch_mla[l], w_out=w_out[l])
        h = _normmod(xf, norm_mix[l], sc1, sh1, seq)
        xf = _token_mixer(h, xf, g1, lw, batch, seq, tables)

        if l % 2 == 0:
            h = _normmod(xf, norm_ffn[l], sc2, sh2, seq)
            act = _swiglu_up(h, ffn_w_gu[l // 2].astype(BF16))
            xf = _residual_matmul(act, ffn_w_down[l // 2].astype(BF16), xf, g2, seq, "down_proj", tm=512, tn=512)
        else:
            w_r = jnp.pad(moe_router[l // 2], ((0, 0), (0, LANES - N_EXPERTS)))
            route = _router(xf, norm_ffn[l], sc2, sh2, w_r, seq)
            slot, te = _moe_routing(route, t)
            n_rows = TOP_K * t + N_EXPERTS * MOE_TM
            xs = _moe_dispatch(xf, norm_ffn[l], sc2, sh2, slot, seq, n_rows)
            act = _moe_up(xs, moe_w_gu[l // 2], te)
            ys = _moe_down(act, moe_w_down[l // 2].astype(BF16), te)
            xf = _moe_combine(xf, g2, route[:, N_EXPERTS + 2:N_EXPERTS + 4], ys, slot, seq)
    return xf.reshape(batch, seq, d)
```

```python
import functools
import math

import numpy as np
import jax
import jax.numpy as jnp
from jax import lax
from jax.experimental import pallas as pl
from jax.experimental.pallas import tpu as pltpu

GRID_W = 64
HEAD_DIM = 128
ROPE_THETA = 500000.0
ROT_DIM = HEAD_DIM // 4
RMS_EPS = 1e-6
NEG_INF = -1e30
NA_HEADS = 8
NA_KH = 8
NA_KW = 16
DIL_GROUPS = ((128, 1), (512, 4), (2048, 16))
DIL_HEADS_PER_GROUP = 4
DIL_HEADS = DIL_HEADS_PER_GROUP * len(DIL_GROUPS)
DIL_RADIUS = 64
MLA_HEADS = 8
MLA_NOPE = 128
MLA_ROPE = 64
MLA_V = 128
MLA_Q_RANK = 768
MLA_KV_RANK = 512
MLA_QK = MLA_NOPE + MLA_ROPE
MLA_QK_PAD = 256
N_EXPERTS = 8
LANES = 128

NA_W = NA_HEADS * HEAD_DIM
DIL_W = DIL_HEADS * HEAD_DIM
DIL_OUT = DIL_HEADS_PER_GROUP * HEAD_DIM

VMEM_LIMIT_BYTES = 48 * 1024 * 1024

F32 = jnp.float32
BF16 = jnp.bfloat16
LOG2E = math.log2(math.e)
LN2 = math.log(2.0)


def _params(*sem):
    return pltpu.CompilerParams(dimension_semantics=sem, vmem_limit_bytes=VMEM_LIMIT_BYTES)


def _dot(a, b):
    return jnp.dot(a, b, preferred_element_type=F32)


def _dot_nt(a, b):
    return lax.dot_general(a, b, (((1,), (1,)), ((), ())), preferred_element_type=F32)


def _adaln_kernel(c_ref, w_ref, b_ref, o_ref):
    o_ref[...] = _dot(c_ref[...], w_ref[...]) + b_ref[...]


def _adaln(c_pad, w_ada, b_ada):
    depth, d, n = w_ada.shape
    tn = 1024
    return pl.pallas_call(
        _adaln_kernel,
        grid=(depth, n // tn),
        in_specs=[pl.BlockSpec(c_pad.shape, lambda l, j: (0, 0)),
                  pl.BlockSpec((None, d, tn), lambda l, j: (l, 0, j)),
                  pl.BlockSpec((None, 1, tn), lambda l, j: (l, 0, j))],
        out_specs=pl.BlockSpec((None, c_pad.shape[0], tn), lambda l, j: (l, 0, j)),
        out_shape=jax.ShapeDtypeStruct((depth, c_pad.shape[0], n), F32),
        compiler_params=_params("parallel", "parallel"),
        name="adaln",
    )(c_pad, w_ada, b_ada.reshape(depth, 1, n))


def _modulated_norm(x, g, sc, sh):
    ms = jnp.mean(x * x, axis=-1, keepdims=True)
    return (x * lax.rsqrt(ms + RMS_EPS) * g) * (1.0 + sc) + sh


def _normmod_kernel(x_ref, g_ref, sc_ref, sh_ref, o_ref):
    o_ref[...] = _modulated_norm(x_ref[...], g_ref[...], sc_ref[...], sh_ref[...]).astype(o_ref.dtype)


def _normmod(x, g, sc, sh, seq, tm=512):
    t, d = x.shape
    per = seq // tm
    return pl.pallas_call(
        _normmod_kernel,
        grid=(t // tm,),
        in_specs=[pl.BlockSpec((tm, d), lambda i: (i, 0)),
                  pl.BlockSpec((1, d), lambda i: (0, 0)),
                  pl.BlockSpec((None, 1, d), lambda i: (i // per, 0, 0)),
                  pl.BlockSpec((None, 1, d), lambda i: (i // per, 0, 0))],
        out_specs=pl.BlockSpec((tm, d), lambda i: (i, 0)),
        out_shape=jax.ShapeDtypeStruct((t, d), BF16),
        compiler_params=_params("parallel"),
        name="normmod",
    )(x, g.reshape(1, d), sc, sh)


def _mm_kernel(*refs, n_extra, epilogue, split):
    a_ref, b_ref = refs[0], refs[1]
    extras = refs[2:2 + n_extra]
    o_ref = refs[2 + n_extra]
    a = a_ref[...]
    n_pieces = b_ref.shape[1] // split
    acc = _dot(a, b_ref[:, :split])
    for c in range(n_pieces):
        nxt = _dot(a, b_ref[:, (c + 1) * split:(c + 2) * split]) if c + 1 < n_pieces else None
        epilogue(acc, extras, o_ref, slice(c * split, (c + 1) * split))
        acc = nxt


def _ep_store(acc, extras, o_ref, cols):
    o_ref[:, cols] = acc.astype(o_ref.dtype)


def _matmul(a, b, *, tm, tn, out_dtype, epilogue=_ep_store, extras=(), split=None, name):
    m, kdim = a.shape
    n = b.shape[1]
    split = tn if split is None else split
    assert m % tm == 0 and n % tn == 0 and tn % split == 0
    in_specs = [pl.BlockSpec((tm, kdim), lambda i, j: (i, 0)),
                pl.BlockSpec((kdim, tn), lambda i, j: (0, j))]
    in_specs += [pl.BlockSpec(bs, im) for (_, bs, im) in extras]
    return pl.pallas_call(
        functools.partial(_mm_kernel, n_extra=len(extras), epilogue=epilogue, split=split),
        grid=(m // tm, n // tn),
        in_specs=in_specs,
        out_specs=pl.BlockSpec((tm, tn), lambda i, j: (i, j)),
        out_shape=jax.ShapeDtypeStruct((m, n), out_dtype),
        compiler_params=_params("parallel", "parallel"),
        name=name,
    )(a, b, *[e[0] for e in extras])


def _head_rms(blk, gain):
    ms = jnp.mean(blk * blk, axis=-1, keepdims=True)
    return blk * lax.rsqrt(ms + RMS_EPS) * gain


def _head_slices(cols):
    return [slice(c, c + HEAD_DIM) for c in range(cols.start, cols.stop, HEAD_DIM)]


def _ep_headnorm(acc, extras, o_ref, cols):
    for i, sl in enumerate(_head_slices(cols)):
        blk = acc[:, i * HEAD_DIM:(i + 1) * HEAD_DIM]
        o_ref[:, sl] = _head_rms(blk, extras[0][:, sl]).astype(o_ref.dtype)


def _rotate_pairs(y, cos_t, sin_t, half):
    lane = lax.broadcasted_iota(jnp.int32, y.shape, 1)
    swapped = jnp.where(lane < half, pltpu.roll(y, LANES - half, 1), pltpu.roll(y, half, 1))
    return y * cos_t + swapped * sin_t


def _pair_swap_matrix(width, half):
    p = np.zeros((width, width), np.float32)
    for base in range(0, width, HEAD_DIM):
        for i in range(half):
            p[base + i + half, base + i] = 1.0
            p[base + i, base + i + half] = 1.0
    return jnp.asarray(p, BF16)


def _ep_headnorm_rope(acc, extras, o_ref, cols):
    cos_t = extras[1][...]
    sin_t = extras[2][...]
    heads = _head_slices(cols)
    y = jnp.concatenate([_head_rms(acc[:, i * HEAD_DIM:(i + 1) * HEAD_DIM], extras[0][:, sl])
                         for i, sl in enumerate(heads)], axis=1)
    swapped = _dot(y.astype(BF16), extras[3][...])
    for i, sl in enumerate(heads):
        loc = slice(i * HEAD_DIM, (i + 1) * HEAD_DIM)
        o_ref[:, sl] = (y[:, loc] * cos_t + swapped[:, loc] * sin_t).astype(o_ref.dtype)


def _ep_rownorm(acc, extras, o_ref, cols):
    o_ref[:, cols] = _head_rms(acc, extras[0][:, cols]).astype(o_ref.dtype)


def _ep_sigmoid(acc, extras, o_ref, cols):
    o_ref[:, cols] = jax.nn.sigmoid(acc).astype(o_ref.dtype)


def _ep_residual(acc, extras, o_ref, cols):
    o_ref[:, cols] = extras[0][:, cols] + extras[1][:, cols] * acc


def _rope_tables(seq, rot, fill):
    half = rot // 2
    inv = ROPE_THETA ** (-jnp.arange(half, dtype=F32) * (2.0 / rot))
    ang = jnp.arange(seq, dtype=jnp.int32).astype(F32)[:, None] * inv[None, :]
    cos, sin = jnp.cos(ang), jnp.sin(ang)
    pad = LANES - rot
    cos_t = jnp.concatenate([cos, cos, jnp.full((seq, pad), fill, F32)], axis=1)
    sin_t = jnp.concatenate([-sin, sin, jnp.zeros((seq, pad), F32)], axis=1)
    return cos_t, sin_t


NA_RB = 4
NA_KROWS = NA_KH + NA_RB
NA_GROUP = 2


def _na_geometry(rows):
    kh = min(NA_KH, rows)
    assert rows % (NA_RB * NA_GROUP) == 0 and rows >= NA_KROWS
    patterns, ids, kstarts = [], [], []
    for r in range(0, rows, NA_RB):
        ks = int(np.clip(r - NA_KH // 2, 0, rows - NA_KROWS))
        pat = tuple((int(np.clip(r + a - NA_KH // 2, 0, rows - kh)) - ks, ks - (r + a)) for a in range(NA_RB))
        if pat not in patterns:
            patterns.append(pat)
        ids.append(patterns.index(pat))
        kstarts.append(ks)
    return patterns, np.array(ids + kstarts, np.int32)


def _na_bias_table(rpb, rows, patterns):
    h = rpb.shape[0]
    kh = min(NA_KH, rows)
    qc = np.arange(GRID_W)
    kc = np.arange(GRID_W)
    cstart = np.clip(qc - NA_KW // 2, 0, GRID_W - NA_KW)
    ok = (kc[None, :] >= cstart[:, None]) & (kc[None, :] < cstart[:, None] + NA_KW)
    dc = np.clip(kc[None, :] - qc[:, None] + NA_KW - 1, 0, 2 * NA_KW - 2)
    by_col = rpb.astype(F32)[:, :, dc] * LOG2E + jnp.where(ok, 0.0, NEG_INF).astype(F32)
    kr = np.arange(NA_KROWS)
    dr = np.zeros((len(patterns), NA_RB, NA_KROWS), np.int32)
    row_ok = np.zeros((len(patterns), NA_RB, NA_KROWS), bool)
    for p, pat in enumerate(patterns):
        for a, (start, offset) in enumerate(pat):
            row_ok[p, a] = (kr >= start) & (kr < start + kh)
            dr[p, a] = np.clip(kr + offset + NA_KH - 1, 0, 2 * NA_KH - 2)
    tab = jnp.take(by_col, jnp.asarray(dr.reshape(-1)), axis=1)
    tab = tab.reshape(h, len(patterns), NA_RB, NA_KROWS, GRID_W, GRID_W)
    tab = jnp.where(row_ok[None, :, :, :, None, None], tab, NEG_INF)
    tab = tab.transpose(0, 1, 2, 4, 3, 5)
    return tab.reshape(h, len(patterns), NA_RB * GRID_W, NA_KROWS * GRID_W)


def _na_kernel(geo_ref, q_ref, k_ref, v_ref, b_ref, o_ref, *, n_blocks):
    qn = NA_RB * GRID_W
    kn = NA_KROWS * GRID_W
    ones = jnp.ones((kn, HEAD_DIM), BF16)

    def scores(blk):
        q0 = pl.multiple_of(blk * qn, qn)
        k0 = pl.multiple_of(geo_ref[n_blocks + blk] * GRID_W, GRID_W)
        s = _dot_nt(q_ref[pl.ds(q0, qn), :], k_ref[pl.ds(k0, kn), :]) + b_ref[geo_ref[blk]]
        return s, q0, k0

    def finish(s, q0, k0):
        v1 = jnp.concatenate([v_ref[pl.ds(k0, kn), :], ones], axis=1)
        m = jnp.max(s, axis=-1, keepdims=True)
        acc = _dot(jnp.exp2(s - m).astype(BF16), v1)
        o_ref[pl.ds(q0, qn), :] = (acc[:, :HEAD_DIM] / acc[:, HEAD_DIM:]).astype(o_ref.dtype)

    def body(i, carry):
        pending = [scores(i * NA_GROUP + j) for j in range(NA_GROUP)]
        for item in pending:
            finish(*item)
        return carry

    lax.fori_loop(0, n_blocks // NA_GROUP, body, 0)


def _na_attention(qk, v, rpb, batch, seq):
    rows = seq // GRID_W
    patterns, geo = _na_geometry(rows)
    bias = _na_bias_table(rpb, rows, patterns)
    t = qk.shape[0]
    blk = (seq, HEAD_DIM)
    return pl.pallas_call(
        functools.partial(_na_kernel, n_blocks=rows // NA_RB),
        grid_spec=pltpu.PrefetchScalarGridSpec(
            num_scalar_prefetch=1,
            grid=(batch, NA_HEADS),
            in_specs=[pl.BlockSpec(blk, lambda b, h, geo_ref: (b, h)),
                      pl.BlockSpec(blk, lambda b, h, geo_ref: (b, NA_HEADS + h)),
                      pl.BlockSpec(blk, lambda b, h, geo_ref: (b, h)),
                      pl.BlockSpec((None,) + bias.shape[1:], lambda b, h, geo_ref: (h, 0, 0, 0))],
            out_specs=pl.BlockSpec(blk, lambda b, h, geo_ref: (b, h))),
        out_shape=jax.ShapeDtypeStruct((t, NA_W), BF16),
        compiler_params=_params("parallel", "parallel"),
        name="na_attention",
    )(jnp.asarray(geo), qk, qk, v, bias)


DIL_QCHUNK = 2 * DIL_RADIUS
DIL_KWIN = 4 * DIL_RADIUS
DIL_GROUP = 4


def _dil_window_mask():
    r = np.arange(DIL_QCHUNK)[None, :, None]
    c = np.arange(DIL_KWIN)[None, None, :]
    off = (np.arange(3) * DIL_RADIUS)[:, None, None]
    return jnp.asarray(np.where(np.abs(c - r - off) <= DIL_RADIUS, 0.0, NEG_INF), F32)


def _dil_kernel(q_ref, k_ref, v_ref, mask_ref, o_ref, lse_ref, *, seq, dil):
    length = seq // dil
    n_chunk = length // DIL_QCHUNK
    ones = jnp.ones((DIL_KWIN, HEAD_DIM), BF16)

    def rows(residue, first, count):
        return pl.ds(residue + first * dil, count, stride=dil)

    def scores(it):
        residue = it // n_chunk
        q0 = (it % n_chunk) * DIL_QCHUNK
        k0 = jnp.clip(q0 - DIL_RADIUS, 0, length - DIL_KWIN)
        q = q_ref[rows(residue, q0, DIL_QCHUNK), :].astype(BF16)
        k = k_ref[rows(residue, k0, DIL_KWIN), :].astype(BF16)
        return _dot_nt(q, k) + mask_ref[(q0 - k0) // DIL_RADIUS], residue, q0, k0

    def finish(s, residue, q0, k0):
        v1 = jnp.concatenate([v_ref[rows(residue, k0, DIL_KWIN), :].astype(BF16), ones], axis=1)
        m = jnp.max(s, axis=-1, keepdims=True)
        acc = _dot(jnp.exp2(s - m).astype(BF16), v1)
        den = acc[:, HEAD_DIM:]
        o_ref[rows(residue, q0, DIL_QCHUNK), :] = acc[:, :HEAD_DIM] / den
        lse_ref[rows(residue, q0, DIL_QCHUNK), :] = m * LN2 + jnp.log(den[:, :1])

    def body(i, carry):
        pending = [scores(i * DIL_GROUP + j) for j in range(DIL_GROUP)]
        for item in pending:
            finish(*item)
        return carry

    assert (dil * n_chunk) % DIL_GROUP == 0
    lax.fori_loop(0, dil * n_chunk // DIL_GROUP, body, 0)


def _dil_group(qk, v, group, dil, batch, seq):
    length = seq // dil
    assert length % DIL_QCHUNK == 0 and length >= DIL_KWIN
    hg = DIL_HEADS_PER_GROUP
    blk = (seq, HEAD_DIM)
    mask = _dil_window_mask()
    return pl.pallas_call(
        functools.partial(_dil_kernel, seq=seq, dil=dil),
        grid=(batch, hg),
        in_specs=[pl.BlockSpec(blk, lambda b, h: (b, group * hg + h)),
                  pl.BlockSpec(blk, lambda b, h: (b, DIL_HEADS + group * hg + h)),
                  pl.BlockSpec(blk, lambda b, h: (b, group * hg + h)),
                  pl.BlockSpec(mask.shape, lambda b, h: (0, 0, 0))],
        out_specs=[pl.BlockSpec(blk, lambda b, h: (b, h)),
                   pl.BlockSpec((None, seq, 1), lambda b, h: (b * hg + h, 0, 0))],
        out_shape=[jax.ShapeDtypeStruct((batch * seq, DIL_OUT), F32),
                   jax.ShapeDtypeStruct((batch * hg, seq, 1), F32)],
        compiler_params=_params("parallel", "parallel"),
        name="dilated_attention",
    )(qk, qk, v, mask)


def _dil_mix_kernel(o0_ref, o1_ref, o2_ref, l0_ref, l1_ref, l2_ref, o_ref):
    l0, l1, l2 = l0_ref[...], l1_ref[...], l2_ref[...]
    mx = jnp.maximum(jnp.maximum(l0, l1), l2)
    e0, e1, e2 = jnp.exp(l0 - mx), jnp.exp(l1 - mx), jnp.exp(l2 - mx)
    z = e0 + e1 + e2
    w0, w1, w2 = e0 / z, e1 / z, e2 / z
    for h in range(DIL_HEADS_PER_GROUP):
        sl = slice(h * HEAD_DIM, (h + 1) * HEAD_DIM)
        hs = slice(h, h + 1)
        o_ref[:, sl] = (w0[:, hs] * o0_ref[:, sl] + w1[:, hs] * o1_ref[:, sl]
                        + w2[:, hs] * o2_ref[:, sl]).astype(o_ref.dtype)


def _dil_mix(outs, lses, tm=512):
    t = outs[0].shape[0]
    ospec = pl.BlockSpec((tm, DIL_OUT), lambda i: (i, 0))
    lspec = pl.BlockSpec((tm, DIL_HEADS_PER_GROUP), lambda i: (i, 0))
    return pl.pallas_call(
        _dil_mix_kernel,
        grid=(t // tm,),
        in_specs=[ospec] * 3 + [lspec] * 3,
        out_specs=ospec,
        out_shape=jax.ShapeDtypeStruct((t, DIL_OUT), BF16),
        compiler_params=_params("parallel"),
        name="dilated_mix",
    )(*outs, *lses)


def _dilated_attention(qk, v, batch, seq):
    hg = DIL_HEADS_PER_GROUP
    outs, lses = [], []
    for g, (window, dil) in enumerate(DIL_GROUPS):
        assert window // (2 * dil) == DIL_RADIUS and seq % dil == 0
        o, lse = _dil_group(qk, v, g, dil, batch, seq)
        outs.append(o)
        lses.append(lse.reshape(batch, hg, seq).transpose(0, 2, 1).reshape(batch * seq, hg))
    return _dil_mix(outs, lses)


def _mla_q_kernel(a_ref, w_ref, g_ref, cos_ref, sin_ref, o_ref):
    a = a_ref[...]
    gain = g_ref[...]
    cos_t, sin_t = cos_ref[...], sin_ref[...]
    for h in range(MLA_HEADS):
        acc = _dot(a, w_ref[h])
        ms = jnp.sum(acc * acc, axis=-1, keepdims=True) * (1.0 / MLA_QK)
        y = acc * lax.rsqrt(ms + RMS_EPS) * gain
        o_ref[h, :, :MLA_NOPE] = y[:, :MLA_NOPE].astype(o_ref.dtype)
        o_ref[h, :, MLA_NOPE:] = _rotate_pairs(y[:, MLA_NOPE:], cos_t, sin_t,
                                               MLA_ROPE // 2).astype(o_ref.dtype)


def _mla_q_proj(cq_n, w_uq_h, gain, cos_t, sin_t, seq, tm=512):
    t, rank = cq_n.shape
    per = seq // tm
    return pl.pallas_call(
        _mla_q_kernel,
        grid=(t // tm,),
        in_specs=[pl.BlockSpec((tm, rank), lambda i: (i, 0)),
                  pl.BlockSpec((MLA_HEADS, rank, MLA_QK_PAD), lambda i: (0, 0, 0)),
                  pl.BlockSpec((1, MLA_QK_PAD), lambda i: (0, 0)),
                  pl.BlockSpec((tm, LANES), lambda i: (i % per, 0)),
                  pl.BlockSpec((tm, LANES), lambda i: (i % per, 0))],
        out_specs=pl.BlockSpec((MLA_HEADS, tm, MLA_QK_PAD), lambda i: (0, i, 0)),
        out_shape=jax.ShapeDtypeStruct((MLA_HEADS, t, MLA_QK_PAD), BF16),
        compiler_params=_params("parallel"),
        name="mla_q_proj",
    )(cq_n, w_uq_h, gain, cos_t, sin_t)


def _mla_kv_kernel(a_ref, w_ref, kr_ref, g0_ref, g1_ref, cos_ref, sin_ref, k_ref, v_ref):
    a = a_ref[...]
    kr = kr_ref[...]
    kr_ss = jnp.sum(kr * kr, axis=-1, keepdims=True)
    g0, g1 = g0_ref[...], g1_ref[...]
    cos_t, sin_t = cos_ref[...], sin_ref[...]
    ones = jnp.ones((a.shape[0], MLA_V), v_ref.dtype)
    for h in range(MLA_HEADS):
        acc = _dot(a, w_ref[h])
        kn = acc[:, :MLA_NOPE]
        ms = (jnp.sum(kn * kn, axis=-1, keepdims=True) + kr_ss) * (1.0 / MLA_QK)
        inv = lax.rsqrt(ms + RMS_EPS)
        k_ref[h, :, :MLA_NOPE] = (kn * inv * g0).astype(k_ref.dtype)
        k_ref[h, :, MLA_NOPE:] = _rotate_pairs(kr * inv * g1, cos_t, sin_t,
                                               MLA_ROPE // 2).astype(k_ref.dtype)
        v_ref[h, :, :MLA_V] = acc[:, MLA_NOPE:].astype(v_ref.dtype)
        v_ref[h, :, MLA_V:] = ones


def _mla_kv_proj(ckv_n, w_ukv_h, k_r, g0, g1, cos_t, sin_t, seq, tm=512):
    t, rank = ckv_n.shape
    per = seq // tm
    return pl.pallas_call(
        _mla_kv_kernel,
        grid=(t // tm,),
        in_specs=[pl.BlockSpec((tm, rank), lambda i: (i, 0)),
                  pl.BlockSpec((MLA_HEADS, rank, MLA_NOPE + MLA_V), lambda i: (0, 0, 0)),
                  pl.BlockSpec((tm, LANES), lambda i: (i, 0)),
                  pl.BlockSpec((1, LANES), lambda i: (0, 0)),
                  pl.BlockSpec((1, LANES), lambda i: (0, 0)),
                  pl.BlockSpec((tm, LANES), lambda i: (i % per, 0)),
                  pl.BlockSpec((tm, LANES), lambda i: (i % per, 0))],
        out_specs=[pl.BlockSpec((MLA_HEADS, tm, MLA_QK_PAD), lambda i: (0, i, 0)),
                   pl.BlockSpec((MLA_HEADS, tm, 2 * MLA_V), lambda i: (0, i, 0))],
        out_shape=[jax.ShapeDtypeStruct((MLA_HEADS, t, MLA_QK_PAD), BF16),
                   jax.ShapeDtypeStruct((MLA_HEADS, t, 2 * MLA_V), BF16)],
        compiler_params=_params("parallel"),
        name="mla_kv_proj",
    )(ckv_n, w_ukv_h, k_r, g0, g1, cos_t, sin_t)


MLA_KV_CHUNKS = 8
MLA_QK_AHEAD = 1


def _mla_attn_kernel(q_ref, k_ref, v_ref, o_ref):
    q = q_ref[...]
    tk = k_ref.shape[0] // MLA_KV_CHUNKS
    m = acc = None
    ahead = [_dot_nt(q, k_ref[i * tk:(i + 1) * tk, :]) for i in range(MLA_QK_AHEAD)]
    for c in range(MLA_KV_CHUNKS):
        s = ahead.pop(0)
        nxt = c + MLA_QK_AHEAD
        if nxt < MLA_KV_CHUNKS:
            ahead.append(_dot_nt(q, k_ref[nxt * tk:(nxt + 1) * tk, :]))
        m_c = jnp.max(s, axis=-1, keepdims=True)
        if c == 0:
            m = m_c
            acc = _dot(jnp.exp2(s - m).astype(BF16), v_ref[:tk, :])
        else:
            m_new = jnp.maximum(m, m_c)
            acc = jnp.exp2(m - m_new) * acc + _dot(jnp.exp2(s - m_new).astype(BF16),
                                                   v_ref[c * tk:(c + 1) * tk, :])
            m = m_new
    o_ref[...] = (acc[:, :MLA_V] / acc[:, MLA_V:]).astype(o_ref.dtype)


def _mla_attention(q, k, v, batch, seq, tq=512):
    t = q.shape[1]
    nq = seq // tq
    return pl.pallas_call(
        _mla_attn_kernel,
        grid=(batch, MLA_HEADS, nq),
        in_specs=[pl.BlockSpec((None, tq, MLA_QK_PAD), lambda b, h, i: (h, b * nq + i, 0)),
                  pl.BlockSpec((None, seq, MLA_QK_PAD), lambda b, h, i: (h, b, 0)),
                  pl.BlockSpec((None, seq, 2 * MLA_V), lambda b, h, i: (h, b, 0))],
        out_specs=pl.BlockSpec((tq, MLA_V), lambda b, h, i: (b * nq + i, h)),
        out_shape=jax.ShapeDtypeStruct((t, MLA_HEADS * MLA_V), BF16),
        compiler_params=_params("parallel", "parallel", "arbitrary"),
        name="mla_attention",
    )(q, k, v)


def _merge_kernel(ona_ref, odl_ref, omla_ref, wna_ref, wdl_ref, wmla_ref, g0_ref, g1_ref, g2_ref, o_ref):
    ona, odl, omla = ona_ref[...], odl_ref[...], omla_ref[...]
    for c in range(o_ref.shape[1] // GU_SPLIT):
        cols = slice(c * GU_SPLIT, (c + 1) * GU_SPLIT)
        acc = g0_ref[:, cols].astype(F32) * _dot(ona, wna_ref[:, cols])
        acc = acc + g1_ref[:, cols].astype(F32) * _dot(odl, wdl_ref[:, cols])
        acc = acc + g2_ref[:, cols].astype(F32) * _dot(omla, wmla_ref[:, cols])
        o_ref[:, cols] = acc.astype(o_ref.dtype)


def _merge(o_na, o_dl, o_mla, w_na, w_dl, w_mla, gates, tm=1024, tn=512):
    t = o_na.shape[0]
    d = w_na.shape[1]
    nj = d // tn

    def act(a):
        return pl.BlockSpec((tm, a.shape[1]), lambda i, j: (i, 0))

    def wgt(w):
        return pl.BlockSpec((w.shape[0], tn), lambda i, j: (0, j))

    def gate(idx):
        return pl.BlockSpec((tm, tn), lambda i, j: (i, idx * nj + j))

    return pl.pallas_call(
        _merge_kernel,
        grid=(t // tm, nj),
        in_specs=[act(o_na), act(o_dl), act(o_mla), wgt(w_na), wgt(w_dl), wgt(w_mla),
                  gate(0), gate(1), gate(2)],
        out_specs=pl.BlockSpec((tm, tn), lambda i, j: (i, j)),
        out_shape=jax.ShapeDtypeStruct((t, d), BF16),
        compiler_params=_params("parallel", "parallel"),
        name="branch_merge",
    )(o_na, o_dl, o_mla, w_na, w_dl, w_mla, gates, gates, gates)


GU_SPLIT = 256


def _swiglu_pieces(h, wg_ref, wu_ref, o_ref):
    for c in range(o_ref.shape[1] // GU_SPLIT):
        cols = slice(c * GU_SPLIT, (c + 1) * GU_SPLIT)
        g = _dot(h, wg_ref[:, cols].astype(BF16))
        u = _dot(h, wu_ref[:, cols].astype(BF16))
        o_ref[:, cols] = (g * jax.nn.sigmoid(g) * u).astype(o_ref.dtype)


def _gu_kernel(h_ref, wg_ref, wu_ref, o_ref):
    _swiglu_pieces(h_ref[...], wg_ref, wu_ref, o_ref)


def _swiglu_up(h, w_gu, tm=1024, tn=512):
    t, d = h.shape
    ff = w_gu.shape[1] // 2
    nj = ff // tn
    return pl.pallas_call(
        _gu_kernel,
        grid=(t // tm, nj),
        in_specs=[pl.BlockSpec((tm, d), lambda i, j: (i, 0)),
                  pl.BlockSpec((d, tn), lambda i, j: (0, j)),
                  pl.BlockSpec((d, tn), lambda i, j: (0, nj + j))],
        out_specs=pl.BlockSpec((tm, tn), lambda i, j: (i, j)),
        out_shape=jax.ShapeDtypeStruct((t, ff), BF16),
        compiler_params=_params("parallel", "parallel"),
        name="swiglu_up",
    )(h, w_gu, w_gu)


MOE_TM = 1024
TOP_K = 2


def _moe_routing(route, tokens):
    e_flat = jnp.concatenate([route[:, N_EXPERTS], route[:, N_EXPERTS + 1]]).astype(jnp.int32)
    onehot = (e_flat[:, None] == jnp.arange(N_EXPERTS, dtype=jnp.int32)[None, :]).astype(jnp.int32)
    csum = jnp.cumsum(onehot, axis=0)
    rank = jnp.sum((csum - onehot) * onehot, axis=1)
    padded = ((csum[-1] + MOE_TM - 1) // MOE_TM) * MOE_TM
    ends = jnp.cumsum(padded)
    slot = jnp.sum(onehot * (ends - padded)[None, :], axis=1) + rank
    n_tiles = (TOP_K * tokens) // MOE_TM + N_EXPERTS
    tile_start = jnp.arange(n_tiles, dtype=jnp.int32) * MOE_TM
    tile_expert = jnp.minimum(jnp.sum(tile_start[:, None] >= ends[None, :], axis=1), N_EXPERTS - 1)
    te = jnp.concatenate([tile_expert, ends[-1:] // MOE_TM]).astype(jnp.int32)
    return slot.astype(jnp.int32), te


def _dispatch_kernel(slot_ref, x_ref, g_ref, sc_ref, sh_ref, dst_in_ref, dst_ref, h_ref, sem, *, rows, tokens):
    del dst_in_ref
    base = pl.program_id(0) * rows
    h_ref[...] = _modulated_norm(x_ref[...], g_ref[...], sc_ref[...], sh_ref[...])

    def row_copy(r, choice):
        slot = slot_ref[choice * tokens + base + r]
        return pltpu.make_async_copy(h_ref.at[pl.ds(r, 1), :], dst_ref.at[pl.ds(slot, 1), :], sem)

    def start(r, carry):
        row_copy(r, 0).start()
        row_copy(r, 1).start()
        return carry

    def wait(r, carry):
        row_copy(r, 0).wait()
        row_copy(r, 1).wait()
        return carry

    lax.fori_loop(0, rows, start, 0, unroll=8)
    lax.fori_loop(0, rows, wait, 0, unroll=8)


def _moe_dispatch(x, g, sc, sh, slot, seq, n_rows, rows=256):
    t, d = x.shape
    per = seq // rows
    return pl.pallas_call(
        functools.partial(_dispatch_kernel, rows=rows, tokens=t),
        grid_spec=pltpu.PrefetchScalarGridSpec(
            num_scalar_prefetch=1,
            grid=(t // rows,),
            in_specs=[pl.BlockSpec((rows, d), lambda i, s: (i, 0)),
                      pl.BlockSpec((1, d), lambda i, s: (0, 0)),
                      pl.BlockSpec((None, 1, d), lambda i, s: (i // per, 0, 0)),
                      pl.BlockSpec((None, 1, d), lambda i, s: (i // per, 0, 0)),
                      pl.BlockSpec(memory_space=pl.ANY)],
            out_specs=pl.BlockSpec(memory_space=pl.ANY),
            scratch_shapes=[pltpu.VMEM((rows, d), F32), pltpu.SemaphoreType.DMA(())]),
        out_shape=jax.ShapeDtypeStruct((n_rows, d), F32),
        input_output_aliases={5: 0},
        compiler_params=_params("arbitrary"),
        name="moe_dispatch",
    )(slot, x, g.reshape(1, d), sc, sh, jnp.zeros((n_rows, d), F32))


def _moe_up_kernel(te_ref, xs_ref, wg_ref, wu_ref, o_ref, hb_ref, *, n_tiles):
    used = pl.program_id(0) < te_ref[n_tiles]

    @pl.when(used & (pl.program_id(1) == 0))
    def _():
        hb_ref[...] = xs_ref[...].astype(BF16)

    @pl.when(used)
    def _():
        _swiglu_pieces(hb_ref[...], wg_ref, wu_ref, o_ref)

    @pl.when(jnp.logical_not(used))
    def _():
        o_ref[...] = jnp.zeros(o_ref.shape, o_ref.dtype)


def _moe_up(xs, w_gu, te, tn=512):
    n_rows, d = xs.shape
    ff = w_gu.shape[2] // 2
    nj = ff // tn
    n_tiles = n_rows // MOE_TM

    def last_used(i, te_ref):
        return jnp.minimum(i, te_ref[n_tiles] - 1)

    return pl.pallas_call(
        functools.partial(_moe_up_kernel, n_tiles=n_tiles),
        grid_spec=pltpu.PrefetchScalarGridSpec(
            num_scalar_prefetch=1,
            grid=(n_tiles, nj),
            in_specs=[pl.BlockSpec((MOE_TM, d), lambda i, j, te_ref: (last_used(i, te_ref), 0)),
                      pl.BlockSpec((None, d, tn), lambda i, j, te_ref: (te_ref[i], 0, j)),
                      pl.BlockSpec((None, d, tn), lambda i, j, te_ref: (te_ref[i], 0, nj + j))],
            out_specs=pl.BlockSpec((MOE_TM, tn), lambda i, j, te_ref: (i, j)),
            scratch_shapes=[pltpu.VMEM((MOE_TM, d), BF16)]),
        out_shape=jax.ShapeDtypeStruct((n_rows, ff), BF16),
        compiler_params=_params("parallel", "arbitrary"),
        name="moe_up",
    )(te, xs, w_gu, w_gu)


def _moe_down_kernel(te_ref, a_ref, w_ref, o_ref, *, n_tiles):
    used = pl.program_id(0) < te_ref[n_tiles]

    @pl.when(used)
    def _():
        o_ref[...] = _dot(a_ref[...], w_ref[...])

    @pl.when(jnp.logical_not(used))
    def _():
        o_ref[...] = jnp.zeros(o_ref.shape, o_ref.dtype)


def _moe_down(act, w_down, te, tn=256):
    n_rows, ff = act.shape
    d = w_down.shape[2]
    n_tiles = n_rows // MOE_TM

    def last_used(i, te_ref):
        return jnp.minimum(i, te_ref[n_tiles] - 1)

    return pl.pallas_call(
        functools.partial(_moe_down_kernel, n_tiles=n_tiles),
        grid_spec=pltpu.PrefetchScalarGridSpec(
            num_scalar_prefetch=1,
            grid=(n_tiles, d // tn),
            in_specs=[pl.BlockSpec((MOE_TM, ff), lambda i, j, te_ref: (last_used(i, te_ref), 0)),
                      pl.BlockSpec((None, ff, tn), lambda i, j, te_ref: (te_ref[i], 0, j))],
            out_specs=pl.BlockSpec((MOE_TM, tn), lambda i, j, te_ref: (i, j))),
        out_shape=jax.ShapeDtypeStruct((n_rows, d), F32),
        compiler_params=_params("parallel", "arbitrary"),
        name="moe_down",
    )(te, act, w_down)


def _combine_kernel(slot_ref, x_ref, gate_ref, w_ref, ys_ref, o_ref, buf_ref, sem, *, rows, tokens):
    base = pl.program_id(0) * rows

    def row_copy(r, choice):
        slot = slot_ref[choice * tokens + base + r]
        return pltpu.make_async_copy(ys_ref.at[pl.ds(slot, 1), :], buf_ref.at[choice, pl.ds(r, 1), :], sem)

    def start(r, carry):
        row_copy(r, 0).start()
        row_copy(r, 1).start()
        return carry

    def wait(r, carry):
        row_copy(r, 0).wait()
        row_copy(r, 1).wait()
        return carry

    lax.fori_loop(0, rows, start, 0, unroll=8)
    lax.fori_loop(0, rows, wait, 0, unroll=8)
    w = w_ref[...]
    y = w[:, 0:1] * buf_ref[0] + w[:, 1:2] * buf_ref[1]
    o_ref[...] = x_ref[...] + gate_ref[...] * y


def _moe_combine(x, gate, w12, ys, slot, seq, rows=256):
    t, d = x.shape
    per = seq // rows
    return pl.pallas_call(
        functools.partial(_combine_kernel, rows=rows, tokens=t),
        grid_spec=pltpu.PrefetchScalarGridSpec(
            num_scalar_prefetch=1,
            grid=(t // rows,),
            in_specs=[pl.BlockSpec((rows, d), lambda i, s: (i, 0)),
                      pl.BlockSpec((None, 1, d), lambda i, s: (i // per, 0, 0)),
                      pl.BlockSpec((rows, TOP_K), lambda i, s: (i, 0)),
                      pl.BlockSpec(memory_space=pl.ANY)],
            out_specs=pl.BlockSpec((rows, d), lambda i, s: (i, 0)),
            scratch_shapes=[pltpu.VMEM((TOP_K, rows, d), F32), pltpu.SemaphoreType.DMA(())]),
        out_shape=jax.ShapeDtypeStruct((t, d), F32),
        compiler_params=_params("arbitrary"),
        name="moe_combine",
    )(slot, x, gate, w12, ys)


def _router_kernel(x_ref, g_ref, sc_ref, sh_ref, w_ref, o_ref):
    h = _modulated_norm(x_ref[...], g_ref[...], sc_ref[...], sh_ref[...])
    logits = jnp.dot(h, w_ref[...], preferred_element_type=F32, precision=lax.Precision.HIGHEST)
    lane = lax.broadcasted_iota(jnp.int32, logits.shape, 1).astype(F32)
    lg = jnp.where(lane < N_EXPERTS, logits, NEG_INF)
    m1 = jnp.max(lg, axis=-1, keepdims=True)
    i1 = jnp.min(jnp.where(lg == m1, lane, float(LANES)), axis=-1, keepdims=True)
    lg2 = jnp.where(lane == i1, NEG_INF, lg)
    m2 = jnp.max(lg2, axis=-1, keepdims=True)
    i2 = jnp.min(jnp.where(lg2 == m2, lane, float(LANES)), axis=-1, keepdims=True)
    e2 = jnp.exp(m2 - m1)
    z = 1.0 + e2
    out = jnp.where(lane == N_EXPERTS, i1, 0.0) + jnp.where(lane == N_EXPERTS + 1, i2, 0.0)
    out = out + jnp.where(lane == N_EXPERTS + 2, 1.0 / z, 0.0) + jnp.where(lane == N_EXPERTS + 3, e2 / z, 0.0)
    o_ref[...] = out


def _router(x, g, sc, sh, w_router_pad, seq, tm=512):
    t, d = x.shape
    per = seq // tm
    return pl.pallas_call(
        _router_kernel,
        grid=(t // tm,),
        in_specs=[pl.BlockSpec((tm, d), lambda i: (i, 0)),
                  pl.BlockSpec((1, d), lambda i: (0, 0)),
                  pl.BlockSpec((None, 1, d), lambda i: (i // per, 0, 0)),
                  pl.BlockSpec((None, 1, d), lambda i: (i // per, 0, 0)),
                  pl.BlockSpec((d, LANES), lambda i: (0, 0))],
        out_specs=pl.BlockSpec((tm, LANES), lambda i: (i, 0)),
        out_shape=jax.ShapeDtypeStruct((t, LANES), F32),
        compiler_params=_params("parallel"),
        name="router",
    )(x, g.reshape(1, d), sc, sh, w_router_pad)


MM_TM = 1024
MM_TN = 1024
MM_SPLIT = 256


def _col_tile(n):
    return MM_TN if n % MM_TN == 0 else MM_TN // 2


def _residual_matmul(a, w, x, gate, seq, name, tm=MM_TM, tn=None):
    per = seq // tm
    tn = _col_tile(w.shape[1]) if tn is None else tn
    return _matmul(a, w, tm=tm, tn=tn, out_dtype=F32, epilogue=_ep_residual, split=MM_SPLIT,
                   extras=[(x, (tm, tn), lambda i, j: (i, j)),
                           (gate, (None, 1, tn), lambda i, j: (i // per, 0, j))],
                   name=name)


def _token_mixer(h, x, gate, lw, batch, seq, tables):
    d = h.shape[1]
    w_in = lw["w_in"]
    tm = MM_TM
    per = seq // tm
    cos_d, sin_d, cos_m, sin_m = tables

    def cols(lo, hi):
        return w_in[:, lo:hi].astype(BF16)

    def tile_gain(gq, gk, heads):
        gq = gq * (LOG2E / math.sqrt(HEAD_DIM))
        return jnp.concatenate([jnp.tile(gq, heads), jnp.tile(gk, heads)]).reshape(1, -1).astype(F32)

    o = 0
    tn = _col_tile(2 * NA_W)
    na_qk = _matmul(h, cols(o, o + 2 * NA_W), tm=tm, tn=tn, out_dtype=BF16, epilogue=_ep_headnorm,
                    split=MM_SPLIT,
                    extras=[(tile_gain(lw["na_q_norm"], lw["na_k_norm"], NA_HEADS), (1, tn),
                             lambda i, j: (0, j))], name="na_qk_proj")
    o += 2 * NA_W
    na_v = _matmul(h, cols(o, o + NA_W), tm=tm, tn=_col_tile(NA_W), out_dtype=BF16, split=MM_SPLIT, name="na_v_proj")
    o += NA_W
    tn = _col_tile(2 * DIL_W)
    dl_qk = _matmul(h, cols(o, o + 2 * DIL_W), tm=tm, tn=tn, out_dtype=F32, epilogue=_ep_headnorm_rope,
                    split=MM_SPLIT,
                    extras=[(tile_gain(lw["dil_q_norm"], lw["dil_k_norm"], DIL_HEADS), (1, tn),
                             lambda i, j: (0, j)),
                            (cos_d, (tm, LANES), lambda i, j: (i % per, 0)),
                            (sin_d, (tm, LANES), lambda i, j: (i % per, 0)),
                            (_pair_swap_matrix(MM_SPLIT, ROT_DIM // 2), (MM_SPLIT, MM_SPLIT),
                             lambda i, j: (0, 0))], name="dil_qk_proj")
    o += 2 * DIL_W
    dl_v = _matmul(h, cols(o, o + DIL_W), tm=tm, tn=_col_tile(DIL_W), out_dtype=F32, split=MM_SPLIT, name="dil_v_proj")
    o += DIL_W
    cq_n = _matmul(h, cols(o, o + MLA_Q_RANK), tm=tm, tn=MLA_Q_RANK, out_dtype=BF16, epilogue=_ep_rownorm,
                   extras=[(lw["mla_q_a_norm"].reshape(1, -1), (1, MLA_Q_RANK), lambda i, j: (0, 0))],
                   name="mla_cq_proj")
    o += MLA_Q_RANK
    ckv_n = _matmul(h, cols(o, o + MLA_KV_RANK), tm=tm, tn=MLA_KV_RANK, out_dtype=BF16, epilogue=_ep_rownorm,
                    extras=[(lw["mla_kv_a_norm"].reshape(1, -1), (1, MLA_KV_RANK), lambda i, j: (0, 0))],
                    name="mla_ckv_proj")
    o += MLA_KV_RANK
    w_kr = jnp.pad(w_in[:, o:o + MLA_ROPE], ((0, 0), (0, LANES - MLA_ROPE))).astype(BF16)
    k_r = _matmul(h, w_kr, tm=tm, tn=LANES, out_dtype=F32, name="mla_kr_proj")
    o += MLA_ROPE
    gates = _matmul(h, cols(o, o + 3 * d), tm=tm, tn=_col_tile(3 * d), out_dtype=BF16, epilogue=_ep_sigmoid,
                    split=MM_SPLIT, name="gate_proj")

    o_na = _na_attention(na_qk, na_v, lw["na_rpb"], batch, seq)
    o_dl = _dilated_attention(dl_qk, dl_v, batch, seq)
    w_uq = lw["mla_w_uq"].reshape(MLA_Q_RANK, MLA_HEADS, MLA_QK).transpose(1, 0, 2)
    w_uq = jnp.pad(w_uq, ((0, 0), (0, 0), (0, MLA_QK_PAD - MLA_QK))).astype(BF16)
    w_ukv = lw["mla_w_ukv"].reshape(MLA_KV_RANK, MLA_HEADS, MLA_NOPE + MLA_V).transpose(1, 0, 2).astype(BF16)
    gq = jnp.pad(lw["mla_q_norm"] * (LOG2E / math.sqrt(MLA_QK)), (0, MLA_QK_PAD - MLA_QK)).reshape(1, MLA_QK_PAD)
    gk = lw["mla_k_norm"]
    gk0 = gk[:MLA_NOPE].reshape(1, LANES)
    gk1 = jnp.pad(gk[MLA_NOPE:], (0, LANES - MLA_ROPE)).reshape(1, LANES)
    q_m = _mla_q_proj(cq_n, w_uq, gq, cos_m, sin_m, seq)
    k_m, v_m = _mla_kv_proj(ckv_n, w_ukv, k_r, gk0, gk1, cos_m, sin_m, seq)
    o_mla = _mla_attention(q_m, k_m, v_m, batch, seq)

    merged = _merge(o_na, o_dl, o_mla, lw["w_branch_na"].astype(BF16), lw["w_branch_dil"].astype(BF16),
                    lw["w_branch_mla"].astype(BF16), gates)
    return _residual_matmul(merged, lw["w_out"].astype(BF16), x, gate, seq, "out_proj")


def kernel(x, c, w_ada, b_ada, norm_mix, norm_ffn, w_in, na_q_norm, na_k_norm, na_rpb, dil_q_norm, dil_k_norm, mla_q_a_norm, mla_w_uq, mla_kv_a_norm, mla_w_ukv, mla_q_norm, mla_k_norm, w_branch_na, w_branch_dil, w_branch_mla, w_out, ffn_w_gu, ffn_w_down, moe_router, moe_w_gu, moe_w_down):
    batch, seq, d = x.shape
    depth = w_ada.shape[0]
    assert seq % MM_TM == 0 and seq % GRID_W == 0
    t = batch * seq
    xf = x.reshape(t, d)

    c_pad = jnp.pad(c, ((0, (-batch) % 8), (0, 0)))
    mod = _adaln(c_pad, w_ada, b_ada)[:, :batch].reshape(depth, batch, 6, 1, d)
    tables = _rope_tables(seq, ROT_DIM, 1.0) + _rope_tables(seq, MLA_ROPE, 1.0)

    for l in range(depth):
        sh1, sc1, g1, sh2, sc2, g2 = [mod[l, :, i] for i in range(6)]
        lw = dict(w_in=w_in[l], na_q_norm=na_q_norm[l], na_k_norm=na_k_norm[l], na_rpb=na_rpb[l],
                  dil_q_norm=dil_q_norm[l], dil_k_norm=dil_k_norm[l], mla_q_a_norm=mla_q_a_norm[l],
                  mla_w_uq=mla_w_uq[l], mla_kv_a_norm=mla_kv_a_norm[l], mla_w_ukv=mla_w_ukv[l],
                  mla_q_norm=mla_q_norm[l], mla_k_norm=mla_k_norm[l], w_branch_na=w_branch_na[l],
                  w_branch_dil=w_branch_dil[l], w_branch_mla=w_branch_mla[l], w_out=w_out[l])
        h = _normmod(xf, norm_mix[l], sc1, sh1, seq)
        xf = _token_mixer(h, xf, g1, lw, batch, seq, tables)

        if l % 2 == 0:
            h = _normmod(xf, norm_ffn[l], sc2, sh2, seq)
            act = _swiglu_up(h, ffn_w_gu[l // 2].astype(BF16))
            xf = _residual_matmul(act, ffn_w_down[l // 2].astype(BF16), xf, g2, seq, "down_proj", tm=1024, tn=256)
        else:
            w_r = jnp.pad(moe_router[l // 2], ((0, 0), (0, LANES - N_EXPERTS)))
            route = _router(xf, norm_ffn[l], sc2, sh2, w_r, seq)
            slot, te = _moe_routing(route, t)
            n_rows = TOP_K * t + N_EXPERTS * MOE_TM
            xs = _moe_dispatch(xf, norm_ffn[l], sc2, sh2, slot, seq, n_rows)
            act = _moe_up(xs, moe_w_gu[l // 2], te)
            ys = _moe_down(act, moe_w_down[l // 2].astype(BF16), te)
            xf = _moe_combine(xf, g2, route[:, N_EXPERTS + 2:N_EXPERTS + 4], ys, slot, seq)
    return xf.reshape(batch, seq, d)
```

```python
import functools
import math

import numpy as np
import jax
import jax.numpy as jnp
from jax import lax
from jax.experimental import pallas as pl
from jax.experimental.pallas import tpu as pltpu

GRID_W = 64
HEAD_DIM = 128
ROPE_THETA = 500000.0
ROT_DIM = HEAD_DIM // 4
RMS_EPS = 1e-6
NEG_INF = -1e30
NA_HEADS = 8
NA_KH = 8
NA_KW = 16
DIL_GROUPS = ((128, 1), (512, 4), (2048, 16))
DIL_HEADS_PER_GROUP = 4
DIL_HEADS = DIL_HEADS_PER_GROUP * len(DIL_GROUPS)
DIL_RADIUS = 64
MLA_HEADS = 8
MLA_NOPE = 128
MLA_ROPE = 64
MLA_V = 128
MLA_Q_RANK = 768
MLA_KV_RANK = 512
MLA_QK = MLA_NOPE + MLA_ROPE
MLA_QK_PAD = 256
N_EXPERTS = 8
LANES = 128

NA_W = NA_HEADS * HEAD_DIM
DIL_W = DIL_HEADS * HEAD_DIM
DIL_OUT = DIL_HEADS_PER_GROUP * HEAD_DIM

VMEM_LIMIT_BYTES = 48 * 1024 * 1024

F32 = jnp.float32
BF16 = jnp.bfloat16
LOG2E = math.log2(math.e)
LN2 = math.log(2.0)


def _params(*sem):
    return pltpu.CompilerParams(dimension_semantics=sem, vmem_limit_bytes=VMEM_LIMIT_BYTES)


def _dot(a, b):
    return jnp.dot(a, b, preferred_element_type=F32)


def _dot_nt(a, b):
    return lax.dot_general(a, b, (((1,), (1,)), ((), ())), preferred_element_type=F32)


def _adaln_kernel(c_ref, w_ref, b_ref, o_ref):
    o_ref[...] = _dot(c_ref[...], w_ref[...]) + b_ref[...]


def _adaln(c_pad, w_ada, b_ada):
    depth, d, n = w_ada.shape
    tn = 1024
    return pl.pallas_call(
        _adaln_kernel,
        grid=(depth, n // tn),
        in_specs=[pl.BlockSpec(c_pad.shape, lambda l, j: (0, 0)),
                  pl.BlockSpec((None, d, tn), lambda l, j: (l, 0, j)),
                  pl.BlockSpec((None, 1, tn), lambda l, j: (l, 0, j))],
        out_specs=pl.BlockSpec((None, c_pad.shape[0], tn), lambda l, j: (l, 0, j)),
        out_shape=jax.ShapeDtypeStruct((depth, c_pad.shape[0], n), F32),
        compiler_params=_params("parallel", "parallel"),
        name="adaln",
    )(c_pad, w_ada, b_ada.reshape(depth, 1, n))


def _modulated_norm(x, g, sc, sh):
    ms = jnp.mean(x * x, axis=-1, keepdims=True)
    return (x * lax.rsqrt(ms + RMS_EPS) * g) * (1.0 + sc) + sh


def _normmod_kernel(x_ref, g_ref, sc_ref, sh_ref, o_ref):
    o_ref[...] = _modulated_norm(x_ref[...], g_ref[...], sc_ref[...], sh_ref[...]).astype(o_ref.dtype)


def _normmod(x, g, sc, sh, seq, tm=512):
    t, d = x.shape
    per = seq // tm
    return pl.pallas_call(
        _normmod_kernel,
        grid=(t // tm,),
        in_specs=[pl.BlockSpec((tm, d), lambda i: (i, 0)),
                  pl.BlockSpec((1, d), lambda i: (0, 0)),
                  pl.BlockSpec((None, 1, d), lambda i: (i // per, 0, 0)),
                  pl.BlockSpec((None, 1, d), lambda i: (i // per, 0, 0))],
        out_specs=pl.BlockSpec((tm, d), lambda i: (i, 0)),
        out_shape=jax.ShapeDtypeStruct((t, d), BF16),
        compiler_params=_params("parallel"),
        name="normmod",
    )(x, g.reshape(1, d), sc, sh)


def _mm_kernel(*refs, n_extra, epilogue, split):
    a_ref, b_ref = refs[0], refs[1]
    extras = refs[2:2 + n_extra]
    o_ref = refs[2 + n_extra]
    a = a_ref[...]
    n_pieces = b_ref.shape[1] // split
    acc = _dot(a, b_ref[:, :split])
    for c in range(n_pieces):
        nxt = _dot(a, b_ref[:, (c + 1) * split:(c + 2) * split]) if c + 1 < n_pieces else None
        epilogue(acc, extras, o_ref, slice(c * split, (c + 1) * split))
        acc = nxt


def _ep_store(acc, extras, o_ref, cols):
    o_ref[:, cols] = acc.astype(o_ref.dtype)


def _matmul(a, b, *, tm, tn, out_dtype, epilogue=_ep_store, extras=(), split=None, name):
    m, kdim = a.shape
    n = b.shape[1]
    split = tn if split is None else split
    assert m % tm == 0 and n % tn == 0 and tn % split == 0
    in_specs = [pl.BlockSpec((tm, kdim), lambda i, j: (i, 0)),
                pl.BlockSpec((kdim, tn), lambda i, j: (0, j))]
    in_specs += [pl.BlockSpec(bs, im) for (_, bs, im) in extras]
    return pl.pallas_call(
        functools.partial(_mm_kernel, n_extra=len(extras), epilogue=epilogue, split=split),
        grid=(m // tm, n // tn),
        in_specs=in_specs,
        out_specs=pl.BlockSpec((tm, tn), lambda i, j: (i, j)),
        out_shape=jax.ShapeDtypeStruct((m, n), out_dtype),
        compiler_params=_params("parallel", "parallel"),
        name=name,
    )(a, b, *[e[0] for e in extras])


def _head_rms(blk, gain):
    ms = jnp.mean(blk * blk, axis=-1, keepdims=True)
    return blk * lax.rsqrt(ms + RMS_EPS) * gain


def _head_slices(cols):
    return [slice(c, c + HEAD_DIM) for c in range(cols.start, cols.stop, HEAD_DIM)]


def _ep_headnorm(acc, extras, o_ref, cols):
    for i, sl in enumerate(_head_slices(cols)):
        blk = acc[:, i * HEAD_DIM:(i + 1) * HEAD_DIM]
        o_ref[:, sl] = _head_rms(blk, extras[0][:, sl]).astype(o_ref.dtype)


def _rotate_pairs(y, cos_t, sin_t, half):
    lane = lax.broadcasted_iota(jnp.int32, y.shape, 1)
    swapped = jnp.where(lane < half, pltpu.roll(y, LANES - half, 1), pltpu.roll(y, half, 1))
    return y * cos_t + swapped * sin_t


def _pair_swap_matrix(width, half):
    p = np.zeros((width, width), np.float32)
    for base in range(0, width, HEAD_DIM):
        for i in range(half):
            p[base + i + half, base + i] = 1.0
            p[base + i, base + i + half] = 1.0
    return jnp.asarray(p, BF16)


def _ep_headnorm_rope(acc, extras, o_ref, cols):
    cos_t = extras[1][...]
    sin_t = extras[2][...]
    heads = _head_slices(cols)
    y = jnp.concatenate([_head_rms(acc[:, i * HEAD_DIM:(i + 1) * HEAD_DIM], extras[0][:, sl])
                         for i, sl in enumerate(heads)], axis=1)
    swapped = _dot(y.astype(BF16), extras[3][...])
    for i, sl in enumerate(heads):
        loc = slice(i * HEAD_DIM, (i + 1) * HEAD_DIM)
        o_ref[:, sl] = (y[:, loc] * cos_t + swapped[:, loc] * sin_t).astype(o_ref.dtype)


def _ep_rownorm(acc, extras, o_ref, cols):
    o_ref[:, cols] = _head_rms(acc, extras[0][:, cols]).astype(o_ref.dtype)


def _ep_sigmoid(acc, extras, o_ref, cols):
    o_ref[:, cols] = jax.nn.sigmoid(acc).astype(o_ref.dtype)


def _ep_residual(acc, extras, o_ref, cols):
    o_ref[:, cols] = extras[0][:, cols] + extras[1][:, cols] * acc


def _rope_tables(seq, rot, fill):
    half = rot // 2
    inv = ROPE_THETA ** (-jnp.arange(half, dtype=F32) * (2.0 / rot))
    ang = jnp.arange(seq, dtype=jnp.int32).astype(F32)[:, None] * inv[None, :]
    cos, sin = jnp.cos(ang), jnp.sin(ang)
    pad = LANES - rot
    cos_t = jnp.concatenate([cos, cos, jnp.full((seq, pad), fill, F32)], axis=1)
    sin_t = jnp.concatenate([-sin, sin, jnp.zeros((seq, pad), F32)], axis=1)
    return cos_t, sin_t


NA_RB = 4
NA_KROWS = NA_KH + NA_RB
NA_GROUP = 2


def _na_geometry(rows):
    kh = min(NA_KH, rows)
    assert rows % (NA_RB * NA_GROUP) == 0 and rows >= NA_KROWS
    patterns, ids, kstarts = [], [], []
    for r in range(0, rows, NA_RB):
        ks = int(np.clip(r - NA_KH // 2, 0, rows - NA_KROWS))
        pat = tuple((int(np.clip(r + a - NA_KH // 2, 0, rows - kh)) - ks, ks - (r + a)) for a in range(NA_RB))
        if pat not in patterns:
            patterns.append(pat)
        ids.append(patterns.index(pat))
        kstarts.append(ks)
    return patterns, np.array(ids + kstarts, np.int32)


def _na_bias_table(rpb, rows, patterns):
    h = rpb.shape[0]
    kh = min(NA_KH, rows)
    qc = np.arange(GRID_W)
    kc = np.arange(GRID_W)
    cstart = np.clip(qc - NA_KW // 2, 0, GRID_W - NA_KW)
    ok = (kc[None, :] >= cstart[:, None]) & (kc[None, :] < cstart[:, None] + NA_KW)
    dc = np.clip(kc[None, :] - qc[:, None] + NA_KW - 1, 0, 2 * NA_KW - 2)
    by_col = rpb.astype(F32)[:, :, dc] * LOG2E + jnp.where(ok, 0.0, NEG_INF).astype(F32)
    masked_row = 2 * NA_KH - 1
    by_col = jnp.concatenate([by_col, jnp.full((h, 1, GRID_W, GRID_W), NEG_INF, F32)], axis=1)
    kr = np.arange(NA_KROWS)
    dr = np.zeros((len(patterns), NA_RB, NA_KROWS), np.int32)
    for p, pat in enumerate(patterns):
        for a, (start, offset) in enumerate(pat):
            row_ok = (kr >= start) & (kr < start + kh)
            dr[p, a] = np.where(row_ok, kr + offset + NA_KH - 1, masked_row)
    tab = jnp.take(by_col, jnp.asarray(dr.reshape(-1)), axis=1)
    tab = tab.reshape(h, len(patterns), NA_RB, NA_KROWS, GRID_W, GRID_W)
    tab = tab.transpose(0, 1, 2, 4, 3, 5)
    return tab.reshape(h, len(patterns), NA_RB * GRID_W, NA_KROWS * GRID_W)


def _na_kernel(geo_ref, q_ref, k_ref, v_ref, b_ref, o_ref, *, n_blocks):
    qn = NA_RB * GRID_W
    kn = NA_KROWS * GRID_W
    ones = jnp.ones((kn, HEAD_DIM), BF16)

    def scores(blk):
        q0 = pl.multiple_of(blk * qn, qn)
        k0 = pl.multiple_of(geo_ref[n_blocks + blk] * GRID_W, GRID_W)
        s = _dot_nt(q_ref[pl.ds(q0, qn), :], k_ref[pl.ds(k0, kn), :]) + b_ref[geo_ref[blk]]
        return s, q0, k0

    def finish(s, q0, k0):
        v1 = jnp.concatenate([v_ref[pl.ds(k0, kn), :], ones], axis=1)
        m = jnp.max(s, axis=-1, keepdims=True)
        acc = _dot(jnp.exp2(s - m).astype(BF16), v1)
        o_ref[pl.ds(q0, qn), :] = (acc[:, :HEAD_DIM] / acc[:, HEAD_DIM:]).astype(o_ref.dtype)

    def body(i, carry):
        pending = [scores(i * NA_GROUP + j) for j in range(NA_GROUP)]
        for item in pending:
            finish(*item)
        return carry

    lax.fori_loop(0, n_blocks // NA_GROUP, body, 0)


def _na_attention(qk, v, rpb, batch, seq):
    rows = seq // GRID_W
    patterns, geo = _na_geometry(rows)
    bias = _na_bias_table(rpb, rows, patterns)
    t = qk.shape[0]
    blk = (seq, HEAD_DIM)
    return pl.pallas_call(
        functools.partial(_na_kernel, n_blocks=rows // NA_RB),
        grid_spec=pltpu.PrefetchScalarGridSpec(
            num_scalar_prefetch=1,
            grid=(batch, NA_HEADS),
            in_specs=[pl.BlockSpec(blk, lambda b, h, geo_ref: (b, h)),
                      pl.BlockSpec(blk, lambda b, h, geo_ref: (b, NA_HEADS + h)),
                      pl.BlockSpec(blk, lambda b, h, geo_ref: (b, h)),
                      pl.BlockSpec((None,) + bias.shape[1:], lambda b, h, geo_ref: (h, 0, 0, 0))],
            out_specs=pl.BlockSpec(blk, lambda b, h, geo_ref: (b, h))),
        out_shape=jax.ShapeDtypeStruct((t, NA_W), BF16),
        compiler_params=_params("parallel", "parallel"),
        name="na_attention",
    )(jnp.asarray(geo), qk, qk, v, bias)


DIL_QCHUNK = 2 * DIL_RADIUS
DIL_KWIN = 4 * DIL_RADIUS
DIL_GROUP = 4


def _dil_window_mask():
    r = np.arange(DIL_QCHUNK)[None, :, None]
    c = np.arange(DIL_KWIN)[None, None, :]
    off = (np.arange(3) * DIL_RADIUS)[:, None, None]
    return jnp.asarray(np.where(np.abs(c - r - off) <= DIL_RADIUS, 0.0, NEG_INF), F32)


def _dil_kernel(q_ref, k_ref, v_ref, mask_ref, o_ref, lse_ref, *, seq, dil):
    length = seq // dil
    n_chunk = length // DIL_QCHUNK
    ones = jnp.ones((DIL_KWIN, HEAD_DIM), BF16)

    def rows(residue, first, count):
        return pl.ds(residue + first * dil, count, stride=dil)

    def scores(it):
        residue = it // n_chunk
        q0 = (it % n_chunk) * DIL_QCHUNK
        k0 = jnp.clip(q0 - DIL_RADIUS, 0, length - DIL_KWIN)
        q = q_ref[rows(residue, q0, DIL_QCHUNK), :].astype(BF16)
        k = k_ref[rows(residue, k0, DIL_KWIN), :].astype(BF16)
        return _dot_nt(q, k) + mask_ref[(q0 - k0) // DIL_RADIUS], residue, q0, k0

    def finish(s, residue, q0, k0):
        v1 = jnp.concatenate([v_ref[rows(residue, k0, DIL_KWIN), :].astype(BF16), ones], axis=1)
        m = jnp.max(s, axis=-1, keepdims=True)
        acc = _dot(jnp.exp2(s - m).astype(BF16), v1)
        den = acc[:, HEAD_DIM:]
        o_ref[rows(residue, q0, DIL_QCHUNK), :] = acc[:, :HEAD_DIM] / den
        lse_ref[rows(residue, q0, DIL_QCHUNK), :] = m * LN2 + jnp.log(den[:, :1])

    def body(i, carry):
        pending = [scores(i * DIL_GROUP + j) for j in range(DIL_GROUP)]
        for item in pending:
            finish(*item)
        return carry

    assert (dil * n_chunk) % DIL_GROUP == 0
    lax.fori_loop(0, dil * n_chunk // DIL_GROUP, body, 0)


def _dil_group(qk, v, group, dil, batch, seq):
    length = seq // dil
    assert length % DIL_QCHUNK == 0 and length >= DIL_KWIN
    hg = DIL_HEADS_PER_GROUP
    blk = (seq, HEAD_DIM)
    mask = _dil_window_mask()
    return pl.pallas_call(
        functools.partial(_dil_kernel, seq=seq, dil=dil),
        grid=(batch, hg),
        in_specs=[pl.BlockSpec(blk, lambda b, h: (b, group * hg + h)),
                  pl.BlockSpec(blk, lambda b, h: (b, DIL_HEADS + group * hg + h)),
                  pl.BlockSpec(blk, lambda b, h: (b, group * hg + h)),
                  pl.BlockSpec(mask.shape, lambda b, h: (0, 0, 0))],
        out_specs=[pl.BlockSpec(blk, lambda b, h: (b, h)),
                   pl.BlockSpec((None, seq, 1), lambda b, h: (b * hg + h, 0, 0))],
        out_shape=[jax.ShapeDtypeStruct((batch * seq, DIL_OUT), F32),
                   jax.ShapeDtypeStruct((batch * hg, seq, 1), F32)],
        compiler_params=_params("parallel", "parallel"),
        name="dilated_attention",
    )(qk, qk, v, mask)


def _dil_mix_kernel(o0_ref, o1_ref, o2_ref, l0_ref, l1_ref, l2_ref, o_ref):
    l0, l1, l2 = l0_ref[...], l1_ref[...], l2_ref[...]
    mx = jnp.maximum(jnp.maximum(l0, l1), l2)
    e0, e1, e2 = jnp.exp(l0 - mx), jnp.exp(l1 - mx), jnp.exp(l2 - mx)
    z = e0 + e1 + e2
    w0, w1, w2 = e0 / z, e1 / z, e2 / z
    for h in range(DIL_HEADS_PER_GROUP):
        sl = slice(h * HEAD_DIM, (h + 1) * HEAD_DIM)
        hs = slice(h, h + 1)
        o_ref[:, sl] = (w0[:, hs] * o0_ref[:, sl] + w1[:, hs] * o1_ref[:, sl]
                        + w2[:, hs] * o2_ref[:, sl]).astype(o_ref.dtype)


def _dil_mix(outs, lses, tm=512):
    t = outs[0].shape[0]
    ospec = pl.BlockSpec((tm, DIL_OUT), lambda i: (i, 0))
    lspec = pl.BlockSpec((tm, DIL_HEADS_PER_GROUP), lambda i: (i, 0))
    return pl.pallas_call(
        _dil_mix_kernel,
        grid=(t // tm,),
        in_specs=[ospec] * 3 + [lspec] * 3,
        out_specs=ospec,
        out_shape=jax.ShapeDtypeStruct((t, DIL_OUT), BF16),
        compiler_params=_params("parallel"),
        name="dilated_mix",
    )(*outs, *lses)


def _dilated_attention(qk, v, batch, seq):
    hg = DIL_HEADS_PER_GROUP
    outs, lses = [], []
    for g, (window, dil) in enumerate(DIL_GROUPS):
        assert window // (2 * dil) == DIL_RADIUS and seq % dil == 0
        o, lse = _dil_group(qk, v, g, dil, batch, seq)
        outs.append(o)
        lses.append(lse.reshape(batch, hg, seq).transpose(0, 2, 1).reshape(batch * seq, hg))
    return _dil_mix(outs, lses)


def _mla_q_kernel(a_ref, w_ref, g_ref, cos_ref, sin_ref, o_ref):
    a = a_ref[...]
    gain = g_ref[...]
    cos_t, sin_t = cos_ref[...], sin_ref[...]
    acc = _dot(a, w_ref[0])
    for h in range(MLA_HEADS):
        nxt = _dot(a, w_ref[h + 1]) if h + 1 < MLA_HEADS else None
        ms = jnp.sum(acc * acc, axis=-1, keepdims=True) * (1.0 / MLA_QK)
        y = acc * lax.rsqrt(ms + RMS_EPS) * gain
        o_ref[h, :, :MLA_NOPE] = y[:, :MLA_NOPE].astype(o_ref.dtype)
        o_ref[h, :, MLA_NOPE:] = _rotate_pairs(y[:, MLA_NOPE:], cos_t, sin_t,
                                               MLA_ROPE // 2).astype(o_ref.dtype)
        acc = nxt


def _mla_q_proj(cq_n, w_uq_h, gain, cos_t, sin_t, seq, tm=512):
    t, rank = cq_n.shape
    per = seq // tm
    return pl.pallas_call(
        _mla_q_kernel,
        grid=(t // tm,),
        in_specs=[pl.BlockSpec((tm, rank), lambda i: (i, 0)),
                  pl.BlockSpec((MLA_HEADS, rank, MLA_QK_PAD), lambda i: (0, 0, 0)),
                  pl.BlockSpec((1, MLA_QK_PAD), lambda i: (0, 0)),
                  pl.BlockSpec((tm, LANES), lambda i: (i % per, 0)),
                  pl.BlockSpec((tm, LANES), lambda i: (i % per, 0))],
        out_specs=pl.BlockSpec((MLA_HEADS, tm, MLA_QK_PAD), lambda i: (0, i, 0)),
        out_shape=jax.ShapeDtypeStruct((MLA_HEADS, t, MLA_QK_PAD), BF16),
        compiler_params=_params("parallel"),
        name="mla_q_proj",
    )(cq_n, w_uq_h, gain, cos_t, sin_t)


def _mla_kv_kernel(a_ref, w_ref, kr_ref, g0_ref, g1_ref, cos_ref, sin_ref, k_ref, v_ref):
    a = a_ref[...]
    kr = kr_ref[...]
    kr_ss = jnp.sum(kr * kr, axis=-1, keepdims=True)
    g0, g1 = g0_ref[...], g1_ref[...]
    cos_t, sin_t = cos_ref[...], sin_ref[...]
    ones = jnp.ones((a.shape[0], MLA_V), v_ref.dtype)
    kr_rot = _rotate_pairs(kr * g1, cos_t, sin_t, MLA_ROPE // 2)
    acc = _dot(a, w_ref[0])
    for h in range(MLA_HEADS):
        nxt = _dot(a, w_ref[h + 1]) if h + 1 < MLA_HEADS else None
        kn = acc[:, :MLA_NOPE]
        ms = (jnp.sum(kn * kn, axis=-1, keepdims=True) + kr_ss) * (1.0 / MLA_QK)
        inv = lax.rsqrt(ms + RMS_EPS)
        k_ref[h, :, :MLA_NOPE] = (kn * inv * g0).astype(k_ref.dtype)
        k_ref[h, :, MLA_NOPE:] = (kr_rot * inv).astype(k_ref.dtype)
        v_ref[h, :, :MLA_V] = acc[:, MLA_NOPE:].astype(v_ref.dtype)
        v_ref[h, :, MLA_V:] = ones
        acc = nxt


def _mla_kv_proj(ckv_n, w_ukv_h, k_r, g0, g1, cos_t, sin_t, seq, tm=1024):
    t, rank = ckv_n.shape
    per = seq // tm
    return pl.pallas_call(
        _mla_kv_kernel,
        grid=(t // tm,),
        in_specs=[pl.BlockSpec((tm, rank), lambda i: (i, 0)),
                  pl.BlockSpec((MLA_HEADS, rank, MLA_NOPE + MLA_V), lambda i: (0, 0, 0)),
                  pl.BlockSpec((tm, LANES), lambda i: (i, 0)),
                  pl.BlockSpec((1, LANES), lambda i: (0, 0)),
                  pl.BlockSpec((1, LANES), lambda i: (0, 0)),
                  pl.BlockSpec((tm, LANES), lambda i: (i % per, 0)),
                  pl.BlockSpec((tm, LANES), lambda i: (i % per, 0))],
        out_specs=[pl.BlockSpec((MLA_HEADS, tm, MLA_QK_PAD), lambda i: (0, i, 0)),
                   pl.BlockSpec((MLA_HEADS, tm, 2 * MLA_V), lambda i: (0, i, 0))],
        out_shape=[jax.ShapeDtypeStruct((MLA_HEADS, t, MLA_QK_PAD), BF16),
                   jax.ShapeDtypeStruct((MLA_HEADS, t, 2 * MLA_V), BF16)],
        compiler_params=_params("parallel"),
        name="mla_kv_proj",
    )(ckv_n, w_ukv_h, k_r, g0, g1, cos_t, sin_t)


MLA_KV_CHUNKS = 8
MLA_QK_AHEAD = 1


def _mla_attn_kernel(q_ref, k_ref, v_ref, o_ref):
    q = q_ref[...]
    tk = k_ref.shape[0] // MLA_KV_CHUNKS
    m = acc = None
    ahead = [_dot_nt(q, k_ref[i * tk:(i + 1) * tk, :]) for i in range(MLA_QK_AHEAD)]
    for c in range(MLA_KV_CHUNKS):
        s = ahead.pop(0)
        nxt = c + MLA_QK_AHEAD
        if nxt < MLA_KV_CHUNKS:
            ahead.append(_dot_nt(q, k_ref[nxt * tk:(nxt + 1) * tk, :]))
        m_c = jnp.max(s, axis=-1, keepdims=True)
        if c == 0:
            m = m_c
            acc = _dot(jnp.exp2(s - m).astype(BF16), v_ref[:tk, :])
        else:
            m_new = jnp.maximum(m, m_c)
            acc = jnp.exp2(m - m_new) * acc + _dot(jnp.exp2(s - m_new).astype(BF16),
                                                   v_ref[c * tk:(c + 1) * tk, :])
            m = m_new
    o_ref[...] = (acc[:, :MLA_V] / acc[:, MLA_V:]).astype(o_ref.dtype)


def _mla_attention(q, k, v, batch, seq, tq=512):
    t = q.shape[1]
    nq = seq // tq
    return pl.pallas_call(
        _mla_attn_kernel,
        grid=(batch, MLA_HEADS, nq),
        in_specs=[pl.BlockSpec((None, tq, MLA_QK_PAD), lambda b, h, i: (h, b * nq + i, 0)),
                  pl.BlockSpec((None, seq, MLA_QK_PAD), lambda b, h, i: (h, b, 0)),
                  pl.BlockSpec((None, seq, 2 * MLA_V), lambda b, h, i: (h, b, 0))],
        out_specs=pl.BlockSpec((tq, MLA_V), lambda b, h, i: (b * nq + i, h)),
        out_shape=jax.ShapeDtypeStruct((t, MLA_HEADS * MLA_V), BF16),
        compiler_params=_params("parallel", "parallel", "arbitrary"),
        name="mla_attention",
    )(q, k, v)


def _merge_kernel(ona_ref, odl_ref, omla_ref, wna_ref, wdl_ref, wmla_ref, g0_ref, g1_ref, g2_ref, o_ref):
    ona, odl, omla = ona_ref[...], odl_ref[...], omla_ref[...]
    for c in range(o_ref.shape[1] // GU_SPLIT):
        cols = slice(c * GU_SPLIT, (c + 1) * GU_SPLIT)
        acc = g0_ref[:, cols].astype(F32) * _dot(ona, wna_ref[:, cols])
        acc = acc + g1_ref[:, cols].astype(F32) * _dot(odl, wdl_ref[:, cols])
        acc = acc + g2_ref[:, cols].astype(F32) * _dot(omla, wmla_ref[:, cols])
        o_ref[:, cols] = acc.astype(o_ref.dtype)


def _merge(o_na, o_dl, o_mla, w_na, w_dl, w_mla, gates, tm=1024, tn=512):
    t = o_na.shape[0]
    d = w_na.shape[1]
    nj = d // tn

    def act(a):
        return pl.BlockSpec((tm, a.shape[1]), lambda i, j: (i, 0))

    def wgt(w):
        return pl.BlockSpec((w.shape[0], tn), lambda i, j: (0, j))

    def gate(idx):
        return pl.BlockSpec((tm, tn), lambda i, j: (i, idx * nj + j))

    return pl.pallas_call(
        _merge_kernel,
        grid=(t // tm, nj),
        in_specs=[act(o_na), act(o_dl), act(o_mla), wgt(w_na), wgt(w_dl), wgt(w_mla),
                  gate(0), gate(1), gate(2)],
        out_specs=pl.BlockSpec((tm, tn), lambda i, j: (i, j)),
        out_shape=jax.ShapeDtypeStruct((t, d), BF16),
        compiler_params=_params("parallel", "parallel"),
        name="branch_merge",
    )(o_na, o_dl, o_mla, w_na, w_dl, w_mla, gates, gates, gates)


GU_SPLIT = 256


def _swiglu_pieces(h, wg_ref, wu_ref, o_ref):
    for c in range(o_ref.shape[1] // GU_SPLIT):
        cols = slice(c * GU_SPLIT, (c + 1) * GU_SPLIT)
        g = _dot(h, wg_ref[:, cols].astype(BF16))
        u = _dot(h, wu_ref[:, cols].astype(BF16))
        o_ref[:, cols] = (g * jax.nn.sigmoid(g) * u).astype(o_ref.dtype)


def _gu_kernel(h_ref, wg_ref, wu_ref, o_ref):
    _swiglu_pieces(h_ref[...], wg_ref, wu_ref, o_ref)


def _swiglu_up(h, w_gu, tm=1024, tn=512):
    t, d = h.shape
    ff = w_gu.shape[1] // 2
    nj = ff // tn
    return pl.pallas_call(
        _gu_kernel,
        grid=(t // tm, nj),
        in_specs=[pl.BlockSpec((tm, d), lambda i, j: (i, 0)),
                  pl.BlockSpec((d, tn), lambda i, j: (0, j)),
                  pl.BlockSpec((d, tn), lambda i, j: (0, nj + j))],
        out_specs=pl.BlockSpec((tm, tn), lambda i, j: (i, j)),
        out_shape=jax.ShapeDtypeStruct((t, ff), BF16),
        compiler_params=_params("parallel", "parallel"),
        name="swiglu_up",
    )(h, w_gu, w_gu)


TOP_K = 2
MOE_TILE_TARGET = 1024
MOE_TILE_ALIGN = 32


def _moe_tile_rows(tokens):
    mean = TOP_K * tokens / N_EXPERTS
    sigma = math.sqrt(TOP_K * tokens * (1.0 / N_EXPERTS) * (1.0 - 1.0 / N_EXPERTS))
    tiles = max(1, round(mean / MOE_TILE_TARGET))
    rows = (mean + 2.0 * sigma) / tiles
    return int(-(-rows // MOE_TILE_ALIGN) * MOE_TILE_ALIGN)


def _moe_routing(route, tokens, tm):
    e_flat = jnp.concatenate([route[:, N_EXPERTS], route[:, N_EXPERTS + 1]]).astype(jnp.int32)
    onehot = (e_flat[:, None] == jnp.arange(N_EXPERTS, dtype=jnp.int32)[None, :]).astype(jnp.int32)
    csum = jnp.cumsum(onehot, axis=0)
    rank = jnp.sum((csum - onehot) * onehot, axis=1)
    padded = ((csum[-1] + tm - 1) // tm) * tm
    ends = jnp.cumsum(padded)
    slot = jnp.sum(onehot * (ends - padded)[None, :], axis=1) + rank
    n_tiles = (TOP_K * tokens) // tm + N_EXPERTS
    tile_start = jnp.arange(n_tiles, dtype=jnp.int32) * tm
    tile_expert = jnp.minimum(jnp.sum(tile_start[:, None] >= ends[None, :], axis=1), N_EXPERTS - 1)
    te = jnp.concatenate([tile_expert, ends[-1:] // tm]).astype(jnp.int32)
    return slot.astype(jnp.int32), te


def _dispatch_kernel(slot_ref, x_ref, g_ref, sc_ref, sh_ref, dst_in_ref, dst_ref, h_ref, sem, *, rows, tokens):
    del dst_in_ref
    base = pl.program_id(0) * rows
    h_ref[...] = _modulated_norm(x_ref[...], g_ref[...], sc_ref[...], sh_ref[...])

    def row_copy(r, choice):
        slot = slot_ref[choice * tokens + base + r]
        return pltpu.make_async_copy(h_ref.at[pl.ds(r, 1), :], dst_ref.at[pl.ds(slot, 1), :], sem)

    def start(r, carry):
        row_copy(r, 0).start()
        row_copy(r, 1).start()
        return carry

    def wait(r, carry):
        row_copy(r, 0).wait()
        row_copy(r, 1).wait()
        return carry

    lax.fori_loop(0, rows, start, 0, unroll=8)
    lax.fori_loop(0, rows, wait, 0, unroll=8)


def _moe_dispatch(x, g, sc, sh, slot, seq, n_rows, rows=256):
    t, d = x.shape
    per = seq // rows
    return pl.pallas_call(
        functools.partial(_dispatch_kernel, rows=rows, tokens=t),
        grid_spec=pltpu.PrefetchScalarGridSpec(
            num_scalar_prefetch=1,
            grid=(t // rows,),
            in_specs=[pl.BlockSpec((rows, d), lambda i, s: (i, 0)),
                      pl.BlockSpec((1, d), lambda i, s: (0, 0)),
                      pl.BlockSpec((None, 1, d), lambda i, s: (i // per, 0, 0)),
                      pl.BlockSpec((None, 1, d), lambda i, s: (i // per, 0, 0)),
                      pl.BlockSpec(memory_space=pl.ANY)],
            out_specs=pl.BlockSpec(memory_space=pl.ANY),
            scratch_shapes=[pltpu.VMEM((rows, d), F32), pltpu.SemaphoreType.DMA(())]),
        out_shape=jax.ShapeDtypeStruct((n_rows, d), F32),
        input_output_aliases={5: 0},
        compiler_params=_params("arbitrary"),
        name="moe_dispatch",
    )(slot, x, g.reshape(1, d), sc, sh, jnp.zeros((n_rows, d), F32))


def _moe_up_kernel(te_ref, xs_ref, wg_ref, wu_ref, o_ref, hb_ref, *, n_tiles):
    used = pl.program_id(0) < te_ref[n_tiles]

    @pl.when(used & (pl.program_id(1) == 0))
    def _():
        hb_ref[...] = xs_ref[...].astype(BF16)

    @pl.when(used)
    def _():
        _swiglu_pieces(hb_ref[...], wg_ref, wu_ref, o_ref)

    @pl.when(jnp.logical_not(used))
    def _():
        o_ref[...] = jnp.zeros(o_ref.shape, o_ref.dtype)


def _moe_up(xs, w_gu, te, tn=512):
    n_rows, d = xs.shape
    ff = w_gu.shape[2] // 2
    nj = ff // tn
    n_tiles = te.shape[0] - 1
    tm = n_rows // n_tiles

    def last_used(i, te_ref):
        return jnp.minimum(i, te_ref[n_tiles] - 1)

    return pl.pallas_call(
        functools.partial(_moe_up_kernel, n_tiles=n_tiles),
        grid_spec=pltpu.PrefetchScalarGridSpec(
            num_scalar_prefetch=1,
            grid=(n_tiles, nj),
            in_specs=[pl.BlockSpec((tm,d), lambda i, j, te_ref: (last_used(i, te_ref), 0)),
                      pl.BlockSpec((None, d, tn), lambda i, j, te_ref: (te_ref[i], 0, j)),
                      pl.BlockSpec((None, d, tn), lambda i, j, te_ref: (te_ref[i], 0, nj + j))],
            out_specs=pl.BlockSpec((tm,tn), lambda i, j, te_ref: (i, j)),
            scratch_shapes=[pltpu.VMEM((tm, d), BF16)]),
        out_shape=jax.ShapeDtypeStruct((n_rows, ff), BF16),
        compiler_params=_params("parallel", "arbitrary"),
        name="moe_up",
    )(te, xs, w_gu, w_gu)


def _moe_down_kernel(te_ref, a_ref, w_ref, o_ref, *, n_tiles):
    used = pl.program_id(0) < te_ref[n_tiles]

    @pl.when(used)
    def _():
        o_ref[...] = _dot(a_ref[...], w_ref[...])

    @pl.when(jnp.logical_not(used))
    def _():
        o_ref[...] = jnp.zeros(o_ref.shape, o_ref.dtype)


def _moe_down(act, w_down, te, tn=256):
    n_rows, ff = act.shape
    d = w_down.shape[2]
    n_tiles = te.shape[0] - 1
    tm = n_rows // n_tiles

    def last_used(i, te_ref):
        return jnp.minimum(i, te_ref[n_tiles] - 1)

    return pl.pallas_call(
        functools.partial(_moe_down_kernel, n_tiles=n_tiles),
        grid_spec=pltpu.PrefetchScalarGridSpec(
            num_scalar_prefetch=1,
            grid=(n_tiles, d // tn),
            in_specs=[pl.BlockSpec((tm,ff), lambda i, j, te_ref: (last_used(i, te_ref), 0)),
                      pl.BlockSpec((None, ff, tn), lambda i, j, te_ref: (te_ref[i], 0, j))],
            out_specs=pl.BlockSpec((tm,tn), lambda i, j, te_ref: (i, j))),
        out_shape=jax.ShapeDtypeStruct((n_rows, d), F32),
        compiler_params=_params("parallel", "arbitrary"),
        name="moe_down",
    )(te, act, w_down)


def _combine_kernel(slot_ref, x_ref, gate_ref, w_ref, ys_ref, o_ref, buf_ref, sem, *, rows, tokens):
    base = pl.program_id(0) * rows

    def row_copy(r, choice):
        slot = slot_ref[choice * tokens + base + r]
        return pltpu.make_async_copy(ys_ref.at[pl.ds(slot, 1), :], buf_ref.at[choice, pl.ds(r, 1), :], sem)

    def start(r, carry):
        row_copy(r, 0).start()
        row_copy(r, 1).start()
        return carry

    def wait(r, carry):
        row_copy(r, 0).wait()
        row_copy(r, 1).wait()
        return carry

    lax.fori_loop(0, rows, start, 0, unroll=8)
    lax.fori_loop(0, rows, wait, 0, unroll=8)
    w = w_ref[...]
    y = w[:, 0:1] * buf_ref[0] + w[:, 1:2] * buf_ref[1]
    o_ref[...] = x_ref[...] + gate_ref[...] * y


def _moe_combine(x, gate, w12, ys, slot, seq, rows=256):
    t, d = x.shape
    per = seq // rows
    return pl.pallas_call(
        functools.partial(_combine_kernel, rows=rows, tokens=t),
        grid_spec=pltpu.PrefetchScalarGridSpec(
            num_scalar_prefetch=1,
            grid=(t // rows,),
            in_specs=[pl.BlockSpec((rows, d), lambda i, s: (i, 0)),
                      pl.BlockSpec((None, 1, d), lambda i, s: (i // per, 0, 0)),
                      pl.BlockSpec((rows, TOP_K), lambda i, s: (i, 0)),
                      pl.BlockSpec(memory_space=pl.ANY)],
            out_specs=pl.BlockSpec((rows, d), lambda i, s: (i, 0)),
            scratch_shapes=[pltpu.VMEM((TOP_K, rows, d), F32), pltpu.SemaphoreType.DMA(())]),
        out_shape=jax.ShapeDtypeStruct((t, d), F32),
        compiler_params=_params("arbitrary"),
        name="moe_combine",
    )(slot, x, gate, w12, ys)


def _router_kernel(x_ref, g_ref, sc_ref, sh_ref, w_ref, o_ref):
    h = _modulated_norm(x_ref[...], g_ref[...], sc_ref[...], sh_ref[...])
    logits = jnp.dot(h, w_ref[...], preferred_element_type=F32, precision=lax.Precision.HIGHEST)
    lane = lax.broadcasted_iota(jnp.int32, logits.shape, 1).astype(F32)
    lg = jnp.where(lane < N_EXPERTS, logits, NEG_INF)
    m1 = jnp.max(lg, axis=-1, keepdims=True)
    i1 = jnp.min(jnp.where(lg == m1, lane, float(LANES)), axis=-1, keepdims=True)
    lg2 = jnp.where(lane == i1, NEG_INF, lg)
    m2 = jnp.max(lg2, axis=-1, keepdims=True)
    i2 = jnp.min(jnp.where(lg2 == m2, lane, float(LANES)), axis=-1, keepdims=True)
    e2 = jnp.exp(m2 - m1)
    z = 1.0 + e2
    out = jnp.where(lane == N_EXPERTS, i1, 0.0) + jnp.where(lane == N_EXPERTS + 1, i2, 0.0)
    out = out + jnp.where(lane == N_EXPERTS + 2, 1.0 / z, 0.0) + jnp.where(lane == N_EXPERTS + 3, e2 / z, 0.0)
    o_ref[...] = out


def _router(x, g, sc, sh, w_router_pad, seq, tm=512):
    t, d = x.shape
    per = seq // tm
    return pl.pallas_call(
        _router_kernel,
        grid=(t // tm,),
        in_specs=[pl.BlockSpec((tm, d), lambda i: (i, 0)),
                  pl.BlockSpec((1, d), lambda i: (0, 0)),
                  pl.BlockSpec((None, 1, d), lambda i: (i // per, 0, 0)),
                  pl.BlockSpec((None, 1, d), lambda i: (i // per, 0, 0)),
                  pl.BlockSpec((d, LANES), lambda i: (0, 0))],
        out_specs=pl.BlockSpec((tm, LANES), lambda i: (i, 0)),
        out_shape=jax.ShapeDtypeStruct((t, LANES), F32),
        compiler_params=_params("parallel"),
        name="router",
    )(x, g.reshape(1, d), sc, sh, w_router_pad)


MM_TM = 1024
MM_TN = 1024
MM_SPLIT = 256


def _col_tile(n):
    return MM_TN if n % MM_TN == 0 else MM_TN // 2


def _residual_matmul(a, w, x, gate, seq, name, tm=MM_TM, tn=None):
    per = seq // tm
    tn = _col_tile(w.shape[1]) if tn is None else tn
    return _matmul(a, w, tm=tm, tn=tn, out_dtype=F32, epilogue=_ep_residual, split=MM_SPLIT,
                   extras=[(x, (tm, tn), lambda i, j: (i, j)),
                           (gate, (None, 1, tn), lambda i, j: (i // per, 0, j))],
                   name=name)


def _token_mixer(h, x, gate, lw, batch, seq, tables):
    d = h.shape[1]
    w_in = lw["w_in"]
    tm = MM_TM
    per = seq // tm
    cos_d, sin_d, cos_m, sin_m = tables

    def cols(lo, hi):
        return w_in[:, lo:hi].astype(BF16)

    def tile_gain(gq, gk, heads):
        gq = gq * (LOG2E / math.sqrt(HEAD_DIM))
        return jnp.concatenate([jnp.tile(gq, heads), jnp.tile(gk, heads)]).reshape(1, -1).astype(F32)

    o = 0
    tn = _col_tile(2 * NA_W)
    na_qk = _matmul(h, cols(o, o + 2 * NA_W), tm=tm, tn=tn, out_dtype=BF16, epilogue=_ep_headnorm,
                    split=MM_SPLIT,
                    extras=[(tile_gain(lw["na_q_norm"], lw["na_k_norm"], NA_HEADS), (1, tn),
                             lambda i, j: (0, j))], name="na_qk_proj")
    o += 2 * NA_W
    na_v = _matmul(h, cols(o, o + NA_W), tm=tm, tn=_col_tile(NA_W), out_dtype=BF16, split=MM_SPLIT, name="na_v_proj")
    o += NA_W
    tn = _col_tile(2 * DIL_W)
    dl_qk = _matmul(h, cols(o, o + 2 * DIL_W), tm=tm, tn=tn, out_dtype=F32, epilogue=_ep_headnorm_rope,
                    split=MM_SPLIT,
                    extras=[(tile_gain(lw["dil_q_norm"], lw["dil_k_norm"], DIL_HEADS), (1, tn),
                             lambda i, j: (0, j)),
                            (cos_d, (tm, LANES), lambda i, j: (i % per, 0)),
                            (sin_d, (tm, LANES), lambda i, j: (i % per, 0)),
                            (_pair_swap_matrix(MM_SPLIT, ROT_DIM // 2), (MM_SPLIT, MM_SPLIT),
                             lambda i, j: (0, 0))], name="dil_qk_proj")
    o += 2 * DIL_W
    dl_v = _matmul(h, cols(o, o + DIL_W), tm=tm, tn=_col_tile(DIL_W), out_dtype=F32, split=MM_SPLIT, name="dil_v_proj")
    o += DIL_W
    cq_n = _matmul(h, cols(o, o + MLA_Q_RANK), tm=tm, tn=MLA_Q_RANK, out_dtype=BF16, epilogue=_ep_rownorm,
                   extras=[(lw["mla_q_a_norm"].reshape(1, -1), (1, MLA_Q_RANK), lambda i, j: (0, 0))],
                   name="mla_cq_proj")
    o += MLA_Q_RANK
    ckv_n = _matmul(h, cols(o, o + MLA_KV_RANK), tm=tm, tn=MLA_KV_RANK, out_dtype=BF16, epilogue=_ep_rownorm,
                    extras=[(lw["mla_kv_a_norm"].reshape(1, -1), (1, MLA_KV_RANK), lambda i, j: (0, 0))],
                    name="mla_ckv_proj")
    o += MLA_KV_RANK
    w_kr = jnp.pad(w_in[:, o:o + MLA_ROPE], ((0, 0), (0, LANES - MLA_ROPE))).astype(BF16)
    k_r = _matmul(h, w_kr, tm=tm, tn=LANES, out_dtype=F32, name="mla_kr_proj")
    o += MLA_ROPE
    gates = _matmul(h, cols(o, o + 3 * d), tm=tm, tn=_col_tile(3 * d), out_dtype=BF16, epilogue=_ep_sigmoid,
                    split=MM_SPLIT, name="gate_proj")

    o_na = _na_attention(na_qk, na_v, lw["na_rpb"], batch, seq)
    o_dl = _dilated_attention(dl_qk, dl_v, batch, seq)
    w_uq = lw["mla_w_uq"].reshape(MLA_Q_RANK, MLA_HEADS, MLA_QK).transpose(1, 0, 2)
    w_uq = jnp.pad(w_uq, ((0, 0), (0, 0), (0, MLA_QK_PAD - MLA_QK))).astype(BF16)
    w_ukv = lw["mla_w_ukv"].reshape(MLA_KV_RANK, MLA_HEADS, MLA_NOPE + MLA_V).transpose(1, 0, 2).astype(BF16)
    gq = jnp.pad(lw["mla_q_norm"] * (LOG2E / math.sqrt(MLA_QK)), (0, MLA_QK_PAD - MLA_QK)).reshape(1, MLA_QK_PAD)
    gk = lw["mla_k_norm"]
    gk0 = gk[:MLA_NOPE].reshape(1, LANES)
    gk1 = jnp.pad(gk[MLA_NOPE:], (0, LANES - MLA_ROPE)).reshape(1, LANES)
    q_m = _mla_q_proj(cq_n, w_uq, gq, cos_m, sin_m, seq)
    k_m, v_m = _mla_kv_proj(ckv_n, w_ukv, k_r, gk0, gk1, cos_m, sin_m, seq)
    o_mla = _mla_attention(q_m, k_m, v_m, batch, seq)

    merged = _merge(o_na, o_dl, o_mla, lw["w_branch_na"].astype(BF16), lw["w_branch_dil"].astype(BF16),
                    lw["w_branch_mla"].astype(BF16), gates)
    return _residual_matmul(merged, lw["w_out"].astype(BF16), x, gate, seq, "out_proj")


def kernel(x, c, w_ada, b_ada, norm_mix, norm_ffn, w_in, na_q_norm, na_k_norm, na_rpb, dil_q_norm, dil_k_norm, mla_q_a_norm, mla_w_uq, mla_kv_a_norm, mla_w_ukv, mla_q_norm, mla_k_norm, w_branch_na, w_branch_dil, w_branch_mla, w_out, ffn_w_gu, ffn_w_down, moe_router, moe_w_gu, moe_w_down):
    batch, seq, d = x.shape
    depth = w_ada.shape[0]
    assert seq % MM_TM == 0 and seq % GRID_W == 0
    t = batch * seq
    xf = x.reshape(t, d)

    c_pad = jnp.pad(c, ((0, (-batch) % 8), (0, 0)))
    mod = _adaln(c_pad, w_ada, b_ada)[:, :batch].reshape(depth, batch, 6, 1, d)
    tables = _rope_tables(seq, ROT_DIM, 1.0) + _rope_tables(seq, MLA_ROPE, 1.0)

    for l in range(depth):
        sh1, sc1, g1, sh2, sc2, g2 = [mod[l, :, i] for i in range(6)]
        lw = dict(w_in=w_in[l], na_q_norm=na_q_norm[l], na_k_norm=na_k_norm[l], na_rpb=na_rpb[l],
                  dil_q_norm=dil_q_norm[l], dil_k_norm=dil_k_norm[l], mla_q_a_norm=mla_q_a_norm[l],
                  mla_w_uq=mla_w_uq[l], mla_kv_a_norm=mla_kv_a_norm[l], mla_w_ukv=mla_w_ukv[l],
                  mla_q_norm=mla_q_norm[l], mla_k_norm=mla_k_norm[l], w_branch_na=w_branch_na[l],
                  w_branch_dil=w_branch_dil[l], w_branch_mla=w_branch_mla[l], w_out=w_out[l])
        h = _normmod(xf, norm_mix[l], sc1, sh1, seq)
        xf = _token_mixer(h, xf, g1, lw, batch, seq, tables)

        if l % 2 == 0:
            h = _normmod(xf, norm_ffn[l], sc2, sh2, seq)
            act = _swiglu_up(h, ffn_w_gu[l // 2].astype(BF16))
            xf = _residual_matmul(act, ffn_w_down[l // 2].astype(BF16), xf, g2, seq, "down_proj", tm=1024, tn=256)
        else:
            w_r = jnp.pad(moe_router[l // 2], ((0, 0), (0, LANES - N_EXPERTS)))
            route = _router(xf, norm_ffn[l], sc2, sh2, w_r, seq)
            tile_rows = _moe_tile_rows(t)
            slot, te = _moe_routing(route, t, tile_rows)
            n_rows = (te.shape[0] - 1) * tile_rows
            xs = _moe_dispatch(xf, norm_ffn[l], sc2, sh2, slot, seq, n_rows)
            act = _moe_up(xs, moe_w_gu[l // 2], te)
            ys = _moe_down(act, moe_w_down[l // 2].astype(BF16), te)
            xf = _moe_combine(xf, g2, route[:, N_EXPERTS + 2:N_EXPERTS + 4], ys, slot, seq)
    return xf.reshape(batch, seq, d)
```

```python
import functools
import math

import numpy as np
import jax
import jax.numpy as jnp
from jax import lax
from jax.experimental import pallas as pl
from jax.experimental.pallas import tpu as pltpu

GRID_W = 64
HEAD_DIM = 128
ROPE_THETA = 500000.0
ROT_DIM = HEAD_DIM // 4
RMS_EPS = 1e-6
NEG_INF = -1e30
NA_HEADS = 8
NA_KH = 8
NA_KW = 16
DIL_GROUPS = ((128, 1), (512, 4), (2048, 16))
DIL_HEADS_PER_GROUP = 4
DIL_HEADS = DIL_HEADS_PER_GROUP * len(DIL_GROUPS)
DIL_RADIUS = 64
MLA_HEADS = 8
MLA_NOPE = 128
MLA_ROPE = 64
MLA_V = 128
MLA_Q_RANK = 768
MLA_KV_RANK = 512
MLA_QK = MLA_NOPE + MLA_ROPE
MLA_QK_PAD = 256
N_EXPERTS = 8
LANES = 128
MXU_COLS = 256

NA_W = NA_HEADS * HEAD_DIM
DIL_W = DIL_HEADS * HEAD_DIM
DIL_OUT = DIL_HEADS_PER_GROUP * HEAD_DIM

VMEM_LIMIT_BYTES = 48 * 1024 * 1024

F32 = jnp.float32
BF16 = jnp.bfloat16
LOG2E = math.log2(math.e)
LN2 = math.log(2.0)


def _params(*sem):
    return pltpu.CompilerParams(dimension_semantics=sem, vmem_limit_bytes=VMEM_LIMIT_BYTES)


def _dot(a, b):
    return jnp.dot(a, b, preferred_element_type=F32)


def _dot_nt(a, b):
    return lax.dot_general(a, b, (((1,), (1,)), ((), ())), preferred_element_type=F32)


def _adaln_kernel(c_ref, w_ref, b_ref, o_ref):
    o_ref[...] = _dot(c_ref[...], w_ref[...]) + b_ref[...]


def _adaln(c_pad, w_ada, b_ada):
    depth, d, n = w_ada.shape
    tn = 1024
    return pl.pallas_call(
        _adaln_kernel,
        grid=(depth, n // tn),
        in_specs=[pl.BlockSpec(c_pad.shape, lambda l, j: (0, 0)),
                  pl.BlockSpec((None, d, tn), lambda l, j: (l, 0, j)),
                  pl.BlockSpec((None, 1, tn), lambda l, j: (l, 0, j))],
        out_specs=pl.BlockSpec((None, c_pad.shape[0], tn), lambda l, j: (l, 0, j)),
        out_shape=jax.ShapeDtypeStruct((depth, c_pad.shape[0], n), F32),
        compiler_params=_params("parallel", "parallel"),
        name="adaln",
    )(c_pad, w_ada, b_ada.reshape(depth, 1, n))


def _modulated_norm(x, g, sc, sh):
    ms = jnp.mean(x * x, axis=-1, keepdims=True)
    return (x * lax.rsqrt(ms + RMS_EPS) * g) * (1.0 + sc) + sh


def _normmod_kernel(x_ref, g_ref, sc_ref, sh_ref, o_ref):
    o_ref[...] = _modulated_norm(x_ref[...], g_ref[...], sc_ref[...], sh_ref[...]).astype(o_ref.dtype)


def _normmod(x, g, sc, sh, seq, tm=512):
    t, d = x.shape
    per = seq // tm
    return pl.pallas_call(
        _normmod_kernel,
        grid=(t // tm,),
        in_specs=[pl.BlockSpec((tm, d), lambda i: (i, 0)),
                  pl.BlockSpec((1, d), lambda i: (0, 0)),
                  pl.BlockSpec((None, 1, d), lambda i: (i // per, 0, 0)),
                  pl.BlockSpec((None, 1, d), lambda i: (i // per, 0, 0))],
        out_specs=pl.BlockSpec((tm, d), lambda i: (i, 0)),
        out_shape=jax.ShapeDtypeStruct((t, d), BF16),
        compiler_params=_params("parallel"),
        name="normmod",
    )(x, g.reshape(1, d), sc, sh)


def _mm_kernel(*refs, n_extra, epilogue, split):
    a_ref, b_ref = refs[0], refs[1]
    extras = refs[2:2 + n_extra]
    o_ref = refs[2 + n_extra]
    a = a_ref[...]
    n_pieces = b_ref.shape[1] // split
    acc = _dot(a, b_ref[:, :split])
    for c in range(n_pieces):
        nxt = _dot(a, b_ref[:, (c + 1) * split:(c + 2) * split]) if c + 1 < n_pieces else None
        epilogue(acc, extras, o_ref, slice(c * split, (c + 1) * split))
        acc = nxt


def _ep_store(acc, extras, o_ref, cols):
    o_ref[:, cols] = acc.astype(o_ref.dtype)


def _matmul(a, b, *, tm, tn, out_dtype, epilogue=_ep_store, extras=(), split=None, name):
    m, kdim = a.shape
    n = b.shape[1]
    split = tn if split is None else split
    assert m % tm == 0 and n % tn == 0 and tn % split == 0
    in_specs = [pl.BlockSpec((tm, kdim), lambda i, j: (i, 0)),
                pl.BlockSpec((kdim, tn), lambda i, j: (0, j))]
    in_specs += [pl.BlockSpec(bs, im) for (_, bs, im) in extras]
    return pl.pallas_call(
        functools.partial(_mm_kernel, n_extra=len(extras), epilogue=epilogue, split=split),
        grid=(m // tm, n // tn),
        in_specs=in_specs,
        out_specs=pl.BlockSpec((tm, tn), lambda i, j: (i, j)),
        out_shape=jax.ShapeDtypeStruct((m, n), out_dtype),
        compiler_params=_params("parallel", "parallel"),
        name=name,
    )(a, b, *[e[0] for e in extras])


def _head_rms(blk, gain):
    ms = jnp.mean(blk * blk, axis=-1, keepdims=True)
    return blk * lax.rsqrt(ms + RMS_EPS) * gain


def _head_slices(cols):
    return [slice(c, c + HEAD_DIM) for c in range(cols.start, cols.stop, HEAD_DIM)]


def _ep_headnorm(acc, extras, o_ref, cols):
    for i, sl in enumerate(_head_slices(cols)):
        blk = acc[:, i * HEAD_DIM:(i + 1) * HEAD_DIM]
        o_ref[:, sl] = _head_rms(blk, extras[0][:, sl]).astype(o_ref.dtype)


def _rotate_pairs(y, cos_t, sin_t, half):
    lane = lax.broadcasted_iota(jnp.int32, y.shape, 1)
    swapped = jnp.where(lane < half, pltpu.roll(y, LANES - half, 1), pltpu.roll(y, half, 1))
    return y * cos_t + swapped * sin_t


def _pair_swap_matrix(width, half):
    p = np.zeros((width, width), np.float32)
    for base in range(0, width, HEAD_DIM):
        for i in range(half):
            p[base + i + half, base + i] = 1.0
            p[base + i, base + i + half] = 1.0
    return jnp.asarray(p, BF16)


def _ep_headnorm_rope(acc, extras, o_ref, cols):
    cos_t = extras[1][...]
    sin_t = extras[2][...]
    heads = _head_slices(cols)
    y = jnp.concatenate([_head_rms(acc[:, i * HEAD_DIM:(i + 1) * HEAD_DIM], extras[0][:, sl])
                         for i, sl in enumerate(heads)], axis=1)
    swapped = _dot(y.astype(BF16), extras[3][...])
    for i, sl in enumerate(heads):
        loc = slice(i * HEAD_DIM, (i + 1) * HEAD_DIM)
        o_ref[:, sl] = (y[:, loc] * cos_t + swapped[:, loc] * sin_t).astype(o_ref.dtype)


def _ep_rownorm(acc, extras, o_ref, cols):
    o_ref[:, cols] = _head_rms(acc, extras[0][:, cols]).astype(o_ref.dtype)


def _ep_sigmoid(acc, extras, o_ref, cols):
    o_ref[:, cols] = jax.nn.sigmoid(acc).astype(o_ref.dtype)


def _ep_residual(acc, extras, o_ref, cols):
    o_ref[:, cols] = extras[0][:, cols] + extras[1][:, cols] * acc


def _rope_tables(seq, rot, fill):
    half = rot // 2
    inv = ROPE_THETA ** (-jnp.arange(half, dtype=F32) * (2.0 / rot))
    ang = jnp.arange(seq, dtype=jnp.int32).astype(F32)[:, None] * inv[None, :]
    cos, sin = jnp.cos(ang), jnp.sin(ang)
    pad = LANES - rot
    cos_t = jnp.concatenate([cos, cos, jnp.full((seq, pad), fill, F32)], axis=1)
    sin_t = jnp.concatenate([-sin, sin, jnp.zeros((seq, pad), F32)], axis=1)
    return cos_t, sin_t


NA_RB = 4
NA_KROWS = NA_KH + NA_RB
NA_GROUP = 2


def _na_geometry(rows):
    kh = min(NA_KH, rows)
    assert rows % (NA_RB * NA_GROUP) == 0 and rows >= NA_KROWS
    patterns, ids, kstarts = [], [], []
    for r in range(0, rows, NA_RB):
        ks = int(np.clip(r - NA_KH // 2, 0, rows - NA_KROWS))
        pat = tuple((int(np.clip(r + a - NA_KH // 2, 0, rows - kh)) - ks, ks - (r + a)) for a in range(NA_RB))
        if pat not in patterns:
            patterns.append(pat)
        ids.append(patterns.index(pat))
        kstarts.append(ks)
    return patterns, np.array(ids + kstarts, np.int32)


def _na_bias_table(rpb, rows, patterns):
    h = rpb.shape[0]
    kh = min(NA_KH, rows)
    qc = np.arange(GRID_W)
    kc = np.arange(GRID_W)
    cstart = np.clip(qc - NA_KW // 2, 0, GRID_W - NA_KW)
    ok = (kc[None, :] >= cstart[:, None]) & (kc[None, :] < cstart[:, None] + NA_KW)
    dc = np.clip(kc[None, :] - qc[:, None] + NA_KW - 1, 0, 2 * NA_KW - 2)
    by_col = rpb.astype(F32)[:, :, dc] * LOG2E + jnp.where(ok, 0.0, NEG_INF).astype(F32)
    masked_row = 2 * NA_KH - 1
    by_col = jnp.concatenate([by_col, jnp.full((h, 1, GRID_W, GRID_W), NEG_INF, F32)], axis=1)
    kr = np.arange(NA_KROWS)
    dr = np.zeros((len(patterns), NA_RB, NA_KROWS), np.int32)
    for p, pat in enumerate(patterns):
        for a, (start, offset) in enumerate(pat):
            row_ok = (kr >= start) & (kr < start + kh)
            dr[p, a] = np.where(row_ok, kr + offset + NA_KH - 1, masked_row)
    tab = jnp.take(by_col, jnp.asarray(dr.reshape(-1)), axis=1)
    tab = tab.reshape(h, len(patterns), NA_RB, NA_KROWS, GRID_W, GRID_W)
    tab = tab.transpose(0, 1, 2, 4, 3, 5)
    return tab.reshape(h, len(patterns), NA_RB * GRID_W, NA_KROWS * GRID_W)


def _na_kernel(geo_ref, q_ref, k_ref, v_ref, b_ref, o_ref, *, n_blocks):
    qn = NA_RB * GRID_W
    kn = NA_KROWS * GRID_W
    ones = jnp.ones((kn, HEAD_DIM), BF16)

    def scores(blk):
        q0 = pl.multiple_of(blk * qn, qn)
        k0 = pl.multiple_of(geo_ref[n_blocks + blk] * GRID_W, GRID_W)
        s = _dot_nt(q_ref[pl.ds(q0, qn), :], k_ref[pl.ds(k0, kn), :]) + b_ref[geo_ref[blk]]
        return s, q0, k0

    def finish(s, q0, k0):
        v1 = jnp.concatenate([v_ref[pl.ds(k0, kn), :], ones], axis=1)
        m = jnp.max(s, axis=-1, keepdims=True)
        acc = _dot(jnp.exp2(s - m).astype(BF16), v1)
        o_ref[pl.ds(q0, qn), :] = (acc[:, :HEAD_DIM] / acc[:, HEAD_DIM:]).astype(o_ref.dtype)

    def body(i, carry):
        pending = [scores(i * NA_GROUP + j) for j in range(NA_GROUP)]
        for item in pending:
            finish(*item)
        return carry

    lax.fori_loop(0, n_blocks // NA_GROUP, body, 0)


def _na_attention(qk, v, rpb, batch, seq):
    rows = seq // GRID_W
    patterns, geo = _na_geometry(rows)
    bias = _na_bias_table(rpb, rows, patterns)
    t = qk.shape[0]
    blk = (seq, HEAD_DIM)
    return pl.pallas_call(
        functools.partial(_na_kernel, n_blocks=rows // NA_RB),
        grid_spec=pltpu.PrefetchScalarGridSpec(
            num_scalar_prefetch=1,
            grid=(batch, NA_HEADS),
            in_specs=[pl.BlockSpec(blk, lambda b, h, geo_ref: (b, h)),
                      pl.BlockSpec(blk, lambda b, h, geo_ref: (b, NA_HEADS + h)),
                      pl.BlockSpec(blk, lambda b, h, geo_ref: (b, h)),
                      pl.BlockSpec((None,) + bias.shape[1:], lambda b, h, geo_ref: (h, 0, 0, 0))],
            out_specs=pl.BlockSpec(blk, lambda b, h, geo_ref: (b, h))),
        out_shape=jax.ShapeDtypeStruct((t, NA_W), BF16),
        compiler_params=_params("parallel", "parallel"),
        name="na_attention",
    )(jnp.asarray(geo), qk, qk, v, bias)


DIL_QCHUNK = 2 * DIL_RADIUS
DIL_KWIN = 4 * DIL_RADIUS
DIL_GROUP = 4


def _dil_window_mask():
    r = np.arange(DIL_QCHUNK)[None, :, None]
    c = np.arange(DIL_KWIN)[None, None, :]
    off = (np.arange(3) * DIL_RADIUS)[:, None, None]
    return jnp.asarray(np.where(np.abs(c - r - off) <= DIL_RADIUS, 0.0, NEG_INF), F32)


def _dil_kernel(q_ref, k_ref, v_ref, mask_ref, o_ref, lse_ref, *, seq, dil):
    length = seq // dil
    n_chunk = length // DIL_QCHUNK
    ones = jnp.ones((DIL_KWIN, HEAD_DIM), BF16)

    def rows(residue, first, count):
        return pl.ds(residue + first * dil, count, stride=dil)

    def scores(it):
        residue = it // n_chunk
        q0 = (it % n_chunk) * DIL_QCHUNK
        k0 = jnp.clip(q0 - DIL_RADIUS, 0, length - DIL_KWIN)
        q = q_ref[rows(residue, q0, DIL_QCHUNK), :].astype(BF16)
        k = k_ref[rows(residue, k0, DIL_KWIN), :].astype(BF16)
        return _dot_nt(q, k) + mask_ref[(q0 - k0) // DIL_RADIUS], residue, q0, k0

    def finish(s, residue, q0, k0):
        v1 = jnp.concatenate([v_ref[rows(residue, k0, DIL_KWIN), :].astype(BF16), ones], axis=1)
        m = jnp.max(s, axis=-1, keepdims=True)
        acc = _dot(jnp.exp2(s - m).astype(BF16), v1)
        den = acc[:, HEAD_DIM:]
        o_ref[rows(residue, q0, DIL_QCHUNK), :] = acc[:, :HEAD_DIM] / den
        lse_ref[rows(residue, q0, DIL_QCHUNK), :] = m * LN2 + jnp.log(den[:, :1])

    def body(i, carry):
        pending = [scores(i * DIL_GROUP + j) for j in range(DIL_GROUP)]
        for item in pending:
            finish(*item)
        return carry

    assert (dil * n_chunk) % DIL_GROUP == 0
    lax.fori_loop(0, dil * n_chunk // DIL_GROUP, body, 0)


def _dil_group(qk, v, group, dil, batch, seq):
    length = seq // dil
    assert length % DIL_QCHUNK == 0 and length >= DIL_KWIN
    hg = DIL_HEADS_PER_GROUP
    blk = (seq, HEAD_DIM)
    mask = _dil_window_mask()
    return pl.pallas_call(
        functools.partial(_dil_kernel, seq=seq, dil=dil),
        grid=(batch, hg),
        in_specs=[pl.BlockSpec(blk, lambda b, h: (b, group * hg + h)),
                  pl.BlockSpec(blk, lambda b, h: (b, DIL_HEADS + group * hg + h)),
                  pl.BlockSpec(blk, lambda b, h: (b, group * hg + h)),
                  pl.BlockSpec(mask.shape, lambda b, h: (0, 0, 0))],
        out_specs=[pl.BlockSpec(blk, lambda b, h: (b, h)),
                   pl.BlockSpec((None, seq, 1), lambda b, h: (b * hg + h, 0, 0))],
        out_shape=[jax.ShapeDtypeStruct((batch * seq, DIL_OUT), F32),
                   jax.ShapeDtypeStruct((batch * hg, seq, 1), F32)],
        compiler_params=_params("parallel", "parallel"),
        name="dilated_attention",
    )(qk, qk, v, mask)


def _dil_mix_kernel(o0_ref, o1_ref, o2_ref, l0_ref, l1_ref, l2_ref, o_ref):
    l0, l1, l2 = l0_ref[...], l1_ref[...], l2_ref[...]
    mx = jnp.maximum(jnp.maximum(l0, l1), l2)
    e0, e1, e2 = jnp.exp(l0 - mx), jnp.exp(l1 - mx), jnp.exp(l2 - mx)
    z = e0 + e1 + e2
    w0, w1, w2 = e0 / z, e1 / z, e2 / z
    for h in range(DIL_HEADS_PER_GROUP):
        sl = slice(h * HEAD_DIM, (h + 1) * HEAD_DIM)
        hs = slice(h, h + 1)
        o_ref[:, sl] = (w0[:, hs] * o0_ref[:, sl] + w1[:, hs] * o1_ref[:, sl]
                        + w2[:, hs] * o2_ref[:, sl]).astype(o_ref.dtype)


def _dil_mix(outs, lses, tm=512):
    t = outs[0].shape[0]
    ospec = pl.BlockSpec((tm, DIL_OUT), lambda i: (i, 0))
    lspec = pl.BlockSpec((tm, DIL_HEADS_PER_GROUP), lambda i: (i, 0))
    return pl.pallas_call(
        _dil_mix_kernel,
        grid=(t // tm,),
        in_specs=[ospec] * 3 + [lspec] * 3,
        out_specs=ospec,
        out_shape=jax.ShapeDtypeStruct((t, DIL_OUT), BF16),
        compiler_params=_params("parallel"),
        name="dilated_mix",
    )(*outs, *lses)


def _dilated_attention(qk, v, batch, seq):
    hg = DIL_HEADS_PER_GROUP
    outs, lses = [], []
    for g, (window, dil) in enumerate(DIL_GROUPS):
        assert window // (2 * dil) == DIL_RADIUS and seq % dil == 0
        o, lse = _dil_group(qk, v, g, dil, batch, seq)
        outs.append(o)
        lses.append(lse.reshape(batch, hg, seq).transpose(0, 2, 1).reshape(batch * seq, hg))
    return _dil_mix(outs, lses)


def _mla_q_kernel(a_ref, w_ref, g_ref, cos_ref, sin_ref, o_ref):
    a = a_ref[...]
    gain = g_ref[...]
    cos_t, sin_t = cos_ref[...], sin_ref[...]
    acc = _dot(a, w_ref[0])
    for h in range(MLA_HEADS):
        nxt = _dot(a, w_ref[h + 1]) if h + 1 < MLA_HEADS else None
        ms = jnp.sum(acc * acc, axis=-1, keepdims=True) * (1.0 / MLA_QK)
        y = acc * lax.rsqrt(ms + RMS_EPS) * gain
        o_ref[h, :, :MLA_NOPE] = y[:, :MLA_NOPE].astype(o_ref.dtype)
        o_ref[h, :, MLA_NOPE:] = _rotate_pairs(y[:, MLA_NOPE:], cos_t, sin_t,
                                               MLA_ROPE // 2).astype(o_ref.dtype)
        acc = nxt


def _mla_q_proj(cq_n, w_uq_h, gain, cos_t, sin_t, seq, tm=512):
    t, rank = cq_n.shape
    per = seq // tm
    return pl.pallas_call(
        _mla_q_kernel,
        grid=(t // tm,),
        in_specs=[pl.BlockSpec((tm, rank), lambda i: (i, 0)),
                  pl.BlockSpec((MLA_HEADS, rank, MLA_QK_PAD), lambda i: (0, 0, 0)),
                  pl.BlockSpec((1, MLA_QK_PAD), lambda i: (0, 0)),
                  pl.BlockSpec((tm, LANES), lambda i: (i % per, 0)),
                  pl.BlockSpec((tm, LANES), lambda i: (i % per, 0))],
        out_specs=pl.BlockSpec((MLA_HEADS, tm, MLA_QK_PAD), lambda i: (0, i, 0)),
        out_shape=jax.ShapeDtypeStruct((MLA_HEADS, t, MLA_QK_PAD), BF16),
        compiler_params=_params("parallel"),
        name="mla_q_proj",
    )(cq_n, w_uq_h, gain, cos_t, sin_t)


def _mla_kv_kernel(a_ref, w_ref, kr_ref, g0_ref, g1_ref, cos_ref, sin_ref, k_ref, v_ref):
    a = a_ref[...]
    kr = kr_ref[...]
    kr_ss = jnp.sum(kr * kr, axis=-1, keepdims=True)
    g0, g1 = g0_ref[...], g1_ref[...]
    cos_t, sin_t = cos_ref[...], sin_ref[...]
    ones = jnp.ones((a.shape[0], MLA_V), v_ref.dtype)
    kr_rot = _rotate_pairs(kr * g1, cos_t, sin_t, MLA_ROPE // 2)
    acc = _dot(a, w_ref[0])
    for h in range(MLA_HEADS):
        nxt = _dot(a, w_ref[h + 1]) if h + 1 < MLA_HEADS else None
        kn = acc[:, :MLA_NOPE]
        ms = (jnp.sum(kn * kn, axis=-1, keepdims=True) + kr_ss) * (1.0 / MLA_QK)
        inv = lax.rsqrt(ms + RMS_EPS)
        k_ref[h, :, :MLA_NOPE] = (kn * inv * g0).astype(k_ref.dtype)
        k_ref[h, :, MLA_NOPE:] = (kr_rot * inv).astype(k_ref.dtype)
        v_ref[h, :, :MLA_V] = acc[:, MLA_NOPE:].astype(v_ref.dtype)
        v_ref[h, :, MLA_V:] = ones
        acc = nxt


def _mla_kv_proj(ckv_n, w_ukv_h, k_r, g0, g1, cos_t, sin_t, seq, tm=1024):
    t, rank = ckv_n.shape
    per = seq // tm
    return pl.pallas_call(
        _mla_kv_kernel,
        grid=(t // tm,),
        in_specs=[pl.BlockSpec((tm, rank), lambda i: (i, 0)),
                  pl.BlockSpec((MLA_HEADS, rank, MLA_NOPE + MLA_V), lambda i: (0, 0, 0)),
                  pl.BlockSpec((tm, LANES), lambda i: (i, 0)),
                  pl.BlockSpec((1, LANES), lambda i: (0, 0)),
                  pl.BlockSpec((1, LANES), lambda i: (0, 0)),
                  pl.BlockSpec((tm, LANES), lambda i: (i % per, 0)),
                  pl.BlockSpec((tm, LANES), lambda i: (i % per, 0))],
        out_specs=[pl.BlockSpec((MLA_HEADS, tm, MLA_QK_PAD), lambda i: (0, i, 0)),
                   pl.BlockSpec((MLA_HEADS, tm, 2 * MLA_V), lambda i: (0, i, 0))],
        out_shape=[jax.ShapeDtypeStruct((MLA_HEADS, t, MLA_QK_PAD), BF16),
                   jax.ShapeDtypeStruct((MLA_HEADS, t, 2 * MLA_V), BF16)],
        compiler_params=_params("parallel"),
        name="mla_kv_proj",
    )(ckv_n, w_ukv_h, k_r, g0, g1, cos_t, sin_t)


MLA_KV_CHUNKS = 8
MLA_QK_AHEAD = 1


def _mla_attn_kernel(q_ref, k_ref, v_ref, o_ref):
    q = q_ref[...]
    tk = k_ref.shape[0] // MLA_KV_CHUNKS
    m = acc = None
    ahead = [_dot_nt(q, k_ref[i * tk:(i + 1) * tk, :]) for i in range(MLA_QK_AHEAD)]
    for c in range(MLA_KV_CHUNKS):
        s = ahead.pop(0)
        nxt = c + MLA_QK_AHEAD
        if nxt < MLA_KV_CHUNKS:
            ahead.append(_dot_nt(q, k_ref[nxt * tk:(nxt + 1) * tk, :]))
        m_c = jnp.max(s, axis=-1, keepdims=True)
        if c == 0:
            m = m_c
            acc = _dot(jnp.exp2(s - m).astype(BF16), v_ref[:tk, :])
        else:
            m_new = jnp.maximum(m, m_c)
            acc = jnp.exp2(m - m_new) * acc + _dot(jnp.exp2(s - m_new).astype(BF16),
                                                   v_ref[c * tk:(c + 1) * tk, :])
            m = m_new
    o_ref[...] = (acc[:, :MLA_V] / acc[:, MLA_V:]).astype(o_ref.dtype)


def _mla_attention(q, k, v, batch, seq, tq=1024):
    t = q.shape[1]
    nq = seq // tq
    return pl.pallas_call(
        _mla_attn_kernel,
        grid=(batch, MLA_HEADS, nq),
        in_specs=[pl.BlockSpec((None, tq, MLA_QK_PAD), lambda b, h, i: (h, b * nq + i, 0)),
                  pl.BlockSpec((None, seq, MLA_QK_PAD), lambda b, h, i: (h, b, 0)),
                  pl.BlockSpec((None, seq, 2 * MLA_V), lambda b, h, i: (h, b, 0))],
        out_specs=pl.BlockSpec((tq, MLA_V), lambda b, h, i: (b * nq + i, h)),
        out_shape=jax.ShapeDtypeStruct((t, MLA_HEADS * MLA_V), BF16),
        compiler_params=_params("parallel", "parallel", "arbitrary"),
        name="mla_attention",
    )(q, k, v)


def _merge_kernel(ona_ref, odl_ref, omla_ref, wna_ref, wdl_ref, wmla_ref, g0_ref, g1_ref, g2_ref, o_ref):
    ona, odl, omla = ona_ref[...], odl_ref[...], omla_ref[...]
    for c in range(o_ref.shape[1] // MXU_COLS):
        cols = slice(c * MXU_COLS, (c + 1) * MXU_COLS)
        acc = g0_ref[:, cols].astype(F32) * _dot(ona, wna_ref[:, cols])
        acc = acc + g1_ref[:, cols].astype(F32) * _dot(odl, wdl_ref[:, cols])
        acc = acc + g2_ref[:, cols].astype(F32) * _dot(omla, wmla_ref[:, cols])
        o_ref[:, cols] = acc.astype(o_ref.dtype)


def _merge(o_na, o_dl, o_mla, w_na, w_dl, w_mla, gates, tm=1024, tn=512):
    t = o_na.shape[0]
    d = w_na.shape[1]
    nj = d // tn

    def act(a):
        return pl.BlockSpec((tm, a.shape[1]), lambda i, j: (i, 0))

    def wgt(w):
        return pl.BlockSpec((w.shape[0], tn), lambda i, j: (0, j))

    def gate(idx):
        return pl.BlockSpec((tm, tn), lambda i, j: (i, idx * nj + j))

    return pl.pallas_call(
        _merge_kernel,
        grid=(t // tm, nj),
        in_specs=[act(o_na), act(o_dl), act(o_mla), wgt(w_na), wgt(w_dl), wgt(w_mla),
                  gate(0), gate(1), gate(2)],
        out_specs=pl.BlockSpec((tm, tn), lambda i, j: (i, j)),
        out_shape=jax.ShapeDtypeStruct((t, d), BF16),
        compiler_params=_params("parallel", "parallel"),
        name="branch_merge",
    )(o_na, o_dl, o_mla, w_na, w_dl, w_mla, gates, gates, gates)


def _swiglu_pieces(h, wg_ref, wu_ref, o_ref):
    for c in range(o_ref.shape[1] // MXU_COLS):
        cols = slice(c * MXU_COLS, (c + 1) * MXU_COLS)
        g = _dot(h, wg_ref[:, cols].astype(BF16))
        u = _dot(h, wu_ref[:, cols].astype(BF16))
        o_ref[:, cols] = (g * jax.nn.sigmoid(g) * u).astype(o_ref.dtype)


def _gu_kernel(h_ref, wg_ref, wu_ref, o_ref):
    _swiglu_pieces(h_ref[...], wg_ref, wu_ref, o_ref)


def _swiglu_up(h, w_gu, tm=1024, tn=512):
    t, d = h.shape
    ff = w_gu.shape[1] // 2
    nj = ff // tn
    return pl.pallas_call(
        _gu_kernel,
        grid=(t // tm, nj),
        in_specs=[pl.BlockSpec((tm, d), lambda i, j: (i, 0)),
                  pl.BlockSpec((d, tn), lambda i, j: (0, j)),
                  pl.BlockSpec((d, tn), lambda i, j: (0, nj + j))],
        out_specs=pl.BlockSpec((tm, tn), lambda i, j: (i, j)),
        out_shape=jax.ShapeDtypeStruct((t, ff), BF16),
        compiler_params=_params("parallel", "parallel"),
        name="swiglu_up",
    )(h, w_gu, w_gu)


TOP_K = 2
MOE_TILE_ROWS = 1024


def _moe_routing(route, tokens, tm):
    e_flat = jnp.concatenate([route[:, N_EXPERTS], route[:, N_EXPERTS + 1]]).astype(jnp.int32)
    onehot = (e_flat[:, None] == jnp.arange(N_EXPERTS, dtype=jnp.int32)[None, :]).astype(jnp.int32)
    csum = jnp.cumsum(onehot, axis=0)
    rank = jnp.sum((csum - onehot) * onehot, axis=1)
    padded = ((csum[-1] + tm - 1) // tm) * tm
    ends = jnp.cumsum(padded)
    slot = jnp.sum(onehot * (ends - padded)[None, :], axis=1) + rank
    n_tiles = (TOP_K * tokens) // tm + N_EXPERTS
    tile_start = jnp.arange(n_tiles, dtype=jnp.int32) * tm
    tile_expert = jnp.minimum(jnp.sum(tile_start[:, None] >= ends[None, :], axis=1), N_EXPERTS - 1)
    te = jnp.concatenate([tile_expert, ends[-1:] // tm]).astype(jnp.int32)
    return slot.astype(jnp.int32), te


def _dispatch_kernel(slot_ref, x_ref, g_ref, sc_ref, sh_ref, dst_in_ref, dst_ref, h_ref, sem, *, rows, tokens):
    del dst_in_ref
    base = pl.program_id(0) * rows
    h_ref[...] = _modulated_norm(x_ref[...], g_ref[...], sc_ref[...], sh_ref[...])

    def row_copy(r, choice):
        slot = slot_ref[choice * tokens + base + r]
        return pltpu.make_async_copy(h_ref.at[pl.ds(r, 1), :], dst_ref.at[pl.ds(slot, 1), :], sem)

    def start(r, carry):
        row_copy(r, 0).start()
        row_copy(r, 1).start()
        return carry

    def wait(r, carry):
        row_copy(r, 0).wait()
        row_copy(r, 1).wait()
        return carry

    lax.fori_loop(0, rows, start, 0, unroll=8)
    lax.fori_loop(0, rows, wait, 0, unroll=8)


def _moe_dispatch(x, g, sc, sh, slot, seq, n_rows, rows=256):
    t, d = x.shape
    per = seq // rows
    return pl.pallas_call(
        functools.partial(_dispatch_kernel, rows=rows, tokens=t),
        grid_spec=pltpu.PrefetchScalarGridSpec(
            num_scalar_prefetch=1,
            grid=(t // rows,),
            in_specs=[pl.BlockSpec((rows, d), lambda i, s: (i, 0)),
                      pl.BlockSpec((1, d), lambda i, s: (0, 0)),
                      pl.BlockSpec((None, 1, d), lambda i, s: (i // per, 0, 0)),
                      pl.BlockSpec((None, 1, d), lambda i, s: (i // per, 0, 0)),
                      pl.BlockSpec(memory_space=pl.ANY)],
            out_specs=pl.BlockSpec(memory_space=pl.ANY),
            scratch_shapes=[pltpu.VMEM((rows, d), F32), pltpu.SemaphoreType.DMA(())]),
        out_shape=jax.ShapeDtypeStruct((n_rows, d), F32),
        input_output_aliases={5: 0},
        compiler_params=_params("arbitrary"),
        name="moe_dispatch",
    )(slot, x, g.reshape(1, d), sc, sh, jnp.zeros((n_rows, d), F32))


def _moe_up_kernel(te_ref, xs_ref, wg_ref, wu_ref, o_ref, hb_ref, *, n_tiles):
    used = pl.program_id(0) < te_ref[n_tiles]

    @pl.when(used & (pl.program_id(1) == 0))
    def _():
        hb_ref[...] = xs_ref[...].astype(BF16)

    @pl.when(used)
    def _():
        _swiglu_pieces(hb_ref[...], wg_ref, wu_ref, o_ref)

    @pl.when(jnp.logical_not(used))
    def _():
        o_ref[...] = jnp.zeros(o_ref.shape, o_ref.dtype)


def _moe_up(xs, w_gu, te, tn=512):
    n_rows, d = xs.shape
    ff = w_gu.shape[2] // 2
    nj = ff // tn
    n_tiles = te.shape[0] - 1
    tm = n_rows // n_tiles

    def last_used(i, te_ref):
        return jnp.minimum(i, te_ref[n_tiles] - 1)

    return pl.pallas_call(
        functools.partial(_moe_up_kernel, n_tiles=n_tiles),
        grid_spec=pltpu.PrefetchScalarGridSpec(
            num_scalar_prefetch=1,
            grid=(n_tiles, nj),
            in_specs=[pl.BlockSpec((tm,d), lambda i, j, te_ref: (last_used(i, te_ref), 0)),
                      pl.BlockSpec((None, d, tn), lambda i, j, te_ref: (te_ref[i], 0, j)),
                      pl.BlockSpec((None, d, tn), lambda i, j, te_ref: (te_ref[i], 0, nj + j))],
            out_specs=pl.BlockSpec((tm,tn), lambda i, j, te_ref: (i, j)),
            scratch_shapes=[pltpu.VMEM((tm, d), BF16)]),
        out_shape=jax.ShapeDtypeStruct((n_rows, ff), BF16),
        compiler_params=_params("parallel", "arbitrary"),
        name="moe_up",
    )(te, xs, w_gu, w_gu)


def _moe_down_kernel(te_ref, a_ref, w_ref, o_ref, *, n_tiles):
    used = pl.program_id(0) < te_ref[n_tiles]

    @pl.when(used)
    def _():
        o_ref[...] = _dot(a_ref[...], w_ref[...])

    @pl.when(jnp.logical_not(used))
    def _():
        o_ref[...] = jnp.zeros(o_ref.shape, o_ref.dtype)


def _moe_down(act, w_down, te, tn=256):
    n_rows, ff = act.shape
    d = w_down.shape[2]
    n_tiles = te.shape[0] - 1
    tm = n_rows // n_tiles

    def last_used(i, te_ref):
        return jnp.minimum(i, te_ref[n_tiles] - 1)

    return pl.pallas_call(
        functools.partial(_moe_down_kernel, n_tiles=n_tiles),
        grid_spec=pltpu.PrefetchScalarGridSpec(
            num_scalar_prefetch=1,
            grid=(n_tiles, d // tn),
            in_specs=[pl.BlockSpec((tm,ff), lambda i, j, te_ref: (last_used(i, te_ref), 0)),
                      pl.BlockSpec((None, ff, tn), lambda i, j, te_ref: (te_ref[i], 0, j))],
            out_specs=pl.BlockSpec((tm,tn), lambda i, j, te_ref: (i, j))),
        out_shape=jax.ShapeDtypeStruct((n_rows, d), F32),
        compiler_params=_params("parallel", "arbitrary"),
        name="moe_down",
    )(te, act, w_down)


def _combine_kernel(slot_ref, x_ref, gate_ref, w_ref, ys_ref, o_ref, buf_ref, sem, *, rows, tokens):
    base = pl.program_id(0) * rows

    def row_copy(r, choice):
        slot = slot_ref[choice * tokens + base + r]
        return pltpu.make_async_copy(ys_ref.at[pl.ds(slot, 1), :], buf_ref.at[choice, pl.ds(r, 1), :], sem)

    def start(r, carry):
        row_copy(r, 0).start()
        row_copy(r, 1).start()
        return carry

    def wait(r, carry):
        row_copy(r, 0).wait()
        row_copy(r, 1).wait()
        return carry

    lax.fori_loop(0, rows, start, 0, unroll=8)
    lax.fori_loop(0, rows, wait, 0, unroll=8)
    w = w_ref[...]
    y = w[:, 0:1] * buf_ref[0] + w[:, 1:2] * buf_ref[1]
    o_ref[...] = x_ref[...] + gate_ref[...] * y


def _moe_combine(x, gate, w12, ys, slot, seq, rows=256):
    t, d = x.shape
    per = seq // rows
    return pl.pallas_call(
        functools.partial(_combine_kernel, rows=rows, tokens=t),
        grid_spec=pltpu.PrefetchScalarGridSpec(
            num_scalar_prefetch=1,
            grid=(t // rows,),
            in_specs=[pl.BlockSpec((rows, d), lambda i, s: (i, 0)),
                      pl.BlockSpec((None, 1, d), lambda i, s: (i // per, 0, 0)),
                      pl.BlockSpec((rows, TOP_K), lambda i, s: (i, 0)),
                      pl.BlockSpec(memory_space=pl.ANY)],
            out_specs=pl.BlockSpec((rows, d), lambda i, s: (i, 0)),
            scratch_shapes=[pltpu.VMEM((TOP_K, rows, d), F32), pltpu.SemaphoreType.DMA(())]),
        out_shape=jax.ShapeDtypeStruct((t, d), F32),
        compiler_params=_params("arbitrary"),
        name="moe_combine",
    )(slot, x, gate, w12, ys)


def _router_kernel(x_ref, g_ref, sc_ref, sh_ref, w_ref, o_ref):
    h = _modulated_norm(x_ref[...], g_ref[...], sc_ref[...], sh_ref[...])
    logits = jnp.dot(h, w_ref[...], preferred_element_type=F32, precision=lax.Precision.HIGHEST)
    lane = lax.broadcasted_iota(jnp.int32, logits.shape, 1).astype(F32)
    lg = jnp.where(lane < N_EXPERTS, logits, NEG_INF)
    m1 = jnp.max(lg, axis=-1, keepdims=True)
    i1 = jnp.min(jnp.where(lg == m1, lane, float(LANES)), axis=-1, keepdims=True)
    lg2 = jnp.where(lane == i1, NEG_INF, lg)
    m2 = jnp.max(lg2, axis=-1, keepdims=True)
    i2 = jnp.min(jnp.where(lg2 == m2, lane, float(LANES)), axis=-1, keepdims=True)
    e2 = jnp.exp(m2 - m1)
    z = 1.0 + e2
    out = jnp.where(lane == N_EXPERTS, i1, 0.0) + jnp.where(lane == N_EXPERTS + 1, i2, 0.0)
    out = out + jnp.where(lane == N_EXPERTS + 2, 1.0 / z, 0.0) + jnp.where(lane == N_EXPERTS + 3, e2 / z, 0.0)
    o_ref[...] = out


def _router(x, g, sc, sh, w_router_pad, seq, tm=512):
    t, d = x.shape
    per = seq // tm
    return pl.pallas_call(
        _router_kernel,
        grid=(t // tm,),
        in_specs=[pl.BlockSpec((tm, d), lambda i: (i, 0)),
                  pl.BlockSpec((1, d), lambda i: (0, 0)),
                  pl.BlockSpec((None, 1, d), lambda i: (i // per, 0, 0)),
                  pl.BlockSpec((None, 1, d), lambda i: (i // per, 0, 0)),
                  pl.BlockSpec((d, LANES), lambda i: (0, 0))],
        out_specs=pl.BlockSpec((tm, LANES), lambda i: (i, 0)),
        out_shape=jax.ShapeDtypeStruct((t, LANES), F32),
        compiler_params=_params("parallel"),
        name="router",
    )(x, g.reshape(1, d), sc, sh, w_router_pad)


MM_TM = 1024
MM_TN = 1024


def _col_tile(n):
    return MM_TN if n % MM_TN == 0 else MM_TN // 2


def _residual_matmul(a, w, x, gate, seq, name, tm=MM_TM, tn=None):
    per = seq // tm
    tn = _col_tile(w.shape[1]) if tn is None else tn
    return _matmul(a, w, tm=tm, tn=tn, out_dtype=F32, epilogue=_ep_residual, split=MXU_COLS,
                   extras=[(x, (tm, tn), lambda i, j: (i, j)),
                           (gate, (None, 1, tn), lambda i, j: (i // per, 0, j))],
                   name=name)


def _token_mixer(h, x, gate, lw, batch, seq, tables):
    d = h.shape[1]
    w_in = lw["w_in"]
    tm = MM_TM
    per = seq // tm
    cos_d, sin_d, cos_m, sin_m = tables

    def cols(lo, hi):
        return w_in[:, lo:hi].astype(BF16)

    def tile_gain(gq, gk, heads):
        gq = gq * (LOG2E / math.sqrt(HEAD_DIM))
        return jnp.concatenate([jnp.tile(gq, heads), jnp.tile(gk, heads)]).reshape(1, -1).astype(F32)

    o = 0
    tn = _col_tile(2 * NA_W)
    na_qk = _matmul(h, cols(o, o + 2 * NA_W), tm=tm, tn=tn, out_dtype=BF16, epilogue=_ep_headnorm,
                    split=MXU_COLS,
                    extras=[(tile_gain(lw["na_q_norm"], lw["na_k_norm"], NA_HEADS), (1, tn),
                             lambda i, j: (0, j))], name="na_qk_proj")
    o += 2 * NA_W
    na_v = _matmul(h, cols(o, o + NA_W), tm=tm, tn=_col_tile(NA_W), out_dtype=BF16, split=MXU_COLS, name="na_v_proj")
    o += NA_W
    tn = _col_tile(2 * DIL_W)
    dl_qk = _matmul(h, cols(o, o + 2 * DIL_W), tm=tm, tn=tn, out_dtype=F32, epilogue=_ep_headnorm_rope,
                    split=MXU_COLS,
                    extras=[(tile_gain(lw["dil_q_norm"], lw["dil_k_norm"], DIL_HEADS), (1, tn),
                             lambda i, j: (0, j)),
                            (cos_d, (tm, LANES), lambda i, j: (i % per, 0)),
                            (sin_d, (tm, LANES), lambda i, j: (i % per, 0)),
                            (_pair_swap_matrix(MXU_COLS, ROT_DIM // 2), (MXU_COLS, MXU_COLS),
                             lambda i, j: (0, 0))], name="dil_qk_proj")
    o += 2 * DIL_W
    dl_v = _matmul(h, cols(o, o + DIL_W), tm=tm, tn=_col_tile(DIL_W), out_dtype=F32, split=MXU_COLS, name="dil_v_proj")
    o += DIL_W
    cq_n = _matmul(h, cols(o, o + MLA_Q_RANK), tm=tm, tn=MLA_Q_RANK, out_dtype=BF16, epilogue=_ep_rownorm,
                   extras=[(lw["mla_q_a_norm"].reshape(1, -1), (1, MLA_Q_RANK), lambda i, j: (0, 0))],
                   name="mla_cq_proj")
    o += MLA_Q_RANK
    ckv_n = _matmul(h, cols(o, o + MLA_KV_RANK), tm=tm, tn=MLA_KV_RANK, out_dtype=BF16, epilogue=_ep_rownorm,
                    extras=[(lw["mla_kv_a_norm"].reshape(1, -1), (1, MLA_KV_RANK), lambda i, j: (0, 0))],
                    name="mla_ckv_proj")
    o += MLA_KV_RANK
    w_kr = jnp.pad(w_in[:, o:o + MLA_ROPE], ((0, 0), (0, LANES - MLA_ROPE))).astype(BF16)
    k_r = _matmul(h, w_kr, tm=tm, tn=LANES, out_dtype=F32, name="mla_kr_proj")
    o += MLA_ROPE
    gates = _matmul(h, cols(o, o + 3 * d), tm=tm, tn=_col_tile(3 * d), out_dtype=BF16, epilogue=_ep_sigmoid,
                    split=MXU_COLS, name="gate_proj")

    o_na = _na_attention(na_qk, na_v, lw["na_rpb"], batch, seq)
    o_dl = _dilated_attention(dl_qk, dl_v, batch, seq)
    w_uq = lw["mla_w_uq"].reshape(MLA_Q_RANK, MLA_HEADS, MLA_QK).transpose(1, 0, 2)
    w_uq = jnp.pad(w_uq, ((0, 0), (0, 0), (0, MLA_QK_PAD - MLA_QK))).astype(BF16)
    w_ukv = lw["mla_w_ukv"].reshape(MLA_KV_RANK, MLA_HEADS, MLA_NOPE + MLA_V).transpose(1, 0, 2).astype(BF16)
    gq = jnp.pad(lw["mla_q_norm"] * (LOG2E / math.sqrt(MLA_QK)), (0, MLA_QK_PAD - MLA_QK)).reshape(1, MLA_QK_PAD)
    gk = lw["mla_k_norm"]
    gk0 = gk[:MLA_NOPE].reshape(1, LANES)
    gk1 = jnp.pad(gk[MLA_NOPE:], (0, LANES - MLA_ROPE)).reshape(1, LANES)
    q_m = _mla_q_proj(cq_n, w_uq, gq, cos_m, sin_m, seq)
    k_m, v_m = _mla_kv_proj(ckv_n, w_ukv, k_r, gk0, gk1, cos_m, sin_m, seq)
    o_mla = _mla_attention(q_m, k_m, v_m, batch, seq)

    merged = _merge(o_na, o_dl, o_mla, lw["w_branch_na"].astype(BF16), lw["w_branch_dil"].astype(BF16),
                    lw["w_branch_mla"].astype(BF16), gates)
    return _residual_matmul(merged, lw["w_out"].astype(BF16), x, gate, seq, "out_proj")


def kernel(x, c, w_ada, b_ada, norm_mix, norm_ffn, w_in, na_q_norm, na_k_norm, na_rpb, dil_q_norm, dil_k_norm, mla_q_a_norm, mla_w_uq, mla_kv_a_norm, mla_w_ukv, mla_q_norm, mla_k_norm, w_branch_na, w_branch_dil, w_branch_mla, w_out, ffn_w_gu, ffn_w_down, moe_router, moe_w_gu, moe_w_down):
    batch, seq, d = x.shape
    depth = w_ada.shape[0]
    assert seq % MM_TM == 0 and seq % GRID_W == 0
    t = batch * seq
    xf = x.reshape(t, d)

    c_pad = jnp.pad(c, ((0, (-batch) % 8), (0, 0)))
    mod = _adaln(c_pad, w_ada, b_ada)[:, :batch].reshape(depth, batch, 6, 1, d)
    tables = _rope_tables(seq, ROT_DIM, 1.0) + _rope_tables(seq, MLA_ROPE, 1.0)

    for l in range(depth):
        sh1, sc1, g1, sh2, sc2, g2 = [mod[l, :, i] for i in range(6)]
        lw = dict(w_in=w_in[l], na_q_norm=na_q_norm[l], na_k_norm=na_k_norm[l], na_rpb=na_rpb[l],
                  dil_q_norm=dil_q_norm[l], dil_k_norm=dil_k_norm[l], mla_q_a_norm=mla_q_a_norm[l],
                  mla_w_uq=mla_w_uq[l], mla_kv_a_norm=mla_kv_a_norm[l], mla_w_ukv=mla_w_ukv[l],
                  mla_q_norm=mla_q_norm[l], mla_k_norm=mla_k_norm[l], w_branch_na=w_branch_na[l],
                  w_branch_dil=w_branch_dil[l], w_branch_mla=w_branch_mla[l], w_out=w_out[l])
        h = _normmod(xf, norm_mix[l], sc1, sh1, seq)
        xf = _token_mixer(h, xf, g1, lw, batch, seq, tables)

        if l % 2 == 0:
            h = _normmod(xf, norm_ffn[l], sc2, sh2, seq)
            act = _swiglu_up(h, ffn_w_gu[l // 2].astype(BF16))
            xf = _residual_matmul(act, ffn_w_down[l // 2].astype(BF16), xf, g2, seq, "down_proj", tm=1024, tn=256)
        else:
            w_r = jnp.pad(moe_router[l // 2], ((0, 0), (0, LANES - N_EXPERTS)))
            route = _router(xf, norm_ffn[l], sc2, sh2, w_r, seq)
            slot, te = _moe_routing(route, t, MOE_TILE_ROWS)
            n_rows = (te.shape[0] - 1) * MOE_TILE_ROWS
            xs = _moe_dispatch(xf, norm_ffn[l], sc2, sh2, slot, seq, n_rows)
            act = _moe_up(xs, moe_w_gu[l // 2], te)
            ys = _moe_down(act, moe_w_down[l // 2].astype(BF16), te)
            xf = _moe_combine(xf, g2, route[:, N_EXPERTS + 2:N_EXPERTS + 4], ys, slot, seq)
    return xf.reshape(batch, seq, d)
```

```python
import functools
import math

import numpy as np
import jax
import jax.numpy as jnp
from jax import lax
from jax.experimental import pallas as pl
from jax.experimental.pallas import tpu as pltpu

GRID_W = 64
HEAD_DIM = 128
ROPE_THETA = 500000.0
ROT_DIM = HEAD_DIM // 4
RMS_EPS = 1e-6
NEG_INF = -1e30
NA_HEADS = 8
NA_KH = 8
NA_KW = 16
DIL_GROUPS = ((128, 1), (512, 4), (2048, 16))
DIL_HEADS_PER_GROUP = 4
DIL_HEADS = DIL_HEADS_PER_GROUP * len(DIL_GROUPS)
DIL_RADIUS = 64
MLA_HEADS = 8
MLA_NOPE = 128
MLA_ROPE = 64
MLA_V = 128
MLA_Q_RANK = 768
MLA_KV_RANK = 512
MLA_QK = MLA_NOPE + MLA_ROPE
MLA_QK_PAD = 256
N_EXPERTS = 8
LANES = 128
MXU_COLS = 256

NA_W = NA_HEADS * HEAD_DIM
DIL_W = DIL_HEADS * HEAD_DIM
DIL_OUT = DIL_HEADS_PER_GROUP * HEAD_DIM

VMEM_LIMIT_BYTES = 48 * 1024 * 1024

F32 = jnp.float32
BF16 = jnp.bfloat16
LOG2E = math.log2(math.e)
LN2 = math.log(2.0)


def _params(*sem):
    return pltpu.CompilerParams(dimension_semantics=sem, vmem_limit_bytes=VMEM_LIMIT_BYTES)


def _dot(a, b):
    return jnp.dot(a, b, preferred_element_type=F32)


def _dot_nt(a, b):
    return lax.dot_general(a, b, (((1,), (1,)), ((), ())), preferred_element_type=F32)


def _adaln_kernel(c_ref, w_ref, b_ref, o_ref):
    o_ref[...] = _dot(c_ref[...], w_ref[...]) + b_ref[...]


def _adaln(c_pad, w_ada, b_ada):
    depth, d, n = w_ada.shape
    tn = 1024
    return pl.pallas_call(
        _adaln_kernel,
        grid=(depth, n // tn),
        in_specs=[pl.BlockSpec(c_pad.shape, lambda l, j: (0, 0)),
                  pl.BlockSpec((None, d, tn), lambda l, j: (l, 0, j)),
                  pl.BlockSpec((None, 1, tn), lambda l, j: (l, 0, j))],
        out_specs=pl.BlockSpec((None, c_pad.shape[0], tn), lambda l, j: (l, 0, j)),
        out_shape=jax.ShapeDtypeStruct((depth, c_pad.shape[0], n), F32),
        compiler_params=_params("parallel", "parallel"),
        name="adaln",
    )(c_pad, w_ada, b_ada.reshape(depth, 1, n))


def _modulated_norm(x, g, sc, sh):
    ms = jnp.mean(x * x, axis=-1, keepdims=True)
    return (x * lax.rsqrt(ms + RMS_EPS) * g) * (1.0 + sc) + sh


def _normmod_kernel(x_ref, g_ref, sc_ref, sh_ref, o_ref):
    o_ref[...] = _modulated_norm(x_ref[...], g_ref[...], sc_ref[...], sh_ref[...]).astype(o_ref.dtype)


def _normmod(x, g, sc, sh, seq, tm=512):
    t, d = x.shape
    per = seq // tm
    return pl.pallas_call(
        _normmod_kernel,
        grid=(t // tm,),
        in_specs=[pl.BlockSpec((tm, d), lambda i: (i, 0)),
                  pl.BlockSpec((1, d), lambda i: (0, 0)),
                  pl.BlockSpec((None, 1, d), lambda i: (i // per, 0, 0)),
                  pl.BlockSpec((None, 1, d), lambda i: (i // per, 0, 0))],
        out_specs=pl.BlockSpec((tm, d), lambda i: (i, 0)),
        out_shape=jax.ShapeDtypeStruct((t, d), BF16),
        compiler_params=_params("parallel"),
        name="normmod",
    )(x, g.reshape(1, d), sc, sh)


def _mm_kernel(*refs, n_extra, epilogue, split):
    a_ref, b_ref = refs[0], refs[1]
    extras = refs[2:2 + n_extra]
    o_ref = refs[2 + n_extra]
    a = a_ref[...]
    n_pieces = b_ref.shape[1] // split

    def piece(c):
        return _dot(a, b_ref[:, c * split:(c + 1) * split].astype(BF16))

    acc = piece(0)
    for c in range(n_pieces):
        nxt = piece(c + 1) if c + 1 < n_pieces else None
        epilogue(acc, extras, o_ref, slice(c * split, (c + 1) * split))
        acc = nxt


def _ep_store(acc, extras, o_ref, cols):
    o_ref[:, cols] = acc.astype(o_ref.dtype)


def _weight_spec(w, layer, col0, tn):
    if w.ndim == 2:
        return pl.BlockSpec((w.shape[0], tn), lambda i, j: (0, j))
    assert col0 % tn == 0
    return pl.BlockSpec((None, w.shape[1], tn), lambda i, j: (layer, 0, col0 // tn + j))


def _matmul(a, b, *, tm, tn, out_dtype, epilogue=_ep_store, extras=(), split=None, name,
            layer=None, col0=0, n=None):
    m, kdim = a.shape
    n = b.shape[-1] if n is None else n
    split = tn if split is None else split
    assert m % tm == 0 and n % tn == 0 and tn % split == 0 and b.shape[-2] == kdim
    in_specs = [pl.BlockSpec((tm, kdim), lambda i, j: (i, 0)), _weight_spec(b, layer, col0, tn)]
    in_specs += [pl.BlockSpec(bs, im) for (_, bs, im) in extras]
    return pl.pallas_call(
        functools.partial(_mm_kernel, n_extra=len(extras), epilogue=epilogue, split=split),
        grid=(m // tm, n // tn),
        in_specs=in_specs,
        out_specs=pl.BlockSpec((tm, tn), lambda i, j: (i, j)),
        out_shape=jax.ShapeDtypeStruct((m, n), out_dtype),
        compiler_params=_params("parallel", "parallel"),
        name=name,
    )(a, b, *[e[0] for e in extras])


def _head_rms(blk, gain):
    ms = jnp.mean(blk * blk, axis=-1, keepdims=True)
    return blk * lax.rsqrt(ms + RMS_EPS) * gain


def _head_slices(cols):
    return [slice(c, c + HEAD_DIM) for c in range(cols.start, cols.stop, HEAD_DIM)]


def _ep_headnorm(acc, extras, o_ref, cols):
    for i, sl in enumerate(_head_slices(cols)):
        blk = acc[:, i * HEAD_DIM:(i + 1) * HEAD_DIM]
        o_ref[:, sl] = _head_rms(blk, extras[0][:, sl]).astype(o_ref.dtype)


def _rotate_pairs(y, cos_t, sin_t, half):
    lane = lax.broadcasted_iota(jnp.int32, y.shape, 1)
    swapped = jnp.where(lane < half, pltpu.roll(y, LANES - half, 1), pltpu.roll(y, half, 1))
    return y * cos_t + swapped * sin_t


def _pair_swap_matrix(width, half):
    p = np.zeros((width, width), np.float32)
    for base in range(0, width, HEAD_DIM):
        for i in range(half):
            p[base + i + half, base + i] = 1.0
            p[base + i, base + i + half] = 1.0
    return jnp.asarray(p, BF16)


def _ep_headnorm_rope(acc, extras, o_ref, cols):
    cos_t = extras[1][...]
    sin_t = extras[2][...]
    heads = _head_slices(cols)
    y = jnp.concatenate([_head_rms(acc[:, i * HEAD_DIM:(i + 1) * HEAD_DIM], extras[0][:, sl])
                         for i, sl in enumerate(heads)], axis=1)
    swapped = _dot(y.astype(BF16), extras[3][...])
    for i, sl in enumerate(heads):
        loc = slice(i * HEAD_DIM, (i + 1) * HEAD_DIM)
        o_ref[:, sl] = (y[:, loc] * cos_t + swapped[:, loc] * sin_t).astype(o_ref.dtype)


def _ep_rownorm(acc, extras, o_ref, cols):
    o_ref[:, cols] = _head_rms(acc, extras[0][:, cols]).astype(o_ref.dtype)


def _ep_sigmoid(acc, extras, o_ref, cols):
    o_ref[:, cols] = jax.nn.sigmoid(acc).astype(o_ref.dtype)


def _ep_residual(acc, extras, o_ref, cols):
    o_ref[:, cols] = extras[0][:, cols] + extras[1][:, cols] * acc


def _rope_tables(seq, rot, fill):
    half = rot // 2
    inv = ROPE_THETA ** (-jnp.arange(half, dtype=F32) * (2.0 / rot))
    ang = jnp.arange(seq, dtype=jnp.int32).astype(F32)[:, None] * inv[None, :]
    cos, sin = jnp.cos(ang), jnp.sin(ang)
    pad = LANES - rot
    cos_t = jnp.concatenate([cos, cos, jnp.full((seq, pad), fill, F32)], axis=1)
    sin_t = jnp.concatenate([-sin, sin, jnp.zeros((seq, pad), F32)], axis=1)
    return cos_t, sin_t


NA_RB = 4
NA_KROWS = NA_KH + NA_RB
NA_GROUP = 2


def _na_geometry(rows):
    kh = min(NA_KH, rows)
    assert rows % (NA_RB * NA_GROUP) == 0 and rows >= NA_KROWS
    patterns, ids, kstarts = [], [], []
    for r in range(0, rows, NA_RB):
        ks = int(np.clip(r - NA_KH // 2, 0, rows - NA_KROWS))
        pat = tuple((int(np.clip(r + a - NA_KH // 2, 0, rows - kh)) - ks, ks - (r + a)) for a in range(NA_RB))
        if pat not in patterns:
            patterns.append(pat)
        ids.append(patterns.index(pat))
        kstarts.append(ks)
    return patterns, np.array(ids + kstarts, np.int32)


def _na_bias_table(rpb, rows, patterns):
    h = rpb.shape[0]
    kh = min(NA_KH, rows)
    qc = np.arange(GRID_W)
    kc = np.arange(GRID_W)
    cstart = np.clip(qc - NA_KW // 2, 0, GRID_W - NA_KW)
    ok = (kc[None, :] >= cstart[:, None]) & (kc[None, :] < cstart[:, None] + NA_KW)
    dc = np.clip(kc[None, :] - qc[:, None] + NA_KW - 1, 0, 2 * NA_KW - 2)
    by_col = rpb.astype(F32)[:, :, dc] * LOG2E + jnp.where(ok, 0.0, NEG_INF).astype(F32)
    masked_row = 2 * NA_KH - 1
    by_col = jnp.concatenate([by_col, jnp.full((h, 1, GRID_W, GRID_W), NEG_INF, F32)], axis=1)
    kr = np.arange(NA_KROWS)
    dr = np.zeros((len(patterns), NA_RB, NA_KROWS), np.int32)
    for p, pat in enumerate(patterns):
        for a, (start, offset) in enumerate(pat):
            row_ok = (kr >= start) & (kr < start + kh)
            dr[p, a] = np.where(row_ok, kr + offset + NA_KH - 1, masked_row)
    tab = jnp.take(by_col, jnp.asarray(dr.reshape(-1)), axis=1)
    tab = tab.reshape(h, len(patterns), NA_RB, NA_KROWS, GRID_W, GRID_W)
    tab = tab.transpose(0, 1, 2, 4, 3, 5)
    return tab.reshape(h, len(patterns), NA_RB * GRID_W, NA_KROWS * GRID_W)


def _na_kernel(geo_ref, q_ref, k_ref, v_ref, b_ref, o_ref, *, n_blocks):
    qn = NA_RB * GRID_W
    kn = NA_KROWS * GRID_W
    ones = jnp.ones((kn, HEAD_DIM), BF16)

    def scores(blk):
        q0 = pl.multiple_of(blk * qn, qn)
        k0 = pl.multiple_of(geo_ref[n_blocks + blk] * GRID_W, GRID_W)
        s = _dot_nt(q_ref[pl.ds(q0, qn), :], k_ref[pl.ds(k0, kn), :]) + b_ref[geo_ref[blk]]
        return s, q0, k0

    def finish(s, q0, k0):
        v1 = jnp.concatenate([v_ref[pl.ds(k0, kn), :], ones], axis=1)
        m = jnp.max(s, axis=-1, keepdims=True)
        acc = _dot(jnp.exp2(s - m).astype(BF16), v1)
        o_ref[pl.ds(q0, qn), :] = (acc[:, :HEAD_DIM] / acc[:, HEAD_DIM:]).astype(o_ref.dtype)

    def body(i, carry):
        pending = [scores(i * NA_GROUP + j) for j in range(NA_GROUP)]
        for item in pending:
            finish(*item)
        return carry

    lax.fori_loop(0, n_blocks // NA_GROUP, body, 0)


def _na_attention(qk, v, rpb, batch, seq):
    rows = seq // GRID_W
    patterns, geo = _na_geometry(rows)
    bias = _na_bias_table(rpb, rows, patterns)
    t = qk.shape[0]
    blk = (seq, HEAD_DIM)
    return pl.pallas_call(
        functools.partial(_na_kernel, n_blocks=rows // NA_RB),
        grid_spec=pltpu.PrefetchScalarGridSpec(
            num_scalar_prefetch=1,
            grid=(batch, NA_HEADS),
            in_specs=[pl.BlockSpec(blk, lambda b, h, geo_ref: (b, h)),
                      pl.BlockSpec(blk, lambda b, h, geo_ref: (b, NA_HEADS + h)),
                      pl.BlockSpec(blk, lambda b, h, geo_ref: (b, h)),
                      pl.BlockSpec((None,) + bias.shape[1:], lambda b, h, geo_ref: (h, 0, 0, 0))],
            out_specs=pl.BlockSpec(blk, lambda b, h, geo_ref: (b, h))),
        out_shape=jax.ShapeDtypeStruct((t, NA_W), BF16),
        compiler_params=_params("parallel", "parallel"),
        name="na_attention",
    )(jnp.asarray(geo), qk, qk, v, bias)


DIL_QCHUNK = 2 * DIL_RADIUS
DIL_KWIN = 4 * DIL_RADIUS
DIL_GROUP = 4


def _dil_window_mask():
    r = np.arange(DIL_QCHUNK)[None, :, None]
    c = np.arange(DIL_KWIN)[None, None, :]
    off = (np.arange(3) * DIL_RADIUS)[:, None, None]
    return jnp.asarray(np.where(np.abs(c - r - off) <= DIL_RADIUS, 0.0, NEG_INF), F32)


def _dil_kernel(q_ref, k_ref, v_ref, mask_ref, o_ref, lse_ref, *, seq, dil):
    length = seq // dil
    n_chunk = length // DIL_QCHUNK
    ones = jnp.ones((DIL_KWIN, HEAD_DIM), BF16)

    def rows(residue, first, count):
        return pl.ds(residue + first * dil, count, stride=dil)

    def scores(it):
        residue = it // n_chunk
        q0 = (it % n_chunk) * DIL_QCHUNK
        k0 = jnp.clip(q0 - DIL_RADIUS, 0, length - DIL_KWIN)
        q = q_ref[rows(residue, q0, DIL_QCHUNK), :].astype(BF16)
        k = k_ref[rows(residue, k0, DIL_KWIN), :].astype(BF16)
        return _dot_nt(q, k) + mask_ref[(q0 - k0) // DIL_RADIUS], residue, q0, k0

    def finish(s, residue, q0, k0):
        v1 = jnp.concatenate([v_ref[rows(residue, k0, DIL_KWIN), :].astype(BF16), ones], axis=1)
        m = jnp.max(s, axis=-1, keepdims=True)
        acc = _dot(jnp.exp2(s - m).astype(BF16), v1)
        den = acc[:, HEAD_DIM:]
        o_ref[rows(residue, q0, DIL_QCHUNK), :] = acc[:, :HEAD_DIM] / den
        lse_ref[rows(residue, q0, DIL_QCHUNK), :] = m * LN2 + jnp.log(den[:, :1])

    def body(i, carry):
        pending = [scores(i * DIL_GROUP + j) for j in range(DIL_GROUP)]
        for item in pending:
            finish(*item)
        return carry

    assert (dil * n_chunk) % DIL_GROUP == 0
    lax.fori_loop(0, dil * n_chunk // DIL_GROUP, body, 0)


def _dil_group(qk, v, group, dil, batch, seq):
    length = seq // dil
    assert length % DIL_QCHUNK == 0 and length >= DIL_KWIN
    hg = DIL_HEADS_PER_GROUP
    blk = (seq, HEAD_DIM)
    mask = _dil_window_mask()
    return pl.pallas_call(
        functools.partial(_dil_kernel, seq=seq, dil=dil),
        grid=(batch, hg),
        in_specs=[pl.BlockSpec(blk, lambda b, h: (b, group * hg + h)),
                  pl.BlockSpec(blk, lambda b, h: (b, DIL_HEADS + group * hg + h)),
                  pl.BlockSpec(blk, lambda b, h: (b, group * hg + h)),
                  pl.BlockSpec(mask.shape, lambda b, h: (0, 0, 0))],
        out_specs=[pl.BlockSpec(blk, lambda b, h: (b, h)),
                   pl.BlockSpec((None, seq, 1), lambda b, h: (b * hg + h, 0, 0))],
        out_shape=[jax.ShapeDtypeStruct((batch * seq, DIL_OUT), F32),
                   jax.ShapeDtypeStruct((batch * hg, seq, 1), F32)],
        compiler_params=_params("parallel", "parallel"),
        name="dilated_attention",
    )(qk, qk, v, mask)


def _dil_mix_kernel(o0_ref, o1_ref, o2_ref, l0_ref, l1_ref, l2_ref, o_ref):
    l0, l1, l2 = l0_ref[...], l1_ref[...], l2_ref[...]
    mx = jnp.maximum(jnp.maximum(l0, l1), l2)
    e0, e1, e2 = jnp.exp(l0 - mx), jnp.exp(l1 - mx), jnp.exp(l2 - mx)
    z = e0 + e1 + e2
    w0, w1, w2 = e0 / z, e1 / z, e2 / z
    for h in range(DIL_HEADS_PER_GROUP):
        sl = slice(h * HEAD_DIM, (h + 1) * HEAD_DIM)
        hs = slice(h, h + 1)
        o_ref[:, sl] = (w0[:, hs] * o0_ref[:, sl] + w1[:, hs] * o1_ref[:, sl]
                        + w2[:, hs] * o2_ref[:, sl]).astype(o_ref.dtype)


def _dil_mix(outs, lses, tm=512):
    t = outs[0].shape[0]
    ospec = pl.BlockSpec((tm, DIL_OUT), lambda i: (i, 0))
    lspec = pl.BlockSpec((tm, DIL_HEADS_PER_GROUP), lambda i: (i, 0))
    return pl.pallas_call(
        _dil_mix_kernel,
        grid=(t // tm,),
        in_specs=[ospec] * 3 + [lspec] * 3,
        out_specs=ospec,
        out_shape=jax.ShapeDtypeStruct((t, DIL_OUT), BF16),
        compiler_params=_params("parallel"),
        name="dilated_mix",
    )(*outs, *lses)


def _dilated_attention(qk, v, batch, seq):
    hg = DIL_HEADS_PER_GROUP
    outs, lses = [], []
    for g, (window, dil) in enumerate(DIL_GROUPS):
        assert window // (2 * dil) == DIL_RADIUS and seq % dil == 0
        o, lse = _dil_group(qk, v, g, dil, batch, seq)
        outs.append(o)
        lses.append(lse.reshape(batch, hg, seq).transpose(0, 2, 1).reshape(batch * seq, hg))
    return _dil_mix(outs, lses)


def _mla_q_kernel(a_ref, w_ref, g_ref, cos_ref, sin_ref, o_ref):
    a = a_ref[...]
    gain = g_ref[...]
    cos_t, sin_t = cos_ref[...], sin_ref[...]
    acc = _dot(a, w_ref[0])
    for h in range(MLA_HEADS):
        nxt = _dot(a, w_ref[h + 1]) if h + 1 < MLA_HEADS else None
        ms = jnp.sum(acc * acc, axis=-1, keepdims=True) * (1.0 / MLA_QK)
        y = acc * lax.rsqrt(ms + RMS_EPS) * gain
        o_ref[h, :, :MLA_NOPE] = y[:, :MLA_NOPE].astype(o_ref.dtype)
        o_ref[h, :, MLA_NOPE:] = _rotate_pairs(y[:, MLA_NOPE:], cos_t, sin_t,
                                               MLA_ROPE // 2).astype(o_ref.dtype)
        acc = nxt


def _mla_q_proj(cq_n, w_uq_h, gain, cos_t, sin_t, seq, tm=512):
    t, rank = cq_n.shape
    per = seq // tm
    return pl.pallas_call(
        _mla_q_kernel,
        grid=(t // tm,),
        in_specs=[pl.BlockSpec((tm, rank), lambda i: (i, 0)),
                  pl.BlockSpec((MLA_HEADS, rank, MLA_QK_PAD), lambda i: (0, 0, 0)),
                  pl.BlockSpec((1, MLA_QK_PAD), lambda i: (0, 0)),
                  pl.BlockSpec((tm, LANES), lambda i: (i % per, 0)),
                  pl.BlockSpec((tm, LANES), lambda i: (i % per, 0))],
        out_specs=pl.BlockSpec((MLA_HEADS, tm, MLA_QK_PAD), lambda i: (0, i, 0)),
        out_shape=jax.ShapeDtypeStruct((MLA_HEADS, t, MLA_QK_PAD), BF16),
        compiler_params=_params("parallel"),
        name="mla_q_proj",
    )(cq_n, w_uq_h, gain, cos_t, sin_t)


def _mla_kv_kernel(a_ref, w_ref, kr_ref, g0_ref, g1_ref, cos_ref, sin_ref, k_ref, v_ref):
    a = a_ref[...]
    kr = kr_ref[...]
    kr_ss = jnp.sum(kr * kr, axis=-1, keepdims=True)
    g0, g1 = g0_ref[...], g1_ref[...]
    cos_t, sin_t = cos_ref[...], sin_ref[...]
    ones = jnp.ones((a.shape[0], MLA_V), v_ref.dtype)
    kr_rot = _rotate_pairs(kr * g1, cos_t, sin_t, MLA_ROPE // 2)
    acc = _dot(a, w_ref[0])
    for h in range(MLA_HEADS):
        nxt = _dot(a, w_ref[h + 1]) if h + 1 < MLA_HEADS else None
        kn = acc[:, :MLA_NOPE]
        ms = (jnp.sum(kn * kn, axis=-1, keepdims=True) + kr_ss) * (1.0 / MLA_QK)
        inv = lax.rsqrt(ms + RMS_EPS)
        k_ref[h, :, :MLA_NOPE] = (kn * inv * g0).astype(k_ref.dtype)
        k_ref[h, :, MLA_NOPE:] = (kr_rot * inv).astype(k_ref.dtype)
        v_ref[h, :, :MLA_V] = acc[:, MLA_NOPE:].astype(v_ref.dtype)
        v_ref[h, :, MLA_V:] = ones
        acc = nxt


def _mla_kv_proj(ckv_n, w_ukv_h, k_r, g0, g1, cos_t, sin_t, seq, tm=1024):
    t, rank = ckv_n.shape
    per = seq // tm
    return pl.pallas_call(
        _mla_kv_kernel,
        grid=(t // tm,),
        in_specs=[pl.BlockSpec((tm, rank), lambda i: (i, 0)),
                  pl.BlockSpec((MLA_HEADS, rank, MLA_NOPE + MLA_V), lambda i: (0, 0, 0)),
                  pl.BlockSpec((tm, LANES), lambda i: (i, 0)),
                  pl.BlockSpec((1, LANES), lambda i: (0, 0)),
                  pl.BlockSpec((1, LANES), lambda i: (0, 0)),
                  pl.BlockSpec((tm, LANES), lambda i: (i % per, 0)),
                  pl.BlockSpec((tm, LANES), lambda i: (i % per, 0))],
        out_specs=[pl.BlockSpec((MLA_HEADS, tm, MLA_QK_PAD), lambda i: (0, i, 0)),
                   pl.BlockSpec((MLA_HEADS, tm, 2 * MLA_V), lambda i: (0, i, 0))],
        out_shape=[jax.ShapeDtypeStruct((MLA_HEADS, t, MLA_QK_PAD), BF16),
                   jax.ShapeDtypeStruct((MLA_HEADS, t, 2 * MLA_V), BF16)],
        compiler_params=_params("parallel"),
        name="mla_kv_proj",
    )(ckv_n, w_ukv_h, k_r, g0, g1, cos_t, sin_t)


MLA_KV_CHUNKS = 8
MLA_QK_AHEAD = 1


def _mla_attn_kernel(q_ref, k_ref, v_ref, o_ref):
    q = q_ref[...]
    tk = k_ref.shape[0] // MLA_KV_CHUNKS
    m = acc = None
    ahead = [_dot_nt(q, k_ref[i * tk:(i + 1) * tk, :]) for i in range(MLA_QK_AHEAD)]
    for c in range(MLA_KV_CHUNKS):
        s = ahead.pop(0)
        nxt = c + MLA_QK_AHEAD
        if nxt < MLA_KV_CHUNKS:
            ahead.append(_dot_nt(q, k_ref[nxt * tk:(nxt + 1) * tk, :]))
        m_c = jnp.max(s, axis=-1, keepdims=True)
        if c == 0:
            m = m_c
            acc = _dot(jnp.exp2(s - m).astype(BF16), v_ref[:tk, :])
        else:
            m_new = jnp.maximum(m, m_c)
            acc = jnp.exp2(m - m_new) * acc + _dot(jnp.exp2(s - m_new).astype(BF16),
                                                   v_ref[c * tk:(c + 1) * tk, :])
            m = m_new
    o_ref[...] = (acc[:, :MLA_V] / acc[:, MLA_V:]).astype(o_ref.dtype)


def _mla_attention(q, k, v, batch, seq, tq=1024):
    t = q.shape[1]
    nq = seq // tq
    return pl.pallas_call(
        _mla_attn_kernel,
        grid=(batch, MLA_HEADS, nq),
        in_specs=[pl.BlockSpec((None, tq, MLA_QK_PAD), lambda b, h, i: (h, b * nq + i, 0)),
                  pl.BlockSpec((None, seq, MLA_QK_PAD), lambda b, h, i: (h, b, 0)),
                  pl.BlockSpec((None, seq, 2 * MLA_V), lambda b, h, i: (h, b, 0))],
        out_specs=pl.BlockSpec((tq, MLA_V), lambda b, h, i: (b * nq + i, h)),
        out_shape=jax.ShapeDtypeStruct((t, MLA_HEADS * MLA_V), BF16),
        compiler_params=_params("parallel", "parallel", "arbitrary"),
        name="mla_attention",
    )(q, k, v)


def _merge_kernel(ona_ref, odl_ref, omla_ref, wna_ref, wdl_ref, wmla_ref, g0_ref, g1_ref, g2_ref, o_ref):
    ona, odl, omla = ona_ref[...], odl_ref[...], omla_ref[...]
    for c in range(o_ref.shape[1] // MXU_COLS):
        cols = slice(c * MXU_COLS, (c + 1) * MXU_COLS)
        acc = g0_ref[:, cols].astype(F32) * _dot(ona, wna_ref[:, cols].astype(BF16))
        acc = acc + g1_ref[:, cols].astype(F32) * _dot(odl, wdl_ref[:, cols].astype(BF16))
        acc = acc + g2_ref[:, cols].astype(F32) * _dot(omla, wmla_ref[:, cols].astype(BF16))
        o_ref[:, cols] = acc.astype(o_ref.dtype)


def _merge(o_na, o_dl, o_mla, w_na, w_dl, w_mla, layer, gates, tm=1024, tn=512):
    t = o_na.shape[0]
    d = w_na.shape[2]
    nj = d // tn

    def act(a):
        return pl.BlockSpec((tm, a.shape[1]), lambda i, j: (i, 0))

    def gate(idx):
        return pl.BlockSpec((tm, tn), lambda i, j: (i, idx * nj + j))

    return pl.pallas_call(
        _merge_kernel,
        grid=(t // tm, nj),
        in_specs=[act(o_na), act(o_dl), act(o_mla),
                  _weight_spec(w_na, layer, 0, tn), _weight_spec(w_dl, layer, 0, tn),
                  _weight_spec(w_mla, layer, 0, tn), gate(0), gate(1), gate(2)],
        out_specs=pl.BlockSpec((tm, tn), lambda i, j: (i, j)),
        out_shape=jax.ShapeDtypeStruct((t, d), BF16),
        compiler_params=_params("parallel", "parallel"),
        name="branch_merge",
    )(o_na, o_dl, o_mla, w_na, w_dl, w_mla, gates, gates, gates)


def _swiglu_pieces(h, wg_ref, wu_ref, o_ref):
    for c in range(o_ref.shape[1] // MXU_COLS):
        cols = slice(c * MXU_COLS, (c + 1) * MXU_COLS)
        g = _dot(h, wg_ref[:, cols].astype(BF16))
        u = _dot(h, wu_ref[:, cols].astype(BF16))
        o_ref[:, cols] = (g * jax.nn.sigmoid(g) * u).astype(o_ref.dtype)


def _gu_kernel(h_ref, wg_ref, wu_ref, o_ref):
    _swiglu_pieces(h_ref[...], wg_ref, wu_ref, o_ref)


def _swiglu_up(h, w_gu, layer, tm=1024, tn=512):
    t, d = h.shape
    ff = w_gu.shape[2] // 2
    nj = ff // tn
    return pl.pallas_call(
        _gu_kernel,
        grid=(t // tm, nj),
        in_specs=[pl.BlockSpec((tm, d), lambda i, j: (i, 0)),
                  _weight_spec(w_gu, layer, 0, tn), _weight_spec(w_gu, layer, ff, tn)],
        out_specs=pl.BlockSpec((tm, tn), lambda i, j: (i, j)),
        out_shape=jax.ShapeDtypeStruct((t, ff), BF16),
        compiler_params=_params("parallel", "parallel"),
        name="swiglu_up",
    )(h, w_gu, w_gu)


TOP_K = 2
MOE_TILE_ROWS = 1024


def _moe_routing(route, tokens, tm):
    e_flat = jnp.concatenate([route[:, N_EXPERTS], route[:, N_EXPERTS + 1]]).astype(jnp.int32)
    onehot = (e_flat[:, None] == jnp.arange(N_EXPERTS, dtype=jnp.int32)[None, :]).astype(jnp.int32)
    csum = jnp.cumsum(onehot, axis=0)
    rank = jnp.sum((csum - onehot) * onehot, axis=1)
    padded = ((csum[-1] + tm - 1) // tm) * tm
    ends = jnp.cumsum(padded)
    slot = jnp.sum(onehot * (ends - padded)[None, :], axis=1) + rank
    n_tiles = (TOP_K * tokens) // tm + N_EXPERTS
    tile_start = jnp.arange(n_tiles, dtype=jnp.int32) * tm
    tile_expert = jnp.minimum(jnp.sum(tile_start[:, None] >= ends[None, :], axis=1), N_EXPERTS - 1)
    te = jnp.concatenate([tile_expert, ends[-1:] // tm]).astype(jnp.int32)
    return slot.astype(jnp.int32), te


def _dispatch_kernel(slot_ref, x_ref, g_ref, sc_ref, sh_ref, dst_in_ref, dst_ref, h_ref, sem, *, rows, tokens):
    del dst_in_ref
    base = pl.program_id(0) * rows
    h_ref[...] = _modulated_norm(x_ref[...], g_ref[...], sc_ref[...], sh_ref[...])

    def row_copy(r, choice):
        slot = slot_ref[choice * tokens + base + r]
        return pltpu.make_async_copy(h_ref.at[pl.ds(r, 1), :], dst_ref.at[pl.ds(slot, 1), :], sem)

    def start(r, carry):
        row_copy(r, 0).start()
        row_copy(r, 1).start()
        return carry

    def wait(r, carry):
        row_copy(r, 0).wait()
        row_copy(r, 1).wait()
        return carry

    lax.fori_loop(0, rows, start, 0, unroll=8)
    lax.fori_loop(0, rows, wait, 0, unroll=8)


def _moe_dispatch(x, g, sc, sh, slot, seq, n_rows, rows=256):
    t, d = x.shape
    per = seq // rows
    return pl.pallas_call(
        functools.partial(_dispatch_kernel, rows=rows, tokens=t),
        grid_spec=pltpu.PrefetchScalarGridSpec(
            num_scalar_prefetch=1,
            grid=(t // rows,),
            in_specs=[pl.BlockSpec((rows, d), lambda i, s: (i, 0)),
                      pl.BlockSpec((1, d), lambda i, s: (0, 0)),
                      pl.BlockSpec((None, 1, d), lambda i, s: (i // per, 0, 0)),
                      pl.BlockSpec((None, 1, d), lambda i, s: (i // per, 0, 0)),
                      pl.BlockSpec(memory_space=pl.ANY)],
            out_specs=pl.BlockSpec(memory_space=pl.ANY),
            scratch_shapes=[pltpu.VMEM((rows, d), F32), pltpu.SemaphoreType.DMA(())]),
        out_shape=jax.ShapeDtypeStruct((n_rows, d), F32),
        input_output_aliases={5: 0},
        compiler_params=_params("arbitrary"),
        name="moe_dispatch",
    )(slot, x, g.reshape(1, d), sc, sh, jnp.zeros((n_rows, d), F32))


def _moe_up_kernel(te_ref, xs_ref, wg_ref, wu_ref, o_ref, hb_ref, *, n_tiles):
    used = pl.program_id(0) < te_ref[n_tiles]

    @pl.when(used & (pl.program_id(1) == 0))
    def _():
        hb_ref[...] = xs_ref[...].astype(BF16)

    @pl.when(used)
    def _():
        _swiglu_pieces(hb_ref[...], wg_ref, wu_ref, o_ref)

    @pl.when(jnp.logical_not(used))
    def _():
        o_ref[...] = jnp.zeros(o_ref.shape, o_ref.dtype)


def _moe_up(xs, w_gu, te, tn=512):
    n_rows, d = xs.shape
    ff = w_gu.shape[2] // 2
    nj = ff // tn
    n_tiles = te.shape[0] - 1
    tm = n_rows // n_tiles

    def last_used(i, te_ref):
        return jnp.minimum(i, te_ref[n_tiles] - 1)

    return pl.pallas_call(
        functools.partial(_moe_up_kernel, n_tiles=n_tiles),
        grid_spec=pltpu.PrefetchScalarGridSpec(
            num_scalar_prefetch=1,
            grid=(n_tiles, nj),
            in_specs=[pl.BlockSpec((tm,d), lambda i, j, te_ref: (last_used(i, te_ref), 0)),
                      pl.BlockSpec((None, d, tn), lambda i, j, te_ref: (te_ref[i], 0, j)),
                      pl.BlockSpec((None, d, tn), lambda i, j, te_ref: (te_ref[i], 0, nj + j))],
            out_specs=pl.BlockSpec((tm,tn), lambda i, j, te_ref: (i, j)),
            scratch_shapes=[pltpu.VMEM((tm, d), BF16)]),
        out_shape=jax.ShapeDtypeStruct((n_rows, ff), BF16),
        compiler_params=_params("parallel", "arbitrary"),
        name="moe_up",
    )(te, xs, w_gu, w_gu)


def _moe_down_kernel(te_ref, a_ref, w_ref, o_ref, *, n_tiles):
    used = pl.program_id(0) < te_ref[n_tiles]

    @pl.when(used)
    def _():
        o_ref[...] = _dot(a_ref[...], w_ref[...])

    @pl.when(jnp.logical_not(used))
    def _():
        o_ref[...] = jnp.zeros(o_ref.shape, o_ref.dtype)


def _moe_down(act, w_down, te, tn=256):
    n_rows, ff = act.shape
    d = w_down.shape[2]
    n_tiles = te.shape[0] - 1
    tm = n_rows // n_tiles

    def last_used(i, te_ref):
        return jnp.minimum(i, te_ref[n_tiles] - 1)

    return pl.pallas_call(
        functools.partial(_moe_down_kernel, n_tiles=n_tiles),
        grid_spec=pltpu.PrefetchScalarGridSpec(
            num_scalar_prefetch=1,
            grid=(n_tiles, d // tn),
            in_specs=[pl.BlockSpec((tm,ff), lambda i, j, te_ref: (last_used(i, te_ref), 0)),
                      pl.BlockSpec((None, ff, tn), lambda i, j, te_ref: (te_ref[i], 0, j))],
            out_specs=pl.BlockSpec((tm,tn), lambda i, j, te_ref: (i, j))),
        out_shape=jax.ShapeDtypeStruct((n_rows, d), F32),
        compiler_params=_params("parallel", "arbitrary"),
        name="moe_down",
    )(te, act, w_down)


def _combine_kernel(slot_ref, x_ref, gate_ref, w_ref, ys_ref, o_ref, buf_ref, sem, *, rows, tokens):
    base = pl.program_id(0) * rows

    def row_copy(r, choice):
        slot = slot_ref[choice * tokens + base + r]
        return pltpu.make_async_copy(ys_ref.at[pl.ds(slot, 1), :], buf_ref.at[choice, pl.ds(r, 1), :], sem)

    def start(r, carry):
        row_copy(r, 0).start()
        row_copy(r, 1).start()
        return carry

    def wait(r, carry):
        row_copy(r, 0).wait()
        row_copy(r, 1).wait()
        return carry

    lax.fori_loop(0, rows, start, 0, unroll=8)
    lax.fori_loop(0, rows, wait, 0, unroll=8)
    w = w_ref[...]
    y = w[:, 0:1] * buf_ref[0] + w[:, 1:2] * buf_ref[1]
    o_ref[...] = x_ref[...] + gate_ref[...] * y


def _moe_combine(x, gate, w12, ys, slot, seq, rows=256):
    t, d = x.shape
    per = seq // rows
    return pl.pallas_call(
        functools.partial(_combine_kernel, rows=rows, tokens=t),
        grid_spec=pltpu.PrefetchScalarGridSpec(
            num_scalar_prefetch=1,
            grid=(t // rows,),
            in_specs=[pl.BlockSpec((rows, d), lambda i, s: (i, 0)),
                      pl.BlockSpec((None, 1, d), lambda i, s: (i // per, 0, 0)),
                      pl.BlockSpec((rows, TOP_K), lambda i, s: (i, 0)),
                      pl.BlockSpec(memory_space=pl.ANY)],
            out_specs=pl.BlockSpec((rows, d), lambda i, s: (i, 0)),
            scratch_shapes=[pltpu.VMEM((TOP_K, rows, d), F32), pltpu.SemaphoreType.DMA(())]),
        out_shape=jax.ShapeDtypeStruct((t, d), F32),
        compiler_params=_params("arbitrary"),
        name="moe_combine",
    )(slot, x, gate, w12, ys)


def _router_kernel(x_ref, g_ref, sc_ref, sh_ref, w_ref, o_ref):
    h = _modulated_norm(x_ref[...], g_ref[...], sc_ref[...], sh_ref[...])
    logits = jnp.dot(h, w_ref[...], preferred_element_type=F32, precision=lax.Precision.HIGHEST)
    lane = lax.broadcasted_iota(jnp.int32, logits.shape, 1).astype(F32)
    lg = jnp.where(lane < N_EXPERTS, logits, NEG_INF)
    m1 = jnp.max(lg, axis=-1, keepdims=True)
    i1 = jnp.min(jnp.where(lg == m1, lane, float(LANES)), axis=-1, keepdims=True)
    lg2 = jnp.where(lane == i1, NEG_INF, lg)
    m2 = jnp.max(lg2, axis=-1, keepdims=True)
    i2 = jnp.min(jnp.where(lg2 == m2, lane, float(LANES)), axis=-1, keepdims=True)
    e2 = jnp.exp(m2 - m1)
    z = 1.0 + e2
    out = jnp.where(lane == N_EXPERTS, i1, 0.0) + jnp.where(lane == N_EXPERTS + 1, i2, 0.0)
    out = out + jnp.where(lane == N_EXPERTS + 2, 1.0 / z, 0.0) + jnp.where(lane == N_EXPERTS + 3, e2 / z, 0.0)
    o_ref[...] = out


def _router(x, g, sc, sh, w_router_pad, seq, tm=512):
    t, d = x.shape
    per = seq // tm
    return pl.pallas_call(
        _router_kernel,
        grid=(t // tm,),
        in_specs=[pl.BlockSpec((tm, d), lambda i: (i, 0)),
                  pl.BlockSpec((1, d), lambda i: (0, 0)),
                  pl.BlockSpec((None, 1, d), lambda i: (i // per, 0, 0)),
                  pl.BlockSpec((None, 1, d), lambda i: (i // per, 0, 0)),
                  pl.BlockSpec((d, LANES), lambda i: (0, 0))],
        out_specs=pl.BlockSpec((tm, LANES), lambda i: (i, 0)),
        out_shape=jax.ShapeDtypeStruct((t, LANES), F32),
        compiler_params=_params("parallel"),
        name="router",
    )(x, g.reshape(1, d), sc, sh, w_router_pad)


MM_TM = 1024
MM_TN = 1024


def _col_tile(n):
    return MM_TN if n % MM_TN == 0 else MM_TN // 2


def _residual_matmul(a, w, layer, x, gate, seq, name, tm=MM_TM, tn=None):
    per = seq // tm
    tn = _col_tile(w.shape[2]) if tn is None else tn
    return _matmul(a, w, layer=layer, tm=tm, tn=tn, out_dtype=F32, epilogue=_ep_residual, split=MXU_COLS,
                   extras=[(x, (tm, tn), lambda i, j: (i, j)),
                           (gate, (None, 1, tn), lambda i, j: (i // per, 0, j))],
                   name=name)


def _token_mixer(h, x, gate, lw, layer, batch, seq, tables):
    d = h.shape[1]
    w_in = lw["w_in"]
    tm = MM_TM
    per = seq // tm
    cos_d, sin_d, cos_m, sin_m = tables

    def cols(lo, hi):
        return w_in[layer, :, lo:hi].astype(BF16)

    def proj(lo, width, **kw):
        return _matmul(h, w_in, layer=layer, col0=lo, n=width, tm=tm, **kw)

    def tile_gain(gq, gk, heads):
        gq = gq * (LOG2E / math.sqrt(HEAD_DIM))
        return jnp.concatenate([jnp.tile(gq, heads), jnp.tile(gk, heads)]).reshape(1, -1).astype(F32)

    o = 0
    tn = _col_tile(2 * NA_W)
    na_qk = proj(o, 2 * NA_W, tn=tn, out_dtype=BF16, epilogue=_ep_headnorm, split=MXU_COLS,
                 extras=[(tile_gain(lw["na_q_norm"], lw["na_k_norm"], NA_HEADS), (1, tn),
                          lambda i, j: (0, j))], name="na_qk_proj")
    o += 2 * NA_W
    na_v = proj(o, NA_W, tn=_col_tile(NA_W), out_dtype=BF16, split=MXU_COLS, name="na_v_proj")
    o += NA_W
    tn = _col_tile(2 * DIL_W)
    dl_qk = proj(o, 2 * DIL_W, tn=tn, out_dtype=F32, epilogue=_ep_headnorm_rope,
                    split=MXU_COLS,
                    extras=[(tile_gain(lw["dil_q_norm"], lw["dil_k_norm"], DIL_HEADS), (1, tn),
                             lambda i, j: (0, j)),
                            (cos_d, (tm, LANES), lambda i, j: (i % per, 0)),
                            (sin_d, (tm, LANES), lambda i, j: (i % per, 0)),
                            (_pair_swap_matrix(MXU_COLS, ROT_DIM // 2), (MXU_COLS, MXU_COLS),
                             lambda i, j: (0, 0))], name="dil_qk_proj")
    o += 2 * DIL_W
    dl_v = proj(o, DIL_W, tn=_col_tile(DIL_W), out_dtype=F32, split=MXU_COLS, name="dil_v_proj")
    o += DIL_W
    cq_n = proj(o, MLA_Q_RANK, tn=MLA_Q_RANK, out_dtype=BF16, epilogue=_ep_rownorm,
                extras=[(lw["mla_q_a_norm"].reshape(1, -1), (1, MLA_Q_RANK), lambda i, j: (0, 0))],
                name="mla_cq_proj")
    o += MLA_Q_RANK
    ckv_n = _matmul(h, cols(o, o + MLA_KV_RANK), tm=tm, tn=MLA_KV_RANK, out_dtype=BF16, epilogue=_ep_rownorm,
                    extras=[(lw["mla_kv_a_norm"].reshape(1, -1), (1, MLA_KV_RANK), lambda i, j: (0, 0))],
                    name="mla_ckv_proj")
    o += MLA_KV_RANK
    w_kr = jnp.pad(w_in[layer, :, o:o + MLA_ROPE], ((0, 0), (0, LANES - MLA_ROPE))).astype(BF16)
    k_r = _matmul(h, w_kr, tm=tm, tn=LANES, out_dtype=F32, name="mla_kr_proj")
    o += MLA_ROPE
    gates = _matmul(h, cols(o, o + 3 * d), tm=tm, tn=_col_tile(3 * d), out_dtype=BF16, epilogue=_ep_sigmoid,
                    split=MXU_COLS, name="gate_proj")

    o_na = _na_attention(na_qk, na_v, lw["na_rpb"], batch, seq)
    o_dl = _dilated_attention(dl_qk, dl_v, batch, seq)
    w_uq = lw["mla_w_uq"].reshape(MLA_Q_RANK, MLA_HEADS, MLA_QK).transpose(1, 0, 2)
    w_uq = jnp.pad(w_uq, ((0, 0), (0, 0), (0, MLA_QK_PAD - MLA_QK))).astype(BF16)
    w_ukv = lw["mla_w_ukv"].reshape(MLA_KV_RANK, MLA_HEADS, MLA_NOPE + MLA_V).transpose(1, 0, 2).astype(BF16)
    gq = jnp.pad(lw["mla_q_norm"] * (LOG2E / math.sqrt(MLA_QK)), (0, MLA_QK_PAD - MLA_QK)).reshape(1, MLA_QK_PAD)
    gk = lw["mla_k_norm"]
    gk0 = gk[:MLA_NOPE].reshape(1, LANES)
    gk1 = jnp.pad(gk[MLA_NOPE:], (0, LANES - MLA_ROPE)).reshape(1, LANES)
    q_m = _mla_q_proj(cq_n, w_uq, gq, cos_m, sin_m, seq)
    k_m, v_m = _mla_kv_proj(ckv_n, w_ukv, k_r, gk0, gk1, cos_m, sin_m, seq)
    o_mla = _mla_attention(q_m, k_m, v_m, batch, seq)

    merged = _merge(o_na, o_dl, o_mla, lw["w_branch_na"], lw["w_branch_dil"], lw["w_branch_mla"], layer, gates)
    return _residual_matmul(merged, lw["w_out"], layer, x, gate, seq, "out_proj")


def kernel(x, c, w_ada, b_ada, norm_mix, norm_ffn, w_in, na_q_norm, na_k_norm, na_rpb, dil_q_norm, dil_k_norm, mla_q_a_norm, mla_w_uq, mla_kv_a_norm, mla_w_ukv, mla_q_norm, mla_k_norm, w_branch_na, w_branch_dil, w_branch_mla, w_out, ffn_w_gu, ffn_w_down, moe_router, moe_w_gu, moe_w_down):
    batch, seq, d = x.shape
    depth = w_ada.shape[0]
    assert seq % MM_TM == 0 and seq % GRID_W == 0
    t = batch * seq
    xf = x.reshape(t, d)

    c_pad = jnp.pad(c, ((0, (-batch) % 8), (0, 0)))
    mod = _adaln(c_pad, w_ada, b_ada)[:, :batch].reshape(depth, batch, 6, 1, d)
    tables = _rope_tables(seq, ROT_DIM, 1.0) + _rope_tables(seq, MLA_ROPE, 1.0)

    for l in range(depth):
        sh1, sc1, g1, sh2, sc2, g2 = [mod[l, :, i] for i in range(6)]
        lw = dict(w_in=w_in, na_q_norm=na_q_norm[l], na_k_norm=na_k_norm[l], na_rpb=na_rpb[l],
                  dil_q_norm=dil_q_norm[l], dil_k_norm=dil_k_norm[l], mla_q_a_norm=mla_q_a_norm[l],
                  mla_w_uq=mla_w_uq[l], mla_kv_a_norm=mla_kv_a_norm[l], mla_w_ukv=mla_w_ukv[l],
                  mla_q_norm=mla_q_norm[l], mla_k_norm=mla_k_norm[l], w_branch_na=w_branch_na,
                  w_branch_dil=w_branch_dil, w_branch_mla=w_branch_mla, w_out=w_out)
        h = _normmod(xf, norm_mix[l], sc1, sh1, seq)
        xf = _token_mixer(h, xf, g1, lw, l, batch, seq, tables)

        if l % 2 == 0:
            h = _normmod(xf, norm_ffn[l], sc2, sh2, seq)
            act = _swiglu_up(h, ffn_w_gu, l // 2)
            xf = _residual_matmul(act, ffn_w_down, l // 2, xf, g2, seq, "down_proj", tm=1024, tn=256)
        else:
            w_r = jnp.pad(moe_router[l // 2], ((0, 0), (0, LANES - N_EXPERTS)))
            route = _router(xf, norm_ffn[l], sc2, sh2, w_r, seq)
            slot, te = _moe_routing(route, t, MOE_TILE_ROWS)
            n_rows = (te.shape[0] - 1) * MOE_TILE_ROWS
            xs = _moe_dispatch(xf, norm_ffn[l], sc2, sh2, slot, seq, n_rows)
            act = _moe_up(xs, moe_w_gu[l // 2], te)
            ys = _moe_down(act, moe_w_down[l // 2].astype(BF16), te)
            xf = _moe_combine(xf, g2, route[:, N_EXPERTS + 2:N_EXPERTS + 4], ys, slot, seq)
    return xf.reshape(batch, seq, d)
```

```python
import functools
import math

import numpy as np
import jax
import jax.numpy as jnp
from jax import lax
from jax.experimental import pallas as pl
from jax.experimental.pallas import tpu as pltpu

GRID_W = 64
HEAD_DIM = 128
ROPE_THETA = 500000.0
ROT_DIM = HEAD_DIM // 4
RMS_EPS = 1e-6
NEG_INF = -1e30
NA_HEADS = 8
NA_KH = 8
NA_KW = 16
DIL_GROUPS = ((128, 1), (512, 4), (2048, 16))
DIL_HEADS_PER_GROUP = 4
DIL_HEADS = DIL_HEADS_PER_GROUP * len(DIL_GROUPS)
DIL_RADIUS = 64
MLA_HEADS = 8
MLA_NOPE = 128
MLA_ROPE = 64
MLA_V = 128
MLA_Q_RANK = 768
MLA_KV_RANK = 512
MLA_QK = MLA_NOPE + MLA_ROPE
MLA_QK_PAD = 256
N_EXPERTS = 8
LANES = 128
MXU_COLS = 256

NA_W = NA_HEADS * HEAD_DIM
DIL_W = DIL_HEADS * HEAD_DIM
DIL_OUT = DIL_HEADS_PER_GROUP * HEAD_DIM

VMEM_LIMIT_BYTES = 48 * 1024 * 1024

F32 = jnp.float32
BF16 = jnp.bfloat16
LOG2E = math.log2(math.e)
LN2 = math.log(2.0)


def _params(*sem):
    return pltpu.CompilerParams(dimension_semantics=sem, vmem_limit_bytes=VMEM_LIMIT_BYTES)


def _dot(a, b):
    return jnp.dot(a, b, preferred_element_type=F32)


def _dot_nt(a, b):
    return lax.dot_general(a, b, (((1,), (1,)), ((), ())), preferred_element_type=F32)


def _adaln_kernel(c_ref, w_ref, b_ref, o_ref):
    o_ref[...] = _dot(c_ref[...], w_ref[...]) + b_ref[...]


def _adaln(c_pad, w_ada, b_ada):
    depth, d, n = w_ada.shape
    tn = 1024
    return pl.pallas_call(
        _adaln_kernel,
        grid=(depth, n // tn),
        in_specs=[pl.BlockSpec(c_pad.shape, lambda l, j: (0, 0)),
                  pl.BlockSpec((None, d, tn), lambda l, j: (l, 0, j)),
                  pl.BlockSpec((None, 1, tn), lambda l, j: (l, 0, j))],
        out_specs=pl.BlockSpec((None, c_pad.shape[0], tn), lambda l, j: (l, 0, j)),
        out_shape=jax.ShapeDtypeStruct((depth, c_pad.shape[0], n), F32),
        compiler_params=_params("parallel", "parallel"),
        name="adaln",
    )(c_pad, w_ada, b_ada.reshape(depth, 1, n))


def _modulated_norm(x, g, sc, sh):
    ms = jnp.mean(x * x, axis=-1, keepdims=True)
    return (x * lax.rsqrt(ms + RMS_EPS) * g) * (1.0 + sc) + sh


def _normmod_kernel(x_ref, g_ref, sc_ref, sh_ref, o_ref):
    o_ref[...] = _modulated_norm(x_ref[...], g_ref[...], sc_ref[...], sh_ref[...]).astype(o_ref.dtype)


def _normmod(x, g, sc, sh, seq, tm=512):
    t, d = x.shape
    per = seq // tm
    return pl.pallas_call(
        _normmod_kernel,
        grid=(t // tm,),
        in_specs=[pl.BlockSpec((tm, d), lambda i: (i, 0)),
                  pl.BlockSpec((1, d), lambda i: (0, 0)),
                  pl.BlockSpec((None, 1, d), lambda i: (i // per, 0, 0)),
                  pl.BlockSpec((None, 1, d), lambda i: (i // per, 0, 0))],
        out_specs=pl.BlockSpec((tm, d), lambda i: (i, 0)),
        out_shape=jax.ShapeDtypeStruct((t, d), BF16),
        compiler_params=_params("parallel"),
        name="normmod",
    )(x, g.reshape(1, d), sc, sh)


def _mm_kernel(*refs, n_extra, epilogue, split):
    a_ref, b_ref = refs[0], refs[1]
    extras = refs[2:2 + n_extra]
    o_ref = refs[2 + n_extra]
    a = a_ref[...]
    n_pieces = b_ref.shape[1] // split
    acc = _dot(a, b_ref[:, :split])
    for c in range(n_pieces):
        nxt = _dot(a, b_ref[:, (c + 1) * split:(c + 2) * split]) if c + 1 < n_pieces else None
        epilogue(acc, extras, o_ref, slice(c * split, (c + 1) * split))
        acc = nxt


def _ep_store(acc, extras, o_ref, cols):
    o_ref[:, cols] = acc.astype(o_ref.dtype)


def _matmul(a, b, *, tm, tn, out_dtype, epilogue=_ep_store, extras=(), split=None, name):
    m, kdim = a.shape
    n = b.shape[1]
    split = tn if split is None else split
    assert m % tm == 0 and n % tn == 0 and tn % split == 0
    in_specs = [pl.BlockSpec((tm, kdim), lambda i, j: (i, 0)),
                pl.BlockSpec((kdim, tn), lambda i, j: (0, j))]
    in_specs += [pl.BlockSpec(bs, im) for (_, bs, im) in extras]
    return pl.pallas_call(
        functools.partial(_mm_kernel, n_extra=len(extras), epilogue=epilogue, split=split),
        grid=(m // tm, n // tn),
        in_specs=in_specs,
        out_specs=pl.BlockSpec((tm, tn), lambda i, j: (i, j)),
        out_shape=jax.ShapeDtypeStruct((m, n), out_dtype),
        compiler_params=_params("parallel", "parallel"),
        name=name,
    )(a, b, *[e[0] for e in extras])


def _head_rms(blk, gain):
    ms = jnp.mean(blk * blk, axis=-1, keepdims=True)
    return blk * lax.rsqrt(ms + RMS_EPS) * gain


def _head_slices(cols):
    return [slice(c, c + HEAD_DIM) for c in range(cols.start, cols.stop, HEAD_DIM)]


def _ep_headnorm(acc, extras, o_ref, cols):
    for i, sl in enumerate(_head_slices(cols)):
        blk = acc[:, i * HEAD_DIM:(i + 1) * HEAD_DIM]
        o_ref[:, sl] = _head_rms(blk, extras[0][:, sl]).astype(o_ref.dtype)


def _rotate_pairs(y, cos_t, sin_t, half):
    lane = lax.broadcasted_iota(jnp.int32, y.shape, 1)
    swapped = jnp.where(lane < half, pltpu.roll(y, LANES - half, 1), pltpu.roll(y, half, 1))
    return y * cos_t + swapped * sin_t


def _pair_swap_matrix(width, half):
    p = np.zeros((width, width), np.float32)
    for base in range(0, width, HEAD_DIM):
        for i in range(half):
            p[base + i + half, base + i] = 1.0
            p[base + i, base + i + half] = 1.0
    return jnp.asarray(p, BF16)


def _ep_headnorm_rope(acc, extras, o_ref, cols):
    cos_t = extras[1][...]
    sin_t = extras[2][...]
    heads = _head_slices(cols)
    y = jnp.concatenate([_head_rms(acc[:, i * HEAD_DIM:(i + 1) * HEAD_DIM], extras[0][:, sl])
                         for i, sl in enumerate(heads)], axis=1)
    swapped = _dot(y.astype(BF16), extras[3][...])
    for i, sl in enumerate(heads):
        loc = slice(i * HEAD_DIM, (i + 1) * HEAD_DIM)
        o_ref[:, sl] = (y[:, loc] * cos_t + swapped[:, loc] * sin_t).astype(o_ref.dtype)


def _ep_rownorm(acc, extras, o_ref, cols):
    o_ref[:, cols] = _head_rms(acc, extras[0][:, cols]).astype(o_ref.dtype)


def _ep_sigmoid(acc, extras, o_ref, cols):
    o_ref[:, cols] = jax.nn.sigmoid(acc).astype(o_ref.dtype)


def _ep_residual(acc, extras, o_ref, cols):
    o_ref[:, cols] = extras[0][:, cols] + extras[1][:, cols] * acc


def _rope_tables(seq, rot, fill):
    half = rot // 2
    inv = ROPE_THETA ** (-jnp.arange(half, dtype=F32) * (2.0 / rot))
    ang = jnp.arange(seq, dtype=jnp.int32).astype(F32)[:, None] * inv[None, :]
    cos, sin = jnp.cos(ang), jnp.sin(ang)
    pad = LANES - rot
    cos_t = jnp.concatenate([cos, cos, jnp.full((seq, pad), fill, F32)], axis=1)
    sin_t = jnp.concatenate([-sin, sin, jnp.zeros((seq, pad), F32)], axis=1)
    return cos_t, sin_t


NA_RB = 4
NA_KROWS = NA_KH + NA_RB
NA_GROUP = 8


def _na_geometry(rows):
    kh = min(NA_KH, rows)
    assert rows % (NA_RB * NA_GROUP) == 0 and rows >= NA_KROWS
    patterns, ids, kstarts = [], [], []
    for r in range(0, rows, NA_RB):
        ks = int(np.clip(r - NA_KH // 2, 0, rows - NA_KROWS))
        pat = tuple((int(np.clip(r + a - NA_KH // 2, 0, rows - kh)) - ks, ks - (r + a)) for a in range(NA_RB))
        if pat not in patterns:
            patterns.append(pat)
        ids.append(patterns.index(pat))
        kstarts.append(ks)
    return patterns, np.array(ids + kstarts, np.int32)


def _na_bias_table(rpb, rows, patterns):
    h = rpb.shape[0]
    kh = min(NA_KH, rows)
    qc = np.arange(GRID_W)
    kc = np.arange(GRID_W)
    cstart = np.clip(qc - NA_KW // 2, 0, GRID_W - NA_KW)
    ok = (kc[None, :] >= cstart[:, None]) & (kc[None, :] < cstart[:, None] + NA_KW)
    dc = np.clip(kc[None, :] - qc[:, None] + NA_KW - 1, 0, 2 * NA_KW - 2)
    by_col = rpb.astype(F32)[:, :, dc] * LOG2E + jnp.where(ok, 0.0, NEG_INF).astype(F32)
    masked_row = 2 * NA_KH - 1
    by_col = jnp.concatenate([by_col, jnp.full((h, 1, GRID_W, GRID_W), NEG_INF, F32)], axis=1)
    kr = np.arange(NA_KROWS)
    dr = np.zeros((len(patterns), NA_RB, NA_KROWS), np.int32)
    for p, pat in enumerate(patterns):
        for a, (start, offset) in enumerate(pat):
            row_ok = (kr >= start) & (kr < start + kh)
            dr[p, a] = np.where(row_ok, kr + offset + NA_KH - 1, masked_row)
    tab = jnp.take(by_col, jnp.asarray(dr.reshape(-1)), axis=1)
    tab = tab.reshape(h, len(patterns), NA_RB, NA_KROWS, GRID_W, GRID_W)
    tab = tab.transpose(0, 1, 2, 4, 3, 5)
    return tab.reshape(h, len(patterns), NA_RB * GRID_W, NA_KROWS * GRID_W)


def _na_kernel(geo_ref, q_ref, k_ref, v_ref, b_ref, o_ref, *, n_blocks):
    qn = NA_RB * GRID_W
    kn = NA_KROWS * GRID_W
    ones = jnp.ones((kn, HEAD_DIM), BF16)

    def scores(blk):
        q0 = pl.multiple_of(blk * qn, qn)
        k0 = pl.multiple_of(geo_ref[n_blocks + blk] * GRID_W, GRID_W)
        s = _dot_nt(q_ref[pl.ds(q0, qn), :], k_ref[pl.ds(k0, kn), :]) + b_ref[geo_ref[blk]]
        return s, q0, k0

    def finish(s, q0, k0):
        v1 = jnp.concatenate([v_ref[pl.ds(k0, kn), :], ones], axis=1)
        m = jnp.max(s, axis=-1, keepdims=True)
        acc = _dot(jnp.exp2(s - m).astype(BF16), v1)
        o_ref[pl.ds(q0, qn), :] = (acc[:, :HEAD_DIM] / acc[:, HEAD_DIM:]).astype(o_ref.dtype)

    def body(i, carry):
        pending = [scores(i * NA_GROUP + j) for j in range(NA_GROUP)]
        for item in pending:
            finish(*item)
        return carry

    lax.fori_loop(0, n_blocks // NA_GROUP, body, 0)


def _na_attention(qk, v, rpb, batch, seq):
    rows = seq // GRID_W
    patterns, geo = _na_geometry(rows)
    bias = _na_bias_table(rpb, rows, patterns)
    t = qk.shape[0]
    blk = (seq, HEAD_DIM)
    return pl.pallas_call(
        functools.partial(_na_kernel, n_blocks=rows // NA_RB),
        grid_spec=pltpu.PrefetchScalarGridSpec(
            num_scalar_prefetch=1,
            grid=(batch, NA_HEADS),
            in_specs=[pl.BlockSpec(blk, lambda b, h, geo_ref: (b, h)),
                      pl.BlockSpec(blk, lambda b, h, geo_ref: (b, NA_HEADS + h)),
                      pl.BlockSpec(blk, lambda b, h, geo_ref: (b, h)),
                      pl.BlockSpec((None,) + bias.shape[1:], lambda b, h, geo_ref: (h, 0, 0, 0))],
            out_specs=pl.BlockSpec(blk, lambda b, h, geo_ref: (b, h))),
        out_shape=jax.ShapeDtypeStruct((t, NA_W), BF16),
        compiler_params=_params("parallel", "parallel"),
        name="na_attention",
    )(jnp.asarray(geo), qk, qk, v, bias)


DIL_QCHUNK = 2 * DIL_RADIUS
DIL_KWIN = 4 * DIL_RADIUS
DIL_GROUP = 16


def _dil_window_mask():
    r = np.arange(DIL_QCHUNK)[None, :, None]
    c = np.arange(DIL_KWIN)[None, None, :]
    off = (np.arange(3) * DIL_RADIUS)[:, None, None]
    return jnp.asarray(np.where(np.abs(c - r - off) <= DIL_RADIUS, 0.0, NEG_INF), F32)


def _dil_kernel(q_ref, k_ref, v_ref, mask_ref, o_ref, lse_ref, *, seq, dil):
    length = seq // dil
    n_chunk = length // DIL_QCHUNK
    ones = jnp.ones((DIL_KWIN, HEAD_DIM), BF16)

    def rows(residue, first, count):
        return pl.ds(residue + first * dil, count, stride=dil)

    def scores(it):
        residue = it // n_chunk
        q0 = (it % n_chunk) * DIL_QCHUNK
        k0 = jnp.clip(q0 - DIL_RADIUS, 0, length - DIL_KWIN)
        q = q_ref[rows(residue, q0, DIL_QCHUNK), :].astype(BF16)
        k = k_ref[rows(residue, k0, DIL_KWIN), :].astype(BF16)
        return _dot_nt(q, k) + mask_ref[(q0 - k0) // DIL_RADIUS], residue, q0, k0

    def finish(s, residue, q0, k0):
        v1 = jnp.concatenate([v_ref[rows(residue, k0, DIL_KWIN), :].astype(BF16), ones], axis=1)
        m = jnp.max(s, axis=-1, keepdims=True)
        acc = _dot(jnp.exp2(s - m).astype(BF16), v1)
        den = acc[:, HEAD_DIM:]
        o_ref[rows(residue, q0, DIL_QCHUNK), :] = acc[:, :HEAD_DIM] / den
        lse_ref[rows(residue, q0, DIL_QCHUNK), :] = m * LN2 + jnp.log(den[:, :1])

    def body(i, carry):
        pending = [scores(i * DIL_GROUP + j) for j in range(DIL_GROUP)]
        for item in pending:
            finish(*item)
        return carry

    assert (dil * n_chunk) % DIL_GROUP == 0
    lax.fori_loop(0, dil * n_chunk // DIL_GROUP, body, 0)


def _dil_group(qk, v, group, dil, batch, seq):
    length = seq // dil
    assert length % DIL_QCHUNK == 0 and length >= DIL_KWIN
    hg = DIL_HEADS_PER_GROUP
    blk = (seq, HEAD_DIM)
    mask = _dil_window_mask()
    return pl.pallas_call(
        functools.partial(_dil_kernel, seq=seq, dil=dil),
        grid=(batch, hg),
        in_specs=[pl.BlockSpec(blk, lambda b, h: (b, group * hg + h)),
                  pl.BlockSpec(blk, lambda b, h: (b, DIL_HEADS + group * hg + h)),
                  pl.BlockSpec(blk, lambda b, h: (b, group * hg + h)),
                  pl.BlockSpec(mask.shape, lambda b, h: (0, 0, 0))],
        out_specs=[pl.BlockSpec(blk, lambda b, h: (b, h)),
                   pl.BlockSpec((None, seq, 1), lambda b, h: (b * hg + h, 0, 0))],
        out_shape=[jax.ShapeDtypeStruct((batch * seq, DIL_OUT), F32),
                   jax.ShapeDtypeStruct((batch * hg, seq, 1), F32)],
        compiler_params=_params("parallel", "parallel"),
        name="dilated_attention",
    )(qk, qk, v, mask)


def _dil_mix_kernel(o0_ref, o1_ref, o2_ref, l0_ref, l1_ref, l2_ref, o_ref):
    l0, l1, l2 = l0_ref[...], l1_ref[...], l2_ref[...]
    mx = jnp.maximum(jnp.maximum(l0, l1), l2)
    e0, e1, e2 = jnp.exp(l0 - mx), jnp.exp(l1 - mx), jnp.exp(l2 - mx)
    z = e0 + e1 + e2
    w0, w1, w2 = e0 / z, e1 / z, e2 / z
    for h in range(DIL_HEADS_PER_GROUP):
        sl = slice(h * HEAD_DIM, (h + 1) * HEAD_DIM)
        hs = slice(h, h + 1)
        o_ref[:, sl] = (w0[:, hs] * o0_ref[:, sl] + w1[:, hs] * o1_ref[:, sl]
                        + w2[:, hs] * o2_ref[:, sl]).astype(o_ref.dtype)


def _dil_mix(outs, lses, tm=512):
    t = outs[0].shape[0]
    ospec = pl.BlockSpec((tm, DIL_OUT), lambda i: (i, 0))
    lspec = pl.BlockSpec((tm, DIL_HEADS_PER_GROUP), lambda i: (i, 0))
    return pl.pallas_call(
        _dil_mix_kernel,
        grid=(t // tm,),
        in_specs=[ospec] * 3 + [lspec] * 3,
        out_specs=ospec,
        out_shape=jax.ShapeDtypeStruct((t, DIL_OUT), BF16),
        compiler_params=_params("parallel"),
        name="dilated_mix",
    )(*outs, *lses)


def _dilated_attention(qk, v, batch, seq):
    hg = DIL_HEADS_PER_GROUP
    outs, lses = [], []
    for g, (window, dil) in enumerate(DIL_GROUPS):
        assert window // (2 * dil) == DIL_RADIUS and seq % dil == 0
        o, lse = _dil_group(qk, v, g, dil, batch, seq)
        outs.append(o)
        lses.append(lse.reshape(batch, hg, seq).transpose(0, 2, 1).reshape(batch * seq, hg))
    return _dil_mix(outs, lses)


def _mla_q_kernel(a_ref, w_ref, g_ref, cos_ref, sin_ref, o_ref):
    a = a_ref[...]
    gain = g_ref[...]
    cos_t, sin_t = cos_ref[...], sin_ref[...]
    acc = _dot(a, w_ref[0])
    for h in range(MLA_HEADS):
        nxt = _dot(a, w_ref[h + 1]) if h + 1 < MLA_HEADS else None
        ms = jnp.sum(acc * acc, axis=-1, keepdims=True) * (1.0 / MLA_QK)
        y = acc * lax.rsqrt(ms + RMS_EPS) * gain
        o_ref[h, :, :MLA_NOPE] = y[:, :MLA_NOPE].astype(o_ref.dtype)
        o_ref[h, :, MLA_NOPE:] = _rotate_pairs(y[:, MLA_NOPE:], cos_t, sin_t,
                                               MLA_ROPE // 2).astype(o_ref.dtype)
        acc = nxt


def _mla_q_proj(cq_n, w_uq_h, gain, cos_t, sin_t, seq, tm=512):
    t, rank = cq_n.shape
    per = seq // tm
    return pl.pallas_call(
        _mla_q_kernel,
        grid=(t // tm,),
        in_specs=[pl.BlockSpec((tm, rank), lambda i: (i, 0)),
                  pl.BlockSpec((MLA_HEADS, rank, MLA_QK_PAD), lambda i: (0, 0, 0)),
                  pl.BlockSpec((1, MLA_QK_PAD), lambda i: (0, 0)),
                  pl.BlockSpec((tm, LANES), lambda i: (i % per, 0)),
                  pl.BlockSpec((tm, LANES), lambda i: (i % per, 0))],
        out_specs=pl.BlockSpec((MLA_HEADS, tm, MLA_QK_PAD), lambda i: (0, i, 0)),
        out_shape=jax.ShapeDtypeStruct((MLA_HEADS, t, MLA_QK_PAD), BF16),
        compiler_params=_params("parallel"),
        name="mla_q_proj",
    )(cq_n, w_uq_h, gain, cos_t, sin_t)


def _mla_kv_kernel(a_ref, w_ref, kr_ref, g0_ref, g1_ref, cos_ref, sin_ref, k_ref, v_ref):
    a = a_ref[...]
    kr = kr_ref[...]
    kr_ss = jnp.sum(kr * kr, axis=-1, keepdims=True)
    g0, g1 = g0_ref[...], g1_ref[...]
    cos_t, sin_t = cos_ref[...], sin_ref[...]
    ones = jnp.ones((a.shape[0], MLA_V), v_ref.dtype)
    kr_rot = _rotate_pairs(kr * g1, cos_t, sin_t, MLA_ROPE // 2)
    acc = _dot(a, w_ref[0])
    for h in range(MLA_HEADS):
        nxt = _dot(a, w_ref[h + 1]) if h + 1 < MLA_HEADS else None
        kn = acc[:, :MLA_NOPE]
        ms = (jnp.sum(kn * kn, axis=-1, keepdims=True) + kr_ss) * (1.0 / MLA_QK)
        inv = lax.rsqrt(ms + RMS_EPS)
        k_ref[h, :, :MLA_NOPE] = (kn * inv * g0).astype(k_ref.dtype)
        k_ref[h, :, MLA_NOPE:] = (kr_rot * inv).astype(k_ref.dtype)
        v_ref[h, :, :MLA_V] = acc[:, MLA_NOPE:].astype(v_ref.dtype)
        v_ref[h, :, MLA_V:] = ones
        acc = nxt


def _mla_kv_proj(ckv_n, w_ukv_h, k_r, g0, g1, cos_t, sin_t, seq, tm=1024):
    t, rank = ckv_n.shape
    per = seq // tm
    return pl.pallas_call(
        _mla_kv_kernel,
        grid=(t // tm,),
        in_specs=[pl.BlockSpec((tm, rank), lambda i: (i, 0)),
                  pl.BlockSpec((MLA_HEADS, rank, MLA_NOPE + MLA_V), lambda i: (0, 0, 0)),
                  pl.BlockSpec((tm, LANES), lambda i: (i, 0)),
                  pl.BlockSpec((1, LANES), lambda i: (0, 0)),
                  pl.BlockSpec((1, LANES), lambda i: (0, 0)),
                  pl.BlockSpec((tm, LANES), lambda i: (i % per, 0)),
                  pl.BlockSpec((tm, LANES), lambda i: (i % per, 0))],
        out_specs=[pl.BlockSpec((MLA_HEADS, tm, MLA_QK_PAD), lambda i: (0, i, 0)),
                   pl.BlockSpec((MLA_HEADS, tm, 2 * MLA_V), lambda i: (0, i, 0))],
        out_shape=[jax.ShapeDtypeStruct((MLA_HEADS, t, MLA_QK_PAD), BF16),
                   jax.ShapeDtypeStruct((MLA_HEADS, t, 2 * MLA_V), BF16)],
        compiler_params=_params("parallel"),
        name="mla_kv_proj",
    )(ckv_n, w_ukv_h, k_r, g0, g1, cos_t, sin_t)


MLA_KV_CHUNKS = 8
MLA_QK_AHEAD = 1


def _mla_attn_kernel(q_ref, k_ref, v_ref, o_ref):
    q = q_ref[...]
    tk = k_ref.shape[0] // MLA_KV_CHUNKS
    m = acc = None
    ahead = [_dot_nt(q, k_ref[i * tk:(i + 1) * tk, :]) for i in range(MLA_QK_AHEAD)]
    for c in range(MLA_KV_CHUNKS):
        s = ahead.pop(0)
        nxt = c + MLA_QK_AHEAD
        if nxt < MLA_KV_CHUNKS:
            ahead.append(_dot_nt(q, k_ref[nxt * tk:(nxt + 1) * tk, :]))
        m_c = jnp.max(s, axis=-1, keepdims=True)
        if c == 0:
            m = m_c
            acc = _dot(jnp.exp2(s - m).astype(BF16), v_ref[:tk, :])
        else:
            m_new = jnp.maximum(m, m_c)
            acc = jnp.exp2(m - m_new) * acc + _dot(jnp.exp2(s - m_new).astype(BF16),
                                                   v_ref[c * tk:(c + 1) * tk, :])
            m = m_new
    o_ref[...] = (acc[:, :MLA_V] / acc[:, MLA_V:]).astype(o_ref.dtype)


def _mla_attention(q, k, v, batch, seq, tq=1024):
    t = q.shape[1]
    nq = seq // tq
    return pl.pallas_call(
        _mla_attn_kernel,
        grid=(batch, MLA_HEADS, nq),
        in_specs=[pl.BlockSpec((None, tq, MLA_QK_PAD), lambda b, h, i: (h, b * nq + i, 0)),
                  pl.BlockSpec((None, seq, MLA_QK_PAD), lambda b, h, i: (h, b, 0)),
                  pl.BlockSpec((None, seq, 2 * MLA_V), lambda b, h, i: (h, b, 0))],
        out_specs=pl.BlockSpec((tq, MLA_V), lambda b, h, i: (b * nq + i, h)),
        out_shape=jax.ShapeDtypeStruct((t, MLA_HEADS * MLA_V), BF16),
        compiler_params=_params("parallel", "parallel", "arbitrary"),
        name="mla_attention",
    )(q, k, v)


def _merge_kernel(ona_ref, odl_ref, omla_ref, wna_ref, wdl_ref, wmla_ref, g0_ref, g1_ref, g2_ref, o_ref):
    ona, odl, omla = ona_ref[...], odl_ref[...], omla_ref[...]
    for c in range(o_ref.shape[1] // MXU_COLS):
        cols = slice(c * MXU_COLS, (c + 1) * MXU_COLS)
        acc = g0_ref[:, cols].astype(F32) * _dot(ona, wna_ref[:, cols])
        acc = acc + g1_ref[:, cols].astype(F32) * _dot(odl, wdl_ref[:, cols])
        acc = acc + g2_ref[:, cols].astype(F32) * _dot(omla, wmla_ref[:, cols])
        o_ref[:, cols] = acc.astype(o_ref.dtype)


def _merge(o_na, o_dl, o_mla, w_na, w_dl, w_mla, gates, tm=1024, tn=512):
    t = o_na.shape[0]
    d = w_na.shape[1]
    nj = d // tn

    def act(a):
        return pl.BlockSpec((tm, a.shape[1]), lambda i, j: (i, 0))

    def wgt(w):
        return pl.BlockSpec((w.shape[0], tn), lambda i, j: (0, j))

    def gate(idx):
        return pl.BlockSpec((tm, tn), lambda i, j: (i, idx * nj + j))

    return pl.pallas_call(
        _merge_kernel,
        grid=(t // tm, nj),
        in_specs=[act(o_na), act(o_dl), act(o_mla), wgt(w_na), wgt(w_dl), wgt(w_mla),
                  gate(0), gate(1), gate(2)],
        out_specs=pl.BlockSpec((tm, tn), lambda i, j: (i, j)),
        out_shape=jax.ShapeDtypeStruct((t, d), BF16),
        compiler_params=_params("parallel", "parallel"),
        name="branch_merge",
    )(o_na, o_dl, o_mla, w_na, w_dl, w_mla, gates, gates, gates)


def _swiglu_pieces(h, wg_ref, wu_ref, o_ref):
    for c in range(o_ref.shape[1] // MXU_COLS):
        cols = slice(c * MXU_COLS, (c + 1) * MXU_COLS)
        g = _dot(h, wg_ref[:, cols].astype(BF16))
        u = _dot(h, wu_ref[:, cols].astype(BF16))
        o_ref[:, cols] = (g * jax.nn.sigmoid(g) * u).astype(o_ref.dtype)


def _gu_kernel(h_ref, wg_ref, wu_ref, o_ref):
    _swiglu_pieces(h_ref[...], wg_ref, wu_ref, o_ref)


def _swiglu_up(h, w_gu, tm=1024, tn=512):
    t, d = h.shape
    ff = w_gu.shape[1] // 2
    nj = ff // tn
    return pl.pallas_call(
        _gu_kernel,
        grid=(t // tm, nj),
        in_specs=[pl.BlockSpec((tm, d), lambda i, j: (i, 0)),
                  pl.BlockSpec((d, tn), lambda i, j: (0, j)),
                  pl.BlockSpec((d, tn), lambda i, j: (0, nj + j))],
        out_specs=pl.BlockSpec((tm, tn), lambda i, j: (i, j)),
        out_shape=jax.ShapeDtypeStruct((t, ff), BF16),
        compiler_params=_params("parallel", "parallel"),
        name="swiglu_up",
    )(h, w_gu, w_gu)


TOP_K = 2
MOE_TILE_ROWS = 1024


def _moe_routing(route, tokens, tm):
    e_flat = jnp.concatenate([route[:, N_EXPERTS], route[:, N_EXPERTS + 1]]).astype(jnp.int32)
    onehot = (e_flat[:, None] == jnp.arange(N_EXPERTS, dtype=jnp.int32)[None, :]).astype(jnp.int32)
    csum = jnp.cumsum(onehot, axis=0)
    rank = jnp.sum((csum - onehot) * onehot, axis=1)
    padded = ((csum[-1] + tm - 1) // tm) * tm
    ends = jnp.cumsum(padded)
    slot = jnp.sum(onehot * (ends - padded)[None, :], axis=1) + rank
    n_tiles = (TOP_K * tokens) // tm + N_EXPERTS
    tile_start = jnp.arange(n_tiles, dtype=jnp.int32) * tm
    tile_expert = jnp.minimum(jnp.sum(tile_start[:, None] >= ends[None, :], axis=1), N_EXPERTS - 1)
    te = jnp.concatenate([tile_expert, ends[-1:] // tm]).astype(jnp.int32)
    return slot.astype(jnp.int32), te


def _dispatch_kernel(slot_ref, x_ref, g_ref, sc_ref, sh_ref, dst_in_ref, dst_ref, h_ref, sem, *, rows, tokens):
    del dst_in_ref
    base = pl.program_id(0) * rows
    h_ref[...] = _modulated_norm(x_ref[...], g_ref[...], sc_ref[...], sh_ref[...])

    def row_copy(r, choice):
        slot = slot_ref[choice * tokens + base + r]
        return pltpu.make_async_copy(h_ref.at[pl.ds(r, 1), :], dst_ref.at[pl.ds(slot, 1), :], sem)

    def start(r, carry):
        row_copy(r, 0).start()
        row_copy(r, 1).start()
        return carry

    def wait(r, carry):
        row_copy(r, 0).wait()
        row_copy(r, 1).wait()
        return carry

    lax.fori_loop(0, rows, start, 0, unroll=8)
    lax.fori_loop(0, rows, wait, 0, unroll=8)


def _moe_dispatch(x, g, sc, sh, slot, seq, n_rows, rows=256):
    t, d = x.shape
    per = seq // rows
    return pl.pallas_call(
        functools.partial(_dispatch_kernel, rows=rows, tokens=t),
        grid_spec=pltpu.PrefetchScalarGridSpec(
            num_scalar_prefetch=1,
            grid=(t // rows,),
            in_specs=[pl.BlockSpec((rows, d), lambda i, s: (i, 0)),
                      pl.BlockSpec((1, d), lambda i, s: (0, 0)),
                      pl.BlockSpec((None, 1, d), lambda i, s: (i // per, 0, 0)),
                      pl.BlockSpec((None, 1, d), lambda i, s: (i // per, 0, 0)),
                      pl.BlockSpec(memory_space=pl.ANY)],
            out_specs=pl.BlockSpec(memory_space=pl.ANY),
            scratch_shapes=[pltpu.VMEM((rows, d), F32), pltpu.SemaphoreType.DMA(())]),
        out_shape=jax.ShapeDtypeStruct((n_rows, d), F32),
        input_output_aliases={5: 0},
        compiler_params=_params("arbitrary"),
        name="moe_dispatch",
    )(slot, x, g.reshape(1, d), sc, sh, jnp.zeros((n_rows, d), F32))


def _moe_up_kernel(te_ref, xs_ref, wg_ref, wu_ref, o_ref, hb_ref, *, n_tiles):
    used = pl.program_id(0) < te_ref[n_tiles]

    @pl.when(used & (pl.program_id(1) == 0))
    def _():
        hb_ref[...] = xs_ref[...].astype(BF16)

    @pl.when(used)
    def _():
        _swiglu_pieces(hb_ref[...], wg_ref, wu_ref, o_ref)

    @pl.when(jnp.logical_not(used))
    def _():
        o_ref[...] = jnp.zeros(o_ref.shape, o_ref.dtype)


def _moe_up(xs, w_gu, te, tn=512):
    n_rows, d = xs.shape
    ff = w_gu.shape[2] // 2
    nj = ff // tn
    n_tiles = te.shape[0] - 1
    tm = n_rows // n_tiles

    def last_used(i, te_ref):
        return jnp.minimum(i, te_ref[n_tiles] - 1)

    return pl.pallas_call(
        functools.partial(_moe_up_kernel, n_tiles=n_tiles),
        grid_spec=pltpu.PrefetchScalarGridSpec(
            num_scalar_prefetch=1,
            grid=(n_tiles, nj),
            in_specs=[pl.BlockSpec((tm, d), lambda i, j, te_ref: (last_used(i, te_ref), 0)),
                      pl.BlockSpec((None, d, tn), lambda i, j, te_ref: (te_ref[i], 0, j)),
                      pl.BlockSpec((None, d, tn), lambda i, j, te_ref: (te_ref[i], 0, nj + j))],
            out_specs=pl.BlockSpec((tm, tn), lambda i, j, te_ref: (i, j)),
            scratch_shapes=[pltpu.VMEM((tm, d), BF16)]),
        out_shape=jax.ShapeDtypeStruct((n_rows, ff), BF16),
        compiler_params=_params("parallel", "arbitrary"),
        name="moe_up",
    )(te, xs, w_gu, w_gu)


def _moe_down_kernel(te_ref, a_ref, w_ref, o_ref, *, n_tiles):
    used = pl.program_id(0) < te_ref[n_tiles]

    @pl.when(used)
    def _():
        o_ref[...] = _dot(a_ref[...], w_ref[...])

    @pl.when(jnp.logical_not(used))
    def _():
        o_ref[...] = jnp.zeros(o_ref.shape, o_ref.dtype)


def _moe_down(act, w_down, te, tn=256):
    n_rows, ff = act.shape
    d = w_down.shape[2]
    n_tiles = te.shape[0] - 1
    tm = n_rows // n_tiles

    def last_used(i, te_ref):
        return jnp.minimum(i, te_ref[n_tiles] - 1)

    return pl.pallas_call(
        functools.partial(_moe_down_kernel, n_tiles=n_tiles),
        grid_spec=pltpu.PrefetchScalarGridSpec(
            num_scalar_prefetch=1,
            grid=(n_tiles, d // tn),
            in_specs=[pl.BlockSpec((tm, ff), lambda i, j, te_ref: (last_used(i, te_ref), 0)),
                      pl.BlockSpec((None, ff, tn), lambda i, j, te_ref: (te_ref[i], 0, j))],
            out_specs=pl.BlockSpec((tm, tn), lambda i, j, te_ref: (i, j))),
        out_shape=jax.ShapeDtypeStruct((n_rows, d), F32),
        compiler_params=_params("parallel", "arbitrary"),
        name="moe_down",
    )(te, act, w_down)


def _combine_kernel(slot_ref, x_ref, gate_ref, w_ref, ys_ref, o_ref, buf_ref, sem, *, rows, tokens):
    base = pl.program_id(0) * rows

    def row_copy(r, choice):
        slot = slot_ref[choice * tokens + base + r]
        return pltpu.make_async_copy(ys_ref.at[pl.ds(slot, 1), :], buf_ref.at[choice, pl.ds(r, 1), :], sem)

    def start(r, carry):
        row_copy(r, 0).start()
        row_copy(r, 1).start()
        return carry

    def wait(r, carry):
        row_copy(r, 0).wait()
        row_copy(r, 1).wait()
        return carry

    lax.fori_loop(0, rows, start, 0, unroll=8)
    lax.fori_loop(0, rows, wait, 0, unroll=8)
    w = w_ref[...]
    y = w[:, 0:1] * buf_ref[0] + w[:, 1:2] * buf_ref[1]
    o_ref[...] = x_ref[...] + gate_ref[...] * y


def _moe_combine(x, gate, w12, ys, slot, seq, rows=256):
    t, d = x.shape
    per = seq // rows
    return pl.pallas_call(
        functools.partial(_combine_kernel, rows=rows, tokens=t),
        grid_spec=pltpu.PrefetchScalarGridSpec(
            num_scalar_prefetch=1,
            grid=(t // rows,),
            in_specs=[pl.BlockSpec((rows, d), lambda i, s: (i, 0)),
                      pl.BlockSpec((None, 1, d), lambda i, s: (i // per, 0, 0)),
                      pl.BlockSpec((rows, TOP_K), lambda i, s: (i, 0)),
                      pl.BlockSpec(memory_space=pl.ANY)],
            out_specs=pl.BlockSpec((rows, d), lambda i, s: (i, 0)),
            scratch_shapes=[pltpu.VMEM((TOP_K, rows, d), F32), pltpu.SemaphoreType.DMA(())]),
        out_shape=jax.ShapeDtypeStruct((t, d), F32),
        compiler_params=_params("arbitrary"),
        name="moe_combine",
    )(slot, x, gate, w12, ys)


def _router_kernel(x_ref, g_ref, sc_ref, sh_ref, w_ref, o_ref):
    h = _modulated_norm(x_ref[...], g_ref[...], sc_ref[...], sh_ref[...])
    logits = jnp.dot(h, w_ref[...], preferred_element_type=F32, precision=lax.Precision.HIGHEST)
    lane = lax.broadcasted_iota(jnp.int32, logits.shape, 1).astype(F32)
    lg = jnp.where(lane < N_EXPERTS, logits, NEG_INF)
    m1 = jnp.max(lg, axis=-1, keepdims=True)
    i1 = jnp.min(jnp.where(lg == m1, lane, float(LANES)), axis=-1, keepdims=True)
    lg2 = jnp.where(lane == i1, NEG_INF, lg)
    m2 = jnp.max(lg2, axis=-1, keepdims=True)
    i2 = jnp.min(jnp.where(lg2 == m2, lane, float(LANES)), axis=-1, keepdims=True)
    e2 = jnp.exp(m2 - m1)
    z = 1.0 + e2
    out = jnp.where(lane == N_EXPERTS, i1, 0.0) + jnp.where(lane == N_EXPERTS + 1, i2, 0.0)
    out = out + jnp.where(lane == N_EXPERTS + 2, 1.0 / z, 0.0) + jnp.where(lane == N_EXPERTS + 3, e2 / z, 0.0)
    o_ref[...] = out


def _router(x, g, sc, sh, w_router_pad, seq, tm=512):
    t, d = x.shape
    per = seq // tm
    return pl.pallas_call(
        _router_kernel,
        grid=(t // tm,),
        in_specs=[pl.BlockSpec((tm, d), lambda i: (i, 0)),
                  pl.BlockSpec((1, d), lambda i: (0, 0)),
                  pl.BlockSpec((None, 1, d), lambda i: (i // per, 0, 0)),
                  pl.BlockSpec((None, 1, d), lambda i: (i // per, 0, 0)),
                  pl.BlockSpec((d, LANES), lambda i: (0, 0))],
        out_specs=pl.BlockSpec((tm, LANES), lambda i: (i, 0)),
        out_shape=jax.ShapeDtypeStruct((t, LANES), F32),
        compiler_params=_params("parallel"),
        name="router",
    )(x, g.reshape(1, d), sc, sh, w_router_pad)


MM_TM = 1024
MM_TN = 1024


def _col_tile(n):
    return MM_TN if n % MM_TN == 0 else MM_TN // 2


def _residual_matmul(a, w, x, gate, seq, name, tm=MM_TM, tn=None):
    per = seq // tm
    tn = _col_tile(w.shape[1]) if tn is None else tn
    return _matmul(a, w, tm=tm, tn=tn, out_dtype=F32, epilogue=_ep_residual, split=MXU_COLS,
                   extras=[(x, (tm, tn), lambda i, j: (i, j)),
                           (gate, (None, 1, tn), lambda i, j: (i // per, 0, j))],
                   name=name)


def _token_mixer(h, x, gate, lw, batch, seq, tables):
    d = h.shape[1]
    w_in = lw["w_in"]
    tm = MM_TM
    per = seq // tm
    cos_d, sin_d, cos_m, sin_m = tables

    def cols(lo, hi):
        return w_in[:, lo:hi].astype(BF16)

    def tile_gain(gq, gk, heads):
        gq = gq * (LOG2E / math.sqrt(HEAD_DIM))
        return jnp.concatenate([jnp.tile(gq, heads), jnp.tile(gk, heads)]).reshape(1, -1).astype(F32)

    o = 0
    tn = _col_tile(2 * NA_W)
    na_qk = _matmul(h, cols(o, o + 2 * NA_W), tm=tm, tn=tn, out_dtype=BF16, epilogue=_ep_headnorm,
                    split=MXU_COLS,
                    extras=[(tile_gain(lw["na_q_norm"], lw["na_k_norm"], NA_HEADS), (1, tn),
                             lambda i, j: (0, j))], name="na_qk_proj")
    o += 2 * NA_W
    na_v = _matmul(h, cols(o, o + NA_W), tm=tm, tn=_col_tile(NA_W), out_dtype=BF16, split=MXU_COLS, name="na_v_proj")
    o += NA_W
    tn = _col_tile(2 * DIL_W)
    dl_qk = _matmul(h, cols(o, o + 2 * DIL_W), tm=tm, tn=tn, out_dtype=F32, epilogue=_ep_headnorm_rope,
                    split=MXU_COLS,
                    extras=[(tile_gain(lw["dil_q_norm"], lw["dil_k_norm"], DIL_HEADS), (1, tn),
                             lambda i, j: (0, j)),
                            (cos_d, (tm, LANES), lambda i, j: (i % per, 0)),
                            (sin_d, (tm, LANES), lambda i, j: (i % per, 0)),
                            (_pair_swap_matrix(MXU_COLS, ROT_DIM // 2), (MXU_COLS, MXU_COLS),
                             lambda i, j: (0, 0))], name="dil_qk_proj")
    o += 2 * DIL_W
    dl_v = _matmul(h, cols(o, o + DIL_W), tm=tm, tn=_col_tile(DIL_W), out_dtype=F32, split=MXU_COLS, name="dil_v_proj")
    o += DIL_W
    cq_n = _matmul(h, cols(o, o + MLA_Q_RANK), tm=tm, tn=MLA_Q_RANK, out_dtype=BF16, epilogue=_ep_rownorm,
                   extras=[(lw["mla_q_a_norm"].reshape(1, -1), (1, MLA_Q_RANK), lambda i, j: (0, 0))],
                   name="mla_cq_proj")
    o += MLA_Q_RANK
    ckv_n = _matmul(h, cols(o, o + MLA_KV_RANK), tm=tm, tn=MLA_KV_RANK, out_dtype=BF16, epilogue=_ep_rownorm,
                    extras=[(lw["mla_kv_a_norm"].reshape(1, -1), (1, MLA_KV_RANK), lambda i, j: (0, 0))],
                    name="mla_ckv_proj")
    o += MLA_KV_RANK
    w_kr = jnp.pad(w_in[:, o:o + MLA_ROPE], ((0, 0), (0, LANES - MLA_ROPE))).astype(BF16)
    k_r = _matmul(h, w_kr, tm=tm, tn=LANES, out_dtype=F32, name="mla_kr_proj")
    o += MLA_ROPE
    gates = _matmul(h, cols(o, o + 3 * d), tm=tm, tn=_col_tile(3 * d), out_dtype=BF16, epilogue=_ep_sigmoid,
                    split=MXU_COLS, name="gate_proj")

    o_na = _na_attention(na_qk, na_v, lw["na_rpb"], batch, seq)
    o_dl = _dilated_attention(dl_qk, dl_v, batch, seq)
    w_uq = lw["mla_w_uq"].reshape(MLA_Q_RANK, MLA_HEADS, MLA_QK).transpose(1, 0, 2)
    w_uq = jnp.pad(w_uq, ((0, 0), (0, 0), (0, MLA_QK_PAD - MLA_QK))).astype(BF16)
    w_ukv = lw["mla_w_ukv"].reshape(MLA_KV_RANK, MLA_HEADS, MLA_NOPE + MLA_V).transpose(1, 0, 2).astype(BF16)
    gq = jnp.pad(lw["mla_q_norm"] * (LOG2E / math.sqrt(MLA_QK)), (0, MLA_QK_PAD - MLA_QK)).reshape(1, MLA_QK_PAD)
    gk = lw["mla_k_norm"]
    gk0 = gk[:MLA_NOPE].reshape(1, LANES)
    gk1 = jnp.pad(gk[MLA_NOPE:], (0, LANES - MLA_ROPE)).reshape(1, LANES)
    q_m = _mla_q_proj(cq_n, w_uq, gq, cos_m, sin_m, seq)
    k_m, v_m = _mla_kv_proj(ckv_n, w_ukv, k_r, gk0, gk1, cos_m, sin_m, seq)
    o_mla = _mla_attention(q_m, k_m, v_m, batch, seq)

    merged = _merge(o_na, o_dl, o_mla, lw["w_branch_na"].astype(BF16), lw["w_branch_dil"].astype(BF16),
                    lw["w_branch_mla"].astype(BF16), gates)
    return _residual_matmul(merged, lw["w_out"].astype(BF16), x, gate, seq, "out_proj")


def kernel(x, c, w_ada, b_ada, norm_mix, norm_ffn, w_in, na_q_norm, na_k_norm, na_rpb, dil_q_norm, dil_k_norm, mla_q_a_norm, mla_w_uq, mla_kv_a_norm, mla_w_ukv, mla_q_norm, mla_k_norm, w_branch_na, w_branch_dil, w_branch_mla, w_out, ffn_w_gu, ffn_w_down, moe_router, moe_w_gu, moe_w_down):
    batch, seq, d = x.shape
    depth = w_ada.shape[0]
    assert seq % MM_TM == 0 and seq % GRID_W == 0
    t = batch * seq
    xf = x.reshape(t, d)

    c_pad = jnp.pad(c, ((0, (-batch) % 8), (0, 0)))
    mod = _adaln(c_pad, w_ada, b_ada)[:, :batch].reshape(depth, batch, 6, 1, d)
    tables = _rope_tables(seq, ROT_DIM, 1.0) + _rope_tables(seq, MLA_ROPE, 1.0)

    for l in range(depth):
        sh1, sc1, g1, sh2, sc2, g2 = [mod[l, :, i] for i in range(6)]
        lw = dict(w_in=w_in[l], na_q_norm=na_q_norm[l], na_k_norm=na_k_norm[l], na_rpb=na_rpb[l],
                  dil_q_norm=dil_q_norm[l], dil_k_norm=dil_k_norm[l], mla_q_a_norm=mla_q_a_norm[l],
                  mla_w_uq=mla_w_uq[l], mla_kv_a_norm=mla_kv_a_norm[l], mla_w_ukv=mla_w_ukv[l],
                  mla_q_norm=mla_q_norm[l], mla_k_norm=mla_k_norm[l], w_branch_na=w_branch_na[l],
                  w_branch_dil=w_branch_dil[l], w_branch_mla=w_branch_mla[l], w_out=w_out[l])
        h = _normmod(xf, norm_mix[l], sc1, sh1, seq)
        xf = _token_mixer(h, xf, g1, lw, batch, seq, tables)

        if l % 2 == 0:
            h = _normmod(xf, norm_ffn[l], sc2, sh2, seq)
            act = _swiglu_up(h, ffn_w_gu[l // 2].astype(BF16))
            xf = _residual_matmul(act, ffn_w_down[l // 2].astype(BF16), xf, g2, seq, "down_proj", tm=1024, tn=256)
        else:
            w_r = jnp.pad(moe_router[l // 2], ((0, 0), (0, LANES - N_EXPERTS)))
            route = _router(xf, norm_ffn[l], sc2, sh2, w_r, seq)
            slot, te = _moe_routing(route, t, MOE_TILE_ROWS)
            n_rows = (te.shape[0] - 1) * MOE_TILE_ROWS
            xs = _moe_dispatch(xf, norm_ffn[l], sc2, sh2, slot, seq, n_rows)
            act = _moe_up(xs, moe_w_gu[l // 2], te)
            ys = _moe_down(act, moe_w_down[l // 2].astype(BF16), te)
            xf = _moe_combine(xf, g2, route[:, N_EXPERTS + 2:N_EXPERTS + 4], ys, slot, seq)
    return xf.reshape(batch, seq, d)
```

```python
import functools
import math

import numpy as np
import jax
import jax.numpy as jnp
from jax import lax
from jax.experimental import pallas as pl
from jax.experimental.pallas import tpu as pltpu

GRID_W = 64
HEAD_DIM = 128
ROPE_THETA = 500000.0
ROT_DIM = HEAD_DIM // 4
RMS_EPS = 1e-6
NEG_INF = -1e30
NA_HEADS = 8
NA_KH = 8
NA_KW = 16
DIL_GROUPS = ((128, 1), (512, 4), (2048, 16))
DIL_HEADS_PER_GROUP = 4
DIL_HEADS = DIL_HEADS_PER_GROUP * len(DIL_GROUPS)
DIL_RADIUS = 64
MLA_HEADS = 8
MLA_NOPE = 128
MLA_ROPE = 64
MLA_V = 128
MLA_Q_RANK = 768
MLA_KV_RANK = 512
MLA_QK = MLA_NOPE + MLA_ROPE
MLA_QK_PAD = 256
N_EXPERTS = 8
LANES = 128
MXU_COLS = 256

NA_W = NA_HEADS * HEAD_DIM
DIL_W = DIL_HEADS * HEAD_DIM
DIL_OUT = DIL_HEADS_PER_GROUP * HEAD_DIM

VMEM_LIMIT_BYTES = 48 * 1024 * 1024

F32 = jnp.float32
BF16 = jnp.bfloat16
LOG2E = math.log2(math.e)
LN2 = math.log(2.0)


def _params(*sem):
    return pltpu.CompilerParams(dimension_semantics=sem, vmem_limit_bytes=VMEM_LIMIT_BYTES)


def _dot(a, b):
    return jnp.dot(a, b, preferred_element_type=F32)


def _dot_nt(a, b):
    return lax.dot_general(a, b, (((1,), (1,)), ((), ())), preferred_element_type=F32)


def _adaln_kernel(c_ref, w_ref, b_ref, o_ref):
    o_ref[...] = _dot(c_ref[...], w_ref[...]) + b_ref[...]


def _adaln(c_pad, w_ada, b_ada):
    depth, d, n = w_ada.shape
    tn = 1024
    return pl.pallas_call(
        _adaln_kernel,
        grid=(depth, n // tn),
        in_specs=[pl.BlockSpec(c_pad.shape, lambda l, j: (0, 0)),
                  pl.BlockSpec((None, d, tn), lambda l, j: (l, 0, j)),
                  pl.BlockSpec((None, 1, tn), lambda l, j: (l, 0, j))],
        out_specs=pl.BlockSpec((None, c_pad.shape[0], tn), lambda l, j: (l, 0, j)),
        out_shape=jax.ShapeDtypeStruct((depth, c_pad.shape[0], n), F32),
        compiler_params=_params("parallel", "parallel"),
        name="adaln",
    )(c_pad, w_ada, b_ada.reshape(depth, 1, n))


def _modulated_norm(x, g, sc, sh):
    ms = jnp.mean(x * x, axis=-1, keepdims=True)
    return (x * lax.rsqrt(ms + RMS_EPS) * g) * (1.0 + sc) + sh


def _normmod_kernel(x_ref, g_ref, sc_ref, sh_ref, o_ref):
    o_ref[...] = _modulated_norm(x_ref[...], g_ref[...], sc_ref[...], sh_ref[...]).astype(o_ref.dtype)


def _normmod(x, g, sc, sh, seq, tm=512):
    t, d = x.shape
    per = seq // tm
    return pl.pallas_call(
        _normmod_kernel,
        grid=(t // tm,),
        in_specs=[pl.BlockSpec((tm, d), lambda i: (i, 0)),
                  pl.BlockSpec((1, d), lambda i: (0, 0)),
                  pl.BlockSpec((None, 1, d), lambda i: (i // per, 0, 0)),
                  pl.BlockSpec((None, 1, d), lambda i: (i // per, 0, 0))],
        out_specs=pl.BlockSpec((tm, d), lambda i: (i, 0)),
        out_shape=jax.ShapeDtypeStruct((t, d), BF16),
        compiler_params=_params("parallel"),
        name="normmod",
    )(x, g.reshape(1, d), sc, sh)


def _mm_kernel(*refs, n_extra, epilogue, split):
    a_ref, b_ref = refs[0], refs[1]
    extras = refs[2:2 + n_extra]
    o_ref = refs[2 + n_extra]
    a = a_ref[...]
    n_pieces = b_ref.shape[1] // split
    acc = _dot(a, b_ref[:, :split])
    for c in range(n_pieces):
        nxt = _dot(a, b_ref[:, (c + 1) * split:(c + 2) * split]) if c + 1 < n_pieces else None
        epilogue(acc, extras, o_ref, slice(c * split, (c + 1) * split))
        acc = nxt


def _ep_store(acc, extras, o_ref, cols):
    o_ref[:, cols] = acc.astype(o_ref.dtype)


def _matmul(a, b, *, tm, tn, out_dtype, epilogue=_ep_store, extras=(), split=None, name):
    m, kdim = a.shape
    n = b.shape[1]
    split = tn if split is None else split
    assert m % tm == 0 and n % tn == 0 and tn % split == 0
    in_specs = [pl.BlockSpec((tm, kdim), lambda i, j: (i, 0)),
                pl.BlockSpec((kdim, tn), lambda i, j: (0, j))]
    in_specs += [pl.BlockSpec(bs, im) for (_, bs, im) in extras]
    return pl.pallas_call(
        functools.partial(_mm_kernel, n_extra=len(extras), epilogue=epilogue, split=split),
        grid=(m // tm, n // tn),
        in_specs=in_specs,
        out_specs=pl.BlockSpec((tm, tn), lambda i, j: (i, j)),
        out_shape=jax.ShapeDtypeStruct((m, n), out_dtype),
        compiler_params=_params("parallel", "parallel"),
        name=name,
    )(a, b, *[e[0] for e in extras])


def _head_rms(blk, gain):
    ms = jnp.mean(blk * blk, axis=-1, keepdims=True)
    return blk * lax.rsqrt(ms + RMS_EPS) * gain


def _head_slices(cols):
    return [slice(c, c + HEAD_DIM) for c in range(cols.start, cols.stop, HEAD_DIM)]


def _ep_headnorm(acc, extras, o_ref, cols):
    for i, sl in enumerate(_head_slices(cols)):
        blk = acc[:, i * HEAD_DIM:(i + 1) * HEAD_DIM]
        o_ref[:, sl] = _head_rms(blk, extras[0][:, sl]).astype(o_ref.dtype)


def _rotate_pairs(y, cos_t, sin_t, half):
    lane = lax.broadcasted_iota(jnp.int32, y.shape, 1)
    swapped = jnp.where(lane < half, pltpu.roll(y, LANES - half, 1), pltpu.roll(y, half, 1))
    return y * cos_t + swapped * sin_t


def _pair_swap_matrix(width, half):
    p = np.zeros((width, width), np.float32)
    for base in range(0, width, HEAD_DIM):
        for i in range(half):
            p[base + i + half, base + i] = 1.0
            p[base + i, base + i + half] = 1.0
    return jnp.asarray(p, BF16)


def _ep_headnorm_rope(acc, extras, o_ref, cols):
    cos_t = extras[1][...]
    sin_t = extras[2][...]
    heads = _head_slices(cols)
    y = jnp.concatenate([_head_rms(acc[:, i * HEAD_DIM:(i + 1) * HEAD_DIM], extras[0][:, sl])
                         for i, sl in enumerate(heads)], axis=1)
    swapped = _dot(y.astype(BF16), extras[3][...])
    for i, sl in enumerate(heads):
        loc = slice(i * HEAD_DIM, (i + 1) * HEAD_DIM)
        o_ref[:, sl] = (y[:, loc] * cos_t + swapped[:, loc] * sin_t).astype(o_ref.dtype)


def _ep_rownorm(acc, extras, o_ref, cols):
    o_ref[:, cols] = _head_rms(acc, extras[0][:, cols]).astype(o_ref.dtype)


def _ep_sigmoid(acc, extras, o_ref, cols):
    o_ref[:, cols] = jax.nn.sigmoid(acc).astype(o_ref.dtype)


def _ep_residual(acc, extras, o_ref, cols):
    o_ref[:, cols] = extras[0][:, cols] + extras[1][:, cols] * acc


def _rope_tables(seq, rot, fill):
    half = rot // 2
    inv = ROPE_THETA ** (-jnp.arange(half, dtype=F32) * (2.0 / rot))
    ang = jnp.arange(seq, dtype=jnp.int32).astype(F32)[:, None] * inv[None, :]
    cos, sin = jnp.cos(ang), jnp.sin(ang)
    pad = LANES - rot
    cos_t = jnp.concatenate([cos, cos, jnp.full((seq, pad), fill, F32)], axis=1)
    sin_t = jnp.concatenate([-sin, sin, jnp.zeros((seq, pad), F32)], axis=1)
    return cos_t, sin_t


NA_RB = 4
NA_KROWS = NA_KH + NA_RB
NA_GROUP = 8


def _na_geometry(rows):
    kh = min(NA_KH, rows)
    assert rows % (NA_RB * NA_GROUP) == 0 and rows >= NA_KROWS
    patterns, ids, kstarts = [], [], []
    for r in range(0, rows, NA_RB):
        ks = int(np.clip(r - NA_KH // 2, 0, rows - NA_KROWS))
        pat = tuple((int(np.clip(r + a - NA_KH // 2, 0, rows - kh)) - ks, ks - (r + a)) for a in range(NA_RB))
        if pat not in patterns:
            patterns.append(pat)
        ids.append(patterns.index(pat))
        kstarts.append(ks)
    return patterns, np.array(ids + kstarts, np.int32)


def _na_bias_table(rpb, rows, patterns):
    h = rpb.shape[0]
    kh = min(NA_KH, rows)
    qc = np.arange(GRID_W)
    kc = np.arange(GRID_W)
    cstart = np.clip(qc - NA_KW // 2, 0, GRID_W - NA_KW)
    ok = (kc[None, :] >= cstart[:, None]) & (kc[None, :] < cstart[:, None] + NA_KW)
    dc = np.clip(kc[None, :] - qc[:, None] + NA_KW - 1, 0, 2 * NA_KW - 2)
    by_col = rpb.astype(F32)[:, :, dc] * LOG2E + jnp.where(ok, 0.0, NEG_INF).astype(F32)
    masked_row = 2 * NA_KH - 1
    by_col = jnp.concatenate([by_col, jnp.full((h, 1, GRID_W, GRID_W), NEG_INF, F32)], axis=1)
    kr = np.arange(NA_KROWS)
    dr = np.zeros((len(patterns), NA_RB, NA_KROWS), np.int32)
    for p, pat in enumerate(patterns):
        for a, (start, offset) in enumerate(pat):
            row_ok = (kr >= start) & (kr < start + kh)
            dr[p, a] = np.where(row_ok, kr + offset + NA_KH - 1, masked_row)
    tab = jnp.take(by_col, jnp.asarray(dr.reshape(-1)), axis=1)
    tab = tab.reshape(h, len(patterns), NA_RB, NA_KROWS, GRID_W, GRID_W)
    tab = tab.transpose(0, 1, 2, 4, 3, 5)
    return tab.reshape(h, len(patterns), NA_RB * GRID_W, NA_KROWS * GRID_W)


def _na_kernel(geo_ref, q_ref, k_ref, v_ref, b_ref, o_ref, *, n_blocks):
    qn = NA_RB * GRID_W
    kn = NA_KROWS * GRID_W
    ones = jnp.ones((kn, HEAD_DIM), BF16)

    def scores(blk):
        q0 = pl.multiple_of(blk * qn, qn)
        k0 = pl.multiple_of(geo_ref[n_blocks + blk] * GRID_W, GRID_W)
        s = _dot_nt(q_ref[pl.ds(q0, qn), :], k_ref[pl.ds(k0, kn), :]) + b_ref[geo_ref[blk]]
        return s, q0, k0

    def finish(s, q0, k0):
        v1 = jnp.concatenate([v_ref[pl.ds(k0, kn), :], ones], axis=1)
        m = jnp.max(s, axis=-1, keepdims=True)
        acc = _dot(jnp.exp2(s - m).astype(BF16), v1)
        o_ref[pl.ds(q0, qn), :] = (acc[:, :HEAD_DIM] / acc[:, HEAD_DIM:]).astype(o_ref.dtype)

    def body(i, carry):
        pending = [scores(i * NA_GROUP + j) for j in range(NA_GROUP)]
        for item in pending:
            finish(*item)
        return carry

    lax.fori_loop(0, n_blocks // NA_GROUP, body, 0)


def _na_attention(qk, v, rpb, batch, seq):
    rows = seq // GRID_W
    patterns, geo = _na_geometry(rows)
    bias = _na_bias_table(rpb, rows, patterns)
    t = qk.shape[0]
    blk = (seq, HEAD_DIM)
    return pl.pallas_call(
        functools.partial(_na_kernel, n_blocks=rows // NA_RB),
        grid_spec=pltpu.PrefetchScalarGridSpec(
            num_scalar_prefetch=1,
            grid=(batch, NA_HEADS),
            in_specs=[pl.BlockSpec(blk, lambda b, h, geo_ref: (b, h)),
                      pl.BlockSpec(blk, lambda b, h, geo_ref: (b, NA_HEADS + h)),
                      pl.BlockSpec(blk, lambda b, h, geo_ref: (b, h)),
                      pl.BlockSpec((None,) + bias.shape[1:], lambda b, h, geo_ref: (h, 0, 0, 0))],
            out_specs=pl.BlockSpec(blk, lambda b, h, geo_ref: (b, h))),
        out_shape=jax.ShapeDtypeStruct((t, NA_W), BF16),
        compiler_params=_params("parallel", "parallel"),
        name="na_attention",
    )(jnp.asarray(geo), qk, qk, v, bias)


DIL_QCHUNK = 2 * DIL_RADIUS
DIL_KWIN = 4 * DIL_RADIUS
DIL_GROUP = 16


def _dil_window_mask():
    r = np.arange(DIL_QCHUNK)[None, :, None]
    c = np.arange(DIL_KWIN)[None, None, :]
    off = (np.arange(3) * DIL_RADIUS)[:, None, None]
    return jnp.asarray(np.where(np.abs(c - r - off) <= DIL_RADIUS, 0.0, NEG_INF), F32)


def _dil_kernel(q_ref, k_ref, v_ref, mask_ref, o_ref, lse_ref, *, seq, dil):
    length = seq // dil
    n_chunk = length // DIL_QCHUNK
    ones = jnp.ones((DIL_KWIN, HEAD_DIM), BF16)

    def rows(residue, first, count):
        return pl.ds(residue + first * dil, count, stride=dil)

    def scores(it):
        residue = it // n_chunk
        q0 = (it % n_chunk) * DIL_QCHUNK
        k0 = jnp.clip(q0 - DIL_RADIUS, 0, length - DIL_KWIN)
        q = q_ref[rows(residue, q0, DIL_QCHUNK), :].astype(BF16)
        k = k_ref[rows(residue, k0, DIL_KWIN), :].astype(BF16)
        return _dot_nt(q, k) + mask_ref[(q0 - k0) // DIL_RADIUS], residue, q0, k0

    def finish(s, residue, q0, k0):
        v1 = jnp.concatenate([v_ref[rows(residue, k0, DIL_KWIN), :].astype(BF16), ones], axis=1)
        m = jnp.max(s, axis=-1, keepdims=True)
        acc = _dot(jnp.exp2(s - m).astype(BF16), v1)
        den = acc[:, HEAD_DIM:]
        o_ref[rows(residue, q0, DIL_QCHUNK), :] = acc[:, :HEAD_DIM] / den
        lse_ref[rows(residue, q0, DIL_QCHUNK), :] = m * LN2 + jnp.log(den[:, :1])

    def body(i, carry):
        pending = [scores(i * DIL_GROUP + j) for j in range(DIL_GROUP)]
        for item in pending:
            finish(*item)
        return carry

    assert (dil * n_chunk) % DIL_GROUP == 0
    lax.fori_loop(0, dil * n_chunk // DIL_GROUP, body, 0)


def _dil_group(qk, v, group, dil, batch, seq):
    length = seq // dil
    assert length % DIL_QCHUNK == 0 and length >= DIL_KWIN
    hg = DIL_HEADS_PER_GROUP
    blk = (seq, HEAD_DIM)
    mask = _dil_window_mask()
    return pl.pallas_call(
        functools.partial(_dil_kernel, seq=seq, dil=dil),
        grid=(batch, hg),
        in_specs=[pl.BlockSpec(blk, lambda b, h: (b, group * hg + h)),
                  pl.BlockSpec(blk, lambda b, h: (b, DIL_HEADS + group * hg + h)),
                  pl.BlockSpec(blk, lambda b, h: (b, group * hg + h)),
                  pl.BlockSpec(mask.shape, lambda b, h: (0, 0, 0))],
        out_specs=[pl.BlockSpec(blk, lambda b, h: (b, h)),
                   pl.BlockSpec((None, seq, 1), lambda b, h: (b * hg + h, 0, 0))],
        out_shape=[jax.ShapeDtypeStruct((batch * seq, DIL_OUT), F32),
                   jax.ShapeDtypeStruct((batch * hg, seq, 1), F32)],
        compiler_params=_params("parallel", "parallel"),
        name="dilated_attention",
    )(qk, qk, v, mask)


def _dil_mix_kernel(o0_ref, o1_ref, o2_ref, l0_ref, l1_ref, l2_ref, o_ref):
    l0, l1, l2 = l0_ref[...], l1_ref[...], l2_ref[...]
    mx = jnp.maximum(jnp.maximum(l0, l1), l2)
    e0, e1, e2 = jnp.exp(l0 - mx), jnp.exp(l1 - mx), jnp.exp(l2 - mx)
    z = e0 + e1 + e2
    w0, w1, w2 = e0 / z, e1 / z, e2 / z
    for h in range(DIL_HEADS_PER_GROUP):
        sl = slice(h * HEAD_DIM, (h + 1) * HEAD_DIM)
        hs = slice(h, h + 1)
        o_ref[:, sl] = (w0[:, hs] * o0_ref[:, sl] + w1[:, hs] * o1_ref[:, sl]
                        + w2[:, hs] * o2_ref[:, sl]).astype(o_ref.dtype)


def _dil_mix(outs, lses, tm=512):
    t = outs[0].shape[0]
    ospec = pl.BlockSpec((tm, DIL_OUT), lambda i: (i, 0))
    lspec = pl.BlockSpec((tm, DIL_HEADS_PER_GROUP), lambda i: (i, 0))
    return pl.pallas_call(
        _dil_mix_kernel,
        grid=(t // tm,),
        in_specs=[ospec] * 3 + [lspec] * 3,
        out_specs=ospec,
        out_shape=jax.ShapeDtypeStruct((t, DIL_OUT), BF16),
        compiler_params=_params("parallel"),
        name="dilated_mix",
    )(*outs, *lses)


def _dilated_attention(qk, v, batch, seq):
    hg = DIL_HEADS_PER_GROUP
    outs, lses = [], []
    for g, (window, dil) in enumerate(DIL_GROUPS):
        assert window // (2 * dil) == DIL_RADIUS and seq % dil == 0
        o, lse = _dil_group(qk, v, g, dil, batch, seq)
        outs.append(o)
        lses.append(lse.reshape(batch, hg, seq).transpose(0, 2, 1).reshape(batch * seq, hg))
    return _dil_mix(outs, lses)


def _mla_q_kernel(a_ref, w_ref, g_ref, cos_ref, sin_ref, o_ref):
    a = a_ref[...]
    gain = g_ref[...]
    cos_t, sin_t = cos_ref[...], sin_ref[...]
    acc = _dot(a, w_ref[0])
    for h in range(MLA_HEADS):
        nxt = _dot(a, w_ref[h + 1]) if h + 1 < MLA_HEADS else None
        ms = jnp.sum(acc * acc, axis=-1, keepdims=True) * (1.0 / MLA_QK)
        y = acc * lax.rsqrt(ms + RMS_EPS) * gain
        o_ref[h, :, :MLA_NOPE] = y[:, :MLA_NOPE].astype(o_ref.dtype)
        o_ref[h, :, MLA_NOPE:] = _rotate_pairs(y[:, MLA_NOPE:], cos_t, sin_t,
                                               MLA_ROPE // 2).astype(o_ref.dtype)
        acc = nxt


def _mla_q_proj(cq_n, w_uq_h, gain, cos_t, sin_t, seq, tm=512):
    t, rank = cq_n.shape
    per = seq // tm
    return pl.pallas_call(
        _mla_q_kernel,
        grid=(t // tm,),
        in_specs=[pl.BlockSpec((tm, rank), lambda i: (i, 0)),
                  pl.BlockSpec((MLA_HEADS, rank, MLA_QK_PAD), lambda i: (0, 0, 0)),
                  pl.BlockSpec((1, MLA_QK_PAD), lambda i: (0, 0)),
                  pl.BlockSpec((tm, LANES), lambda i: (i % per, 0)),
                  pl.BlockSpec((tm, LANES), lambda i: (i % per, 0))],
        out_specs=pl.BlockSpec((MLA_HEADS, tm, MLA_QK_PAD), lambda i: (0, i, 0)),
        out_shape=jax.ShapeDtypeStruct((MLA_HEADS, t, MLA_QK_PAD), BF16),
        compiler_params=_params("parallel"),
        name="mla_q_proj",
    )(cq_n, w_uq_h, gain, cos_t, sin_t)


def _mla_kv_kernel(a_ref, w_ref, kr_ref, g0_ref, g1_ref, cos_ref, sin_ref, k_ref, v_ref):
    a = a_ref[...]
    kr = kr_ref[...]
    kr_ss = jnp.sum(kr * kr, axis=-1, keepdims=True)
    g0, g1 = g0_ref[...], g1_ref[...]
    cos_t, sin_t = cos_ref[...], sin_ref[...]
    ones = jnp.ones((a.shape[0], MLA_V), v_ref.dtype)
    kr_rot = _rotate_pairs(kr * g1, cos_t, sin_t, MLA_ROPE // 2)
    acc = _dot(a, w_ref[0])
    for h in range(MLA_HEADS):
        nxt = _dot(a, w_ref[h + 1]) if h + 1 < MLA_HEADS else None
        kn = acc[:, :MLA_NOPE]
        ms = (jnp.sum(kn * kn, axis=-1, keepdims=True) + kr_ss) * (1.0 / MLA_QK)
        inv = lax.rsqrt(ms + RMS_EPS)
        k_ref[h, :, :MLA_NOPE] = (kn * inv * g0).astype(k_ref.dtype)
        k_ref[h, :, MLA_NOPE:] = (kr_rot * inv).astype(k_ref.dtype)
        v_ref[h, :, :MLA_V] = acc[:, MLA_NOPE:].astype(v_ref.dtype)
        v_ref[h, :, MLA_V:] = ones
        acc = nxt


def _mla_kv_proj(ckv_n, w_ukv_h, k_r, g0, g1, cos_t, sin_t, seq, tm=1024):
    t, rank = ckv_n.shape
    per = seq // tm
    return pl.pallas_call(
        _mla_kv_kernel,
        grid=(t // tm,),
        in_specs=[pl.BlockSpec((tm, rank), lambda i: (i, 0)),
                  pl.BlockSpec((MLA_HEADS, rank, MLA_NOPE + MLA_V), lambda i: (0, 0, 0)),
                  pl.BlockSpec((tm, LANES), lambda i: (i, 0)),
                  pl.BlockSpec((1, LANES), lambda i: (0, 0)),
                  pl.BlockSpec((1, LANES), lambda i: (0, 0)),
                  pl.BlockSpec((tm, LANES), lambda i: (i % per, 0)),
                  pl.BlockSpec((tm, LANES), lambda i: (i % per, 0))],
        out_specs=[pl.BlockSpec((MLA_HEADS, tm, MLA_QK_PAD), lambda i: (0, i, 0)),
                   pl.BlockSpec((MLA_HEADS, tm, 2 * MLA_V), lambda i: (0, i, 0))],
        out_shape=[jax.ShapeDtypeStruct((MLA_HEADS, t, MLA_QK_PAD), BF16),
                   jax.ShapeDtypeStruct((MLA_HEADS, t, 2 * MLA_V), BF16)],
        compiler_params=_params("parallel"),
        name="mla_kv_proj",
    )(ckv_n, w_ukv_h, k_r, g0, g1, cos_t, sin_t)


MLA_KV_CHUNKS = 8
MLA_QK_AHEAD = 1


def _mla_attn_kernel(q_ref, k_ref, v_ref, o_ref):
    q = q_ref[...]
    tk = k_ref.shape[0] // MLA_KV_CHUNKS
    m = acc = None
    ahead = [_dot_nt(q, k_ref[i * tk:(i + 1) * tk, :]) for i in range(MLA_QK_AHEAD)]
    for c in range(MLA_KV_CHUNKS):
        s = ahead.pop(0)
        nxt = c + MLA_QK_AHEAD
        if nxt < MLA_KV_CHUNKS:
            ahead.append(_dot_nt(q, k_ref[nxt * tk:(nxt + 1) * tk, :]))
        m_c = jnp.max(s, axis=-1, keepdims=True)
        if c == 0:
            m = m_c
            acc = _dot(jnp.exp2(s - m).astype(BF16), v_ref[:tk, :])
        else:
            m_new = jnp.maximum(m, m_c)
            acc = jnp.exp2(m - m_new) * acc + _dot(jnp.exp2(s - m_new).astype(BF16),
                                                   v_ref[c * tk:(c + 1) * tk, :])
            m = m_new
    o_ref[...] = (acc[:, :MLA_V] / acc[:, MLA_V:]).astype(o_ref.dtype)


def _mla_attention(q, k, v, batch, seq, tq=1024):
    t = q.shape[1]
    nq = seq // tq
    return pl.pallas_call(
        _mla_attn_kernel,
        grid=(batch, MLA_HEADS, nq),
        in_specs=[pl.BlockSpec((None, tq, MLA_QK_PAD), lambda b, h, i: (h, b * nq + i, 0)),
                  pl.BlockSpec((None, seq, MLA_QK_PAD), lambda b, h, i: (h, b, 0)),
                  pl.BlockSpec((None, seq, 2 * MLA_V), lambda b, h, i: (h, b, 0))],
        out_specs=pl.BlockSpec((tq, MLA_V), lambda b, h, i: (b * nq + i, h)),
        out_shape=jax.ShapeDtypeStruct((t, MLA_HEADS * MLA_V), BF16),
        compiler_params=_params("parallel", "parallel", "arbitrary"),
        name="mla_attention",
    )(q, k, v)


def _merge_kernel(ona_ref, odl_ref, omla_ref, wna_ref, wdl_ref, wmla_ref, g0_ref, g1_ref, g2_ref, o_ref):
    ona, odl, omla = ona_ref[...], odl_ref[...], omla_ref[...]
    for c in range(o_ref.shape[1] // MXU_COLS):
        cols = slice(c * MXU_COLS, (c + 1) * MXU_COLS)
        acc = g0_ref[:, cols].astype(F32) * _dot(ona, wna_ref[:, cols])
        acc = acc + g1_ref[:, cols].astype(F32) * _dot(odl, wdl_ref[:, cols])
        acc = acc + g2_ref[:, cols].astype(F32) * _dot(omla, wmla_ref[:, cols])
        o_ref[:, cols] = acc.astype(o_ref.dtype)


def _merge(o_na, o_dl, o_mla, w_na, w_dl, w_mla, gates, tm=1024, tn=512):
    t = o_na.shape[0]
    d = w_na.shape[1]
    nj = d // tn

    def act(a):
        return pl.BlockSpec((tm, a.shape[1]), lambda i, j: (i, 0))

    def wgt(w):
        return pl.BlockSpec((w.shape[0], tn), lambda i, j: (0, j))

    def gate(idx):
        return pl.BlockSpec((tm, tn), lambda i, j: (i, idx * nj + j))

    return pl.pallas_call(
        _merge_kernel,
        grid=(t // tm, nj),
        in_specs=[act(o_na), act(o_dl), act(o_mla), wgt(w_na), wgt(w_dl), wgt(w_mla),
                  gate(0), gate(1), gate(2)],
        out_specs=pl.BlockSpec((tm, tn), lambda i, j: (i, j)),
        out_shape=jax.ShapeDtypeStruct((t, d), BF16),
        compiler_params=_params("parallel", "parallel"),
        name="branch_merge",
    )(o_na, o_dl, o_mla, w_na, w_dl, w_mla, gates, gates, gates)


def _swiglu_pieces(h, wg_ref, wu_ref, o_ref, rows=slice(None)):
    for c in range(o_ref.shape[1] // MXU_COLS):
        cols = slice(c * MXU_COLS, (c + 1) * MXU_COLS)
        g = _dot(h, wg_ref[:, cols].astype(BF16))
        u = _dot(h, wu_ref[:, cols].astype(BF16))
        o_ref[rows, cols] = (g * jax.nn.sigmoid(g) * u).astype(o_ref.dtype)


def _gu_kernel(h_ref, wg_ref, wu_ref, o_ref):
    _swiglu_pieces(h_ref[...], wg_ref, wu_ref, o_ref)


def _swiglu_up(h, w_gu, tm=1024, tn=512):
    t, d = h.shape
    ff = w_gu.shape[1] // 2
    nj = ff // tn
    return pl.pallas_call(
        _gu_kernel,
        grid=(t // tm, nj),
        in_specs=[pl.BlockSpec((tm, d), lambda i, j: (i, 0)),
                  pl.BlockSpec((d, tn), lambda i, j: (0, j)),
                  pl.BlockSpec((d, tn), lambda i, j: (0, nj + j))],
        out_specs=pl.BlockSpec((tm, tn), lambda i, j: (i, j)),
        out_shape=jax.ShapeDtypeStruct((t, ff), BF16),
        compiler_params=_params("parallel", "parallel"),
        name="swiglu_up",
    )(h, w_gu, w_gu)


TOP_K = 2
MOE_TILE_ROWS = 1024


def _moe_routing(route, tokens, tm):
    e_flat = jnp.concatenate([route[:, N_EXPERTS], route[:, N_EXPERTS + 1]]).astype(jnp.int32)
    onehot = (e_flat[:, None] == jnp.arange(N_EXPERTS, dtype=jnp.int32)[None, :]).astype(jnp.int32)
    csum = jnp.cumsum(onehot, axis=0)
    rank = jnp.sum((csum - onehot) * onehot, axis=1)
    padded = ((csum[-1] + tm - 1) // tm) * tm
    ends = jnp.cumsum(padded)
    slot = jnp.sum(onehot * (ends - padded)[None, :], axis=1) + rank
    n_tiles = (TOP_K * tokens) // tm + N_EXPERTS
    tile_start = jnp.arange(n_tiles, dtype=jnp.int32) * tm
    tile_expert = jnp.minimum(jnp.sum(tile_start[:, None] >= ends[None, :], axis=1), N_EXPERTS - 1)
    group_end = jnp.sum(jnp.where(jnp.arange(N_EXPERTS)[None, :] == tile_expert[:, None],
                                  (ends - padded + csum[-1])[None, :], 0), axis=1)
    tile_valid = jnp.clip(group_end - tile_start, 0, tm)
    te = jnp.concatenate([tile_expert, ends[-1:] // tm, tile_valid]).astype(jnp.int32)
    return slot.astype(jnp.int32), te


def _dispatch_kernel(slot_ref, x_ref, g_ref, sc_ref, sh_ref, dst_in_ref, dst_ref, h_ref, sem, *, rows, tokens):
    del dst_in_ref
    base = pl.program_id(0) * rows
    h_ref[...] = _modulated_norm(x_ref[...], g_ref[...], sc_ref[...], sh_ref[...])

    def row_copy(r, choice):
        slot = slot_ref[choice * tokens + base + r]
        return pltpu.make_async_copy(h_ref.at[pl.ds(r, 1), :], dst_ref.at[pl.ds(slot, 1), :], sem)

    def start(r, carry):
        row_copy(r, 0).start()
        row_copy(r, 1).start()
        return carry

    def wait(r, carry):
        row_copy(r, 0).wait()
        row_copy(r, 1).wait()
        return carry

    lax.fori_loop(0, rows, start, 0, unroll=8)
    lax.fori_loop(0, rows, wait, 0, unroll=8)


def _moe_dispatch(x, g, sc, sh, slot, seq, n_rows, rows=256):
    t, d = x.shape
    per = seq // rows
    return pl.pallas_call(
        functools.partial(_dispatch_kernel, rows=rows, tokens=t),
        grid_spec=pltpu.PrefetchScalarGridSpec(
            num_scalar_prefetch=1,
            grid=(t // rows,),
            in_specs=[pl.BlockSpec((rows, d), lambda i, s: (i, 0)),
                      pl.BlockSpec((1, d), lambda i, s: (0, 0)),
                      pl.BlockSpec((None, 1, d), lambda i, s: (i // per, 0, 0)),
                      pl.BlockSpec((None, 1, d), lambda i, s: (i // per, 0, 0)),
                      pl.BlockSpec(memory_space=pl.ANY)],
            out_specs=pl.BlockSpec(memory_space=pl.ANY),
            scratch_shapes=[pltpu.VMEM((rows, d), F32), pltpu.SemaphoreType.DMA(())]),
        out_shape=jax.ShapeDtypeStruct((n_rows, d), F32),
        input_output_aliases={5: 0},
        compiler_params=_params("arbitrary"),
        name="moe_dispatch",
    )(slot, x, g.reshape(1, d), sc, sh, jnp.zeros((n_rows, d), F32))


def _moe_up_kernel(te_ref, xs_ref, wg_ref, wu_ref, o_ref, hb_ref, *, n_tiles):
    i = pl.program_id(0)
    used = i < te_ref[n_tiles]
    half = o_ref.shape[0] // 2
    full = te_ref[n_tiles + 1 + i] > half

    @pl.when(used & (pl.program_id(1) == 0))
    def _():
        hb_ref[...] = xs_ref[...].astype(BF16)

    @pl.when(used & full)
    def _():
        _swiglu_pieces(hb_ref[...], wg_ref, wu_ref, o_ref)

    @pl.when(used & jnp.logical_not(full))
    def _():
        _swiglu_pieces(hb_ref[:half, :], wg_ref, wu_ref, o_ref, rows=slice(0, half))
        o_ref[half:, :] = jnp.zeros((o_ref.shape[0] - half, o_ref.shape[1]), o_ref.dtype)

    @pl.when(jnp.logical_not(used))
    def _():
        o_ref[...] = jnp.zeros(o_ref.shape, o_ref.dtype)


def _moe_up(xs, w_gu, te, tn=512):
    n_rows, d = xs.shape
    ff = w_gu.shape[2] // 2
    nj = ff // tn
    n_tiles = (te.shape[0] - 1) // 2
    tm = n_rows // n_tiles

    def last_used(i, te_ref):
        return jnp.minimum(i, te_ref[n_tiles] - 1)

    return pl.pallas_call(
        functools.partial(_moe_up_kernel, n_tiles=n_tiles),
        grid_spec=pltpu.PrefetchScalarGridSpec(
            num_scalar_prefetch=1,
            grid=(n_tiles, nj),
            in_specs=[pl.BlockSpec((tm, d), lambda i, j, te_ref: (last_used(i, te_ref), 0)),
                      pl.BlockSpec((None, d, tn), lambda i, j, te_ref: (te_ref[i], 0, j)),
                      pl.BlockSpec((None, d, tn), lambda i, j, te_ref: (te_ref[i], 0, nj + j))],
            out_specs=pl.BlockSpec((tm, tn), lambda i, j, te_ref: (i, j)),
            scratch_shapes=[pltpu.VMEM((tm, d), BF16)]),
        out_shape=jax.ShapeDtypeStruct((n_rows, ff), BF16),
        compiler_params=_params("parallel", "arbitrary"),
        name="moe_up",
    )(te, xs, w_gu, w_gu)


def _moe_down_kernel(te_ref, a_ref, w_ref, o_ref, *, n_tiles):
    i = pl.program_id(0)
    used = i < te_ref[n_tiles]
    half = o_ref.shape[0] // 2
    full = te_ref[n_tiles + 1 + i] > half

    @pl.when(used & full)
    def _():
        o_ref[...] = _dot(a_ref[...], w_ref[...])

    @pl.when(used & jnp.logical_not(full))
    def _():
        o_ref[:half, :] = _dot(a_ref[:half, :], w_ref[...])
        o_ref[half:, :] = jnp.zeros((o_ref.shape[0] - half, o_ref.shape[1]), o_ref.dtype)

    @pl.when(jnp.logical_not(used))
    def _():
        o_ref[...] = jnp.zeros(o_ref.shape, o_ref.dtype)


def _moe_down(act, w_down, te, tn=256):
    n_rows, ff = act.shape
    d = w_down.shape[2]
    n_tiles = (te.shape[0] - 1) // 2
    tm = n_rows // n_tiles

    def last_used(i, te_ref):
        return jnp.minimum(i, te_ref[n_tiles] - 1)

    return pl.pallas_call(
        functools.partial(_moe_down_kernel, n_tiles=n_tiles),
        grid_spec=pltpu.PrefetchScalarGridSpec(
            num_scalar_prefetch=1,
            grid=(n_tiles, d // tn),
            in_specs=[pl.BlockSpec((tm, ff), lambda i, j, te_ref: (last_used(i, te_ref), 0)),
                      pl.BlockSpec((None, ff, tn), lambda i, j, te_ref: (te_ref[i], 0, j))],
            out_specs=pl.BlockSpec((tm, tn), lambda i, j, te_ref: (i, j))),
        out_shape=jax.ShapeDtypeStruct((n_rows, d), F32),
        compiler_params=_params("parallel", "arbitrary"),
        name="moe_down",
    )(te, act, w_down)


def _combine_kernel(slot_ref, x_ref, gate_ref, w_ref, ys_ref, o_ref, buf_ref, sem, *, rows, tokens):
    base = pl.program_id(0) * rows

    def row_copy(r, choice):
        slot = slot_ref[choice * tokens + base + r]
        return pltpu.make_async_copy(ys_ref.at[pl.ds(slot, 1), :], buf_ref.at[choice, pl.ds(r, 1), :], sem)

    def start(r, carry):
        row_copy(r, 0).start()
        row_copy(r, 1).start()
        return carry

    def wait(r, carry):
        row_copy(r, 0).wait()
        row_copy(r, 1).wait()
        return carry

    lax.fori_loop(0, rows, start, 0, unroll=8)
    lax.fori_loop(0, rows, wait, 0, unroll=8)
    w = w_ref[...]
    y = w[:, 0:1] * buf_ref[0] + w[:, 1:2] * buf_ref[1]
    o_ref[...] = x_ref[...] + gate_ref[...] * y


def _moe_combine(x, gate, w12, ys, slot, seq, rows=256):
    t, d = x.shape
    per = seq // rows
    return pl.pallas_call(
        functools.partial(_combine_kernel, rows=rows, tokens=t),
        grid_spec=pltpu.PrefetchScalarGridSpec(
            num_scalar_prefetch=1,
            grid=(t // rows,),
            in_specs=[pl.BlockSpec((rows, d), lambda i, s: (i, 0)),
                      pl.BlockSpec((None, 1, d), lambda i, s: (i // per, 0, 0)),
                      pl.BlockSpec((rows, TOP_K), lambda i, s: (i, 0)),
                      pl.BlockSpec(memory_space=pl.ANY)],
            out_specs=pl.BlockSpec((rows, d), lambda i, s: (i, 0)),
            scratch_shapes=[pltpu.VMEM((TOP_K, rows, d), F32), pltpu.SemaphoreType.DMA(())]),
        out_shape=jax.ShapeDtypeStruct((t, d), F32),
        compiler_params=_params("arbitrary"),
        name="moe_combine",
    )(slot, x, gate, w12, ys)


def _router_kernel(x_ref, g_ref, sc_ref, sh_ref, w_ref, o_ref):
    h = _modulated_norm(x_ref[...], g_ref[...], sc_ref[...], sh_ref[...])
    logits = jnp.dot(h, w_ref[...], preferred_element_type=F32, precision=lax.Precision.HIGHEST)
    lane = lax.broadcasted_iota(jnp.int32, logits.shape, 1).astype(F32)
    lg = jnp.where(lane < N_EXPERTS, logits, NEG_INF)
    m1 = jnp.max(lg, axis=-1, keepdims=True)
    i1 = jnp.min(jnp.where(lg == m1, lane, float(LANES)), axis=-1, keepdims=True)
    lg2 = jnp.where(lane == i1, NEG_INF, lg)
    m2 = jnp.max(lg2, axis=-1, keepdims=True)
    i2 = jnp.min(jnp.where(lg2 == m2, lane, float(LANES)), axis=-1, keepdims=True)
    e2 = jnp.exp(m2 - m1)
    z = 1.0 + e2
    out = jnp.where(lane == N_EXPERTS, i1, 0.0) + jnp.where(lane == N_EXPERTS + 1, i2, 0.0)
    out = out + jnp.where(lane == N_EXPERTS + 2, 1.0 / z, 0.0) + jnp.where(lane == N_EXPERTS + 3, e2 / z, 0.0)
    o_ref[...] = out


def _router(x, g, sc, sh, w_router_pad, seq, tm=512):
    t, d = x.shape
    per = seq // tm
    return pl.pallas_call(
        _router_kernel,
        grid=(t // tm,),
        in_specs=[pl.BlockSpec((tm, d), lambda i: (i, 0)),
                  pl.BlockSpec((1, d), lambda i: (0, 0)),
                  pl.BlockSpec((None, 1, d), lambda i: (i // per, 0, 0)),
                  pl.BlockSpec((None, 1, d), lambda i: (i // per, 0, 0)),
                  pl.BlockSpec((d, LANES), lambda i: (0, 0))],
        out_specs=pl.BlockSpec((tm, LANES), lambda i: (i, 0)),
        out_shape=jax.ShapeDtypeStruct((t, LANES), F32),
        compiler_params=_params("parallel"),
        name="router",
    )(x, g.reshape(1, d), sc, sh, w_router_pad)


MM_TM = 1024
MM_TN = 1024


def _col_tile(n):
    return MM_TN if n % MM_TN == 0 else MM_TN // 2


def _residual_matmul(a, w, x, gate, seq, name, tm=MM_TM, tn=None):
    per = seq // tm
    tn = _col_tile(w.shape[1]) if tn is None else tn
    return _matmul(a, w, tm=tm, tn=tn, out_dtype=F32, epilogue=_ep_residual, split=MXU_COLS,
                   extras=[(x, (tm, tn), lambda i, j: (i, j)),
                           (gate, (None, 1, tn), lambda i, j: (i // per, 0, j))],
                   name=name)


def _token_mixer(h, x, gate, lw, batch, seq, tables):
    d = h.shape[1]
    w_in = lw["w_in"]
    tm = MM_TM
    per = seq // tm
    cos_d, sin_d, cos_m, sin_m = tables

    def cols(lo, hi):
        return w_in[:, lo:hi].astype(BF16)

    def tile_gain(gq, gk, heads):
        gq = gq * (LOG2E / math.sqrt(HEAD_DIM))
        return jnp.concatenate([jnp.tile(gq, heads), jnp.tile(gk, heads)]).reshape(1, -1).astype(F32)

    o = 0
    tn = _col_tile(2 * NA_W)
    na_qk = _matmul(h, cols(o, o + 2 * NA_W), tm=tm, tn=tn, out_dtype=BF16, epilogue=_ep_headnorm,
                    split=MXU_COLS,
                    extras=[(tile_gain(lw["na_q_norm"], lw["na_k_norm"], NA_HEADS), (1, tn),
                             lambda i, j: (0, j))], name="na_qk_proj")
    o += 2 * NA_W
    na_v = _matmul(h, cols(o, o + NA_W), tm=tm, tn=_col_tile(NA_W), out_dtype=BF16, split=MXU_COLS, name="na_v_proj")
    o += NA_W
    tn = _col_tile(2 * DIL_W)
    dl_qk = _matmul(h, cols(o, o + 2 * DIL_W), tm=tm, tn=tn, out_dtype=F32, epilogue=_ep_headnorm_rope,
                    split=MXU_COLS,
                    extras=[(tile_gain(lw["dil_q_norm"], lw["dil_k_norm"], DIL_HEADS), (1, tn),
                             lambda i, j: (0, j)),
                            (cos_d, (tm, LANES), lambda i, j: (i % per, 0)),
                            (sin_d, (tm, LANES), lambda i, j: (i % per, 0)),
                            (_pair_swap_matrix(MXU_COLS, ROT_DIM // 2), (MXU_COLS, MXU_COLS),
                             lambda i, j: (0, 0))], name="dil_qk_proj")
    o += 2 * DIL_W
    dl_v = _matmul(h, cols(o, o + DIL_W), tm=tm, tn=_col_tile(DIL_W), out_dtype=F32, split=MXU_COLS, name="dil_v_proj")
    o += DIL_W
    cq_n = _matmul(h, cols(o, o + MLA_Q_RANK), tm=tm, tn=MLA_Q_RANK, out_dtype=BF16, epilogue=_ep_rownorm,
                   extras=[(lw["mla_q_a_norm"].reshape(1, -1), (1, MLA_Q_RANK), lambda i, j: (0, 0))],
                   name="mla_cq_proj")
    o += MLA_Q_RANK
    ckv_n = _matmul(h, cols(o, o + MLA_KV_RANK), tm=tm, tn=MLA_KV_RANK, out_dtype=BF16, epilogue=_ep_rownorm,
                    extras=[(lw["mla_kv_a_norm"].reshape(1, -1), (1, MLA_KV_RANK), lambda i, j: (0, 0))],
                    name="mla_ckv_proj")
    o += MLA_KV_RANK
    w_kr = jnp.pad(w_in[:, o:o + MLA_ROPE], ((0, 0), (0, LANES - MLA_ROPE))).astype(BF16)
    k_r = _matmul(h, w_kr, tm=tm, tn=LANES, out_dtype=F32, name="mla_kr_proj")
    o += MLA_ROPE
    gates = _matmul(h, cols(o, o + 3 * d), tm=tm, tn=_col_tile(3 * d), out_dtype=BF16, epilogue=_ep_sigmoid,
                    split=MXU_COLS, name="gate_proj")

    o_na = _na_attention(na_qk, na_v, lw["na_rpb"], batch, seq)
    o_dl = _dilated_attention(dl_qk, dl_v, batch, seq)
    w_uq = lw["mla_w_uq"].reshape(MLA_Q_RANK, MLA_HEADS, MLA_QK).transpose(1, 0, 2)
    w_uq = jnp.pad(w_uq, ((0, 0), (0, 0), (0, MLA_QK_PAD - MLA_QK))).astype(BF16)
    w_ukv = lw["mla_w_ukv"].reshape(MLA_KV_RANK, MLA_HEADS, MLA_NOPE + MLA_V).transpose(1, 0, 2).astype(BF16)
    gq = jnp.pad(lw["mla_q_norm"] * (LOG2E / math.sqrt(MLA_QK)), (0, MLA_QK_PAD - MLA_QK)).reshape(1, MLA_QK_PAD)
    gk = lw["mla_k_norm"]
    gk0 = gk[:MLA_NOPE].reshape(1, LANES)
    gk1 = jnp.pad(gk[MLA_NOPE:], (0, LANES - MLA_ROPE)).reshape(1, LANES)
    q_m = _mla_q_proj(cq_n, w_uq, gq, cos_m, sin_m, seq)
    k_m, v_m = _mla_kv_proj(ckv_n, w_ukv, k_r, gk0, gk1, cos_m, sin_m, seq)
    o_mla = _mla_attention(q_m, k_m, v_m, batch, seq)

    merged = _merge(o_na, o_dl, o_mla, lw["w_branch_na"].astype(BF16), lw["w_branch_dil"].astype(BF16),
                    lw["w_branch_mla"].astype(BF16), gates)
    return _residual_matmul(merged, lw["w_out"].astype(BF16), x, gate, seq, "out_proj")


def kernel(x, c, w_ada, b_ada, norm_mix, norm_ffn, w_in, na_q_norm, na_k_norm, na_rpb, dil_q_norm, dil_k_norm, mla_q_a_norm, mla_w_uq, mla_kv_a_norm, mla_w_ukv, mla_q_norm, mla_k_norm, w_branch_na, w_branch_dil, w_branch_mla, w_out, ffn_w_gu, ffn_w_down, moe_router, moe_w_gu, moe_w_down):
    batch, seq, d = x.shape
    depth = w_ada.shape[0]
    assert seq % MM_TM == 0 and seq % GRID_W == 0
    t = batch * seq
    xf = x.reshape(t, d)

    c_pad = jnp.pad(c, ((0, (-batch) % 8), (0, 0)))
    mod = _adaln(c_pad, w_ada, b_ada)[:, :batch].reshape(depth, batch, 6, 1, d)
    tables = _rope_tables(seq, ROT_DIM, 1.0) + _rope_tables(seq, MLA_ROPE, 1.0)

    for l in range(depth):
        sh1, sc1, g1, sh2, sc2, g2 = [mod[l, :, i] for i in range(6)]
        lw = dict(w_in=w_in[l], na_q_norm=na_q_norm[l], na_k_norm=na_k_norm[l], na_rpb=na_rpb[l],
                  dil_q_norm=dil_q_norm[l], dil_k_norm=dil_k_norm[l], mla_q_a_norm=mla_q_a_norm[l],
                  mla_w_uq=mla_w_uq[l], mla_kv_a_norm=mla_kv_a_norm[l], mla_w_ukv=mla_w_ukv[l],
                  mla_q_norm=mla_q_norm[l], mla_k_norm=mla_k_norm[l], w_branch_na=w_branch_na[l],
                  w_branch_dil=w_branch_dil[l], w_branch_mla=w_branch_mla[l], w_out=w_out[l])
        h = _normmod(xf, norm_mix[l], sc1, sh1, seq)
        xf = _token_mixer(h, xf, g1, lw, batch, seq, tables)

        if l % 2 == 0:
            h = _normmod(xf, norm_ffn[l], sc2, sh2, seq)
            act = _swiglu_up(h, ffn_w_gu[l // 2].astype(BF16))
            xf = _residual_matmul(act, ffn_w_down[l // 2].astype(BF16), xf, g2, seq, "down_proj", tm=1024, tn=256)
        else:
            w_r = jnp.pad(moe_router[l // 2], ((0, 0), (0, LANES - N_EXPERTS)))
            route = _router(xf, norm_ffn[l], sc2, sh2, w_r, seq)
            slot, te = _moe_routing(route, t, MOE_TILE_ROWS)
            n_rows = ((te.shape[0] - 1) // 2) * MOE_TILE_ROWS
            xs = _moe_dispatch(xf, norm_ffn[l], sc2, sh2, slot, seq, n_rows)
            act = _moe_up(xs, moe_w_gu[l // 2], te)
            ys = _moe_down(act, moe_w_down[l // 2].astype(BF16), te)
            xf = _moe_combine(xf, g2, route[:, N_EXPERTS + 2:N_EXPERTS + 4], ys, slot, seq)
    return xf.reshape(batch, seq, d)
```

```python
import functools
import math

import numpy as np
import jax
import jax.numpy as jnp
from jax import lax
from jax.experimental import pallas as pl
from jax.experimental.pallas import tpu as pltpu

GRID_W = 64
HEAD_DIM = 128
ROPE_THETA = 500000.0
ROT_DIM = HEAD_DIM // 4
RMS_EPS = 1e-6
NEG_INF = -1e30
NA_HEADS = 8
NA_KH = 8
NA_KW = 16
DIL_GROUPS = ((128, 1), (512, 4), (2048, 16))
DIL_HEADS_PER_GROUP = 4
DIL_HEADS = DIL_HEADS_PER_GROUP * len(DIL_GROUPS)
DIL_RADIUS = 64
MLA_HEADS = 8
MLA_NOPE = 128
MLA_ROPE = 64
MLA_V = 128
MLA_Q_RANK = 768
MLA_KV_RANK = 512
MLA_QK = MLA_NOPE + MLA_ROPE
MLA_QK_PAD = 256
N_EXPERTS = 8
LANES = 128
MXU_COLS = 256

NA_W = NA_HEADS * HEAD_DIM
DIL_W = DIL_HEADS * HEAD_DIM
DIL_OUT = DIL_HEADS_PER_GROUP * HEAD_DIM

VMEM_LIMIT_BYTES = 48 * 1024 * 1024

F32 = jnp.float32
BF16 = jnp.bfloat16
LOG2E = math.log2(math.e)
LN2 = math.log(2.0)


def _params(*sem):
    return pltpu.CompilerParams(dimension_semantics=sem, vmem_limit_bytes=VMEM_LIMIT_BYTES)


def _dot(a, b):
    return jnp.dot(a, b, preferred_element_type=F32)


def _dot_nt(a, b):
    return lax.dot_general(a, b, (((1,), (1,)), ((), ())), preferred_element_type=F32)


def _adaln_kernel(c_ref, w_ref, b_ref, o_ref):
    o_ref[...] = _dot(c_ref[...], w_ref[...]) + b_ref[...]


def _adaln(c_pad, w_ada, b_ada):
    depth, d, n = w_ada.shape
    tn = 1024
    return pl.pallas_call(
        _adaln_kernel,
        grid=(depth, n // tn),
        in_specs=[pl.BlockSpec(c_pad.shape, lambda l, j: (0, 0)),
                  pl.BlockSpec((None, d, tn), lambda l, j: (l, 0, j)),
                  pl.BlockSpec((None, 1, tn), lambda l, j: (l, 0, j))],
        out_specs=pl.BlockSpec((None, c_pad.shape[0], tn), lambda l, j: (l, 0, j)),
        out_shape=jax.ShapeDtypeStruct((depth, c_pad.shape[0], n), F32),
        compiler_params=_params("parallel", "parallel"),
        name="adaln",
    )(c_pad, w_ada, b_ada.reshape(depth, 1, n))


def _modulated_norm(x, g, sc, sh):
    ms = jnp.mean(x * x, axis=-1, keepdims=True)
    return (x * lax.rsqrt(ms + RMS_EPS) * g) * (1.0 + sc) + sh


def _normmod_kernel(x_ref, g_ref, sc_ref, sh_ref, o_ref):
    o_ref[...] = _modulated_norm(x_ref[...], g_ref[...], sc_ref[...], sh_ref[...]).astype(o_ref.dtype)


def _normmod(x, g, sc, sh, seq, tm=512):
    t, d = x.shape
    per = seq // tm
    return pl.pallas_call(
        _normmod_kernel,
        grid=(t // tm,),
        in_specs=[pl.BlockSpec((tm, d), lambda i: (i, 0)),
                  pl.BlockSpec((1, d), lambda i: (0, 0)),
                  pl.BlockSpec((None, 1, d), lambda i: (i // per, 0, 0)),
                  pl.BlockSpec((None, 1, d), lambda i: (i // per, 0, 0))],
        out_specs=pl.BlockSpec((tm, d), lambda i: (i, 0)),
        out_shape=jax.ShapeDtypeStruct((t, d), BF16),
        compiler_params=_params("parallel"),
        name="normmod",
    )(x, g.reshape(1, d), sc, sh)


def _mm_kernel(*refs, n_extra, epilogue, split):
    a_ref, b_ref = refs[0], refs[1]
    extras = refs[2:2 + n_extra]
    o_ref = refs[2 + n_extra]
    a = a_ref[...]
    n_pieces = b_ref.shape[1] // split
    acc = _dot(a, b_ref[:, :split])
    for c in range(n_pieces):
        nxt = _dot(a, b_ref[:, (c + 1) * split:(c + 2) * split]) if c + 1 < n_pieces else None
        epilogue(acc, extras, o_ref, slice(c * split, (c + 1) * split))
        acc = nxt


def _ep_store(acc, extras, o_ref, cols):
    o_ref[:, cols] = acc.astype(o_ref.dtype)


def _matmul(a, b, *, tm, tn, out_dtype, epilogue=_ep_store, extras=(), split=None, name):
    m, kdim = a.shape
    n = b.shape[1]
    split = tn if split is None else split
    assert m % tm == 0 and n % tn == 0 and tn % split == 0
    in_specs = [pl.BlockSpec((tm, kdim), lambda i, j: (i, 0)),
                pl.BlockSpec((kdim, tn), lambda i, j: (0, j))]
    in_specs += [pl.BlockSpec(bs, im) for (_, bs, im) in extras]
    return pl.pallas_call(
        functools.partial(_mm_kernel, n_extra=len(extras), epilogue=epilogue, split=split),
        grid=(m // tm, n // tn),
        in_specs=in_specs,
        out_specs=pl.BlockSpec((tm, tn), lambda i, j: (i, j)),
        out_shape=jax.ShapeDtypeStruct((m, n), out_dtype),
        compiler_params=_params("parallel", "parallel"),
        name=name,
    )(a, b, *[e[0] for e in extras])


def _head_rms(blk, gain):
    ms = jnp.mean(blk * blk, axis=-1, keepdims=True)
    return blk * lax.rsqrt(ms + RMS_EPS) * gain


def _head_slices(cols):
    return [slice(c, c + HEAD_DIM) for c in range(cols.start, cols.stop, HEAD_DIM)]


def _ep_headnorm(acc, extras, o_ref, cols):
    for i, sl in enumerate(_head_slices(cols)):
        blk = acc[:, i * HEAD_DIM:(i + 1) * HEAD_DIM]
        o_ref[:, sl] = _head_rms(blk, extras[0][:, sl]).astype(o_ref.dtype)


def _rotate_pairs(y, cos_t, sin_t, half):
    lane = lax.broadcasted_iota(jnp.int32, y.shape, 1)
    swapped = jnp.where(lane < half, pltpu.roll(y, LANES - half, 1), pltpu.roll(y, half, 1))
    return y * cos_t + swapped * sin_t


def _pair_swap_matrix(width, half):
    p = np.zeros((width, width), np.float32)
    for base in range(0, width, HEAD_DIM):
        for i in range(half):
            p[base + i + half, base + i] = 1.0
            p[base + i, base + i + half] = 1.0
    return jnp.asarray(p, BF16)


def _ep_headnorm_rope(acc, extras, o_ref, cols):
    cos_t = extras[1][...]
    sin_t = extras[2][...]
    heads = _head_slices(cols)
    y = jnp.concatenate([_head_rms(acc[:, i * HEAD_DIM:(i + 1) * HEAD_DIM], extras[0][:, sl])
                         for i, sl in enumerate(heads)], axis=1)
    swapped = _dot(y.astype(BF16), extras[3][...])
    for i, sl in enumerate(heads):
        loc = slice(i * HEAD_DIM, (i + 1) * HEAD_DIM)
        o_ref[:, sl] = (y[:, loc] * cos_t + swapped[:, loc] * sin_t).astype(o_ref.dtype)


def _ep_rownorm(acc, extras, o_ref, cols):
    o_ref[:, cols] = _head_rms(acc, extras[0][:, cols]).astype(o_ref.dtype)


def _ep_sigmoid(acc, extras, o_ref, cols):
    o_ref[:, cols] = jax.nn.sigmoid(acc).astype(o_ref.dtype)


def _ep_residual(acc, extras, o_ref, cols):
    o_ref[:, cols] = extras[0][:, cols] + extras[1][:, cols] * acc


def _rope_tables(seq, rot, fill):
    half = rot // 2
    inv = ROPE_THETA ** (-jnp.arange(half, dtype=F32) * (2.0 / rot))
    ang = jnp.arange(seq, dtype=jnp.int32).astype(F32)[:, None] * inv[None, :]
    cos, sin = jnp.cos(ang), jnp.sin(ang)
    pad = LANES - rot
    cos_t = jnp.concatenate([cos, cos, jnp.full((seq, pad), fill, F32)], axis=1)
    sin_t = jnp.concatenate([-sin, sin, jnp.zeros((seq, pad), F32)], axis=1)
    return cos_t, sin_t


NA_RB = 4
NA_KROWS = NA_KH + NA_RB
NA_GROUP = 8


def _na_geometry(rows):
    kh = min(NA_KH, rows)
    assert rows % (NA_RB * NA_GROUP) == 0 and rows >= NA_KROWS
    patterns, ids, kstarts = [], [], []
    for r in range(0, rows, NA_RB):
        ks = int(np.clip(r - NA_KH // 2, 0, rows - NA_KROWS))
        pat = tuple((int(np.clip(r + a - NA_KH // 2, 0, rows - kh)) - ks, ks - (r + a)) for a in range(NA_RB))
        if pat not in patterns:
            patterns.append(pat)
        ids.append(patterns.index(pat))
        kstarts.append(ks)
    return patterns, np.array(ids + kstarts, np.int32)


def _na_bias_table(rpb, rows, patterns):
    h = rpb.shape[0]
    kh = min(NA_KH, rows)
    qc = np.arange(GRID_W)
    kc = np.arange(GRID_W)
    cstart = np.clip(qc - NA_KW // 2, 0, GRID_W - NA_KW)
    ok = (kc[None, :] >= cstart[:, None]) & (kc[None, :] < cstart[:, None] + NA_KW)
    dc = np.clip(kc[None, :] - qc[:, None] + NA_KW - 1, 0, 2 * NA_KW - 2)
    by_col = rpb.astype(F32)[:, :, dc] * LOG2E + jnp.where(ok, 0.0, NEG_INF).astype(F32)
    masked_row = 2 * NA_KH - 1
    by_col = jnp.concatenate([by_col, jnp.full((h, 1, GRID_W, GRID_W), NEG_INF, F32)], axis=1)
    kr = np.arange(NA_KROWS)
    dr = np.zeros((len(patterns), NA_RB, NA_KROWS), np.int32)
    for p, pat in enumerate(patterns):
        for a, (start, offset) in enumerate(pat):
            row_ok = (kr >= start) & (kr < start + kh)
            dr[p, a] = np.where(row_ok, kr + offset + NA_KH - 1, masked_row)
    tab = jnp.take(by_col, jnp.asarray(dr.reshape(-1)), axis=1)
    tab = tab.reshape(h, len(patterns), NA_RB, NA_KROWS, GRID_W, GRID_W)
    tab = tab.transpose(0, 1, 2, 4, 3, 5)
    return tab.reshape(h, len(patterns), NA_RB * GRID_W, NA_KROWS * GRID_W)


def _na_kernel(geo_ref, q_ref, k_ref, v_ref, b_ref, o_ref, *, n_blocks):
    qn = NA_RB * GRID_W
    kn = NA_KROWS * GRID_W
    ones = jnp.ones((kn, HEAD_DIM), BF16)

    def scores(blk):
        q0 = pl.multiple_of(blk * qn, qn)
        k0 = pl.multiple_of(geo_ref[n_blocks + blk] * GRID_W, GRID_W)
        s = _dot_nt(q_ref[pl.ds(q0, qn), :], k_ref[pl.ds(k0, kn), :]) + b_ref[geo_ref[blk]]
        return s, q0, k0

    def finish(s, q0, k0):
        v1 = jnp.concatenate([v_ref[pl.ds(k0, kn), :], ones], axis=1)
        m = jnp.max(s, axis=-1, keepdims=True)
        acc = _dot(jnp.exp2(s - m).astype(BF16), v1)
        o_ref[pl.ds(q0, qn), :] = (acc[:, :HEAD_DIM] / acc[:, HEAD_DIM:]).astype(o_ref.dtype)

    def body(i, carry):
        pending = [scores(i * NA_GROUP + j) for j in range(NA_GROUP)]
        for item in pending:
            finish(*item)
        return carry

    lax.fori_loop(0, n_blocks // NA_GROUP, body, 0)


def _na_attention(qk, v, rpb, batch, seq):
    rows = seq // GRID_W
    patterns, geo = _na_geometry(rows)
    bias = _na_bias_table(rpb, rows, patterns)
    t = qk.shape[0]
    blk = (seq, HEAD_DIM)
    return pl.pallas_call(
        functools.partial(_na_kernel, n_blocks=rows // NA_RB),
        grid_spec=pltpu.PrefetchScalarGridSpec(
            num_scalar_prefetch=1,
            grid=(batch, NA_HEADS),
            in_specs=[pl.BlockSpec(blk, lambda b, h, geo_ref: (b, h)),
                      pl.BlockSpec(blk, lambda b, h, geo_ref: (b, NA_HEADS + h)),
                      pl.BlockSpec(blk, lambda b, h, geo_ref: (b, h)),
                      pl.BlockSpec((None,) + bias.shape[1:], lambda b, h, geo_ref: (h, 0, 0, 0))],
            out_specs=pl.BlockSpec(blk, lambda b, h, geo_ref: (b, h))),
        out_shape=jax.ShapeDtypeStruct((t, NA_W), BF16),
        compiler_params=_params("parallel", "parallel"),
        name="na_attention",
    )(jnp.asarray(geo), qk, qk, v, bias)


DIL_QCHUNK = 2 * DIL_RADIUS
DIL_KWIN = 4 * DIL_RADIUS
DIL_GROUP = 16


def _dil_window_mask():
    r = np.arange(DIL_QCHUNK)[None, :, None]
    c = np.arange(DIL_KWIN)[None, None, :]
    off = (np.arange(3) * DIL_RADIUS)[:, None, None]
    return jnp.asarray(np.where(np.abs(c - r - off) <= DIL_RADIUS, 0.0, NEG_INF), F32)


def _dil_kernel(q_ref, k_ref, v_ref, mask_ref, o_ref, lse_ref, *, seq, dil):
    length = seq // dil
    n_chunk = length // DIL_QCHUNK
    ones = jnp.ones((DIL_KWIN, HEAD_DIM), BF16)

    def rows(residue, first, count):
        return pl.ds(residue + first * dil, count, stride=dil)

    def scores(it):
        residue = it // n_chunk
        q0 = (it % n_chunk) * DIL_QCHUNK
        k0 = jnp.clip(q0 - DIL_RADIUS, 0, length - DIL_KWIN)
        q = q_ref[rows(residue, q0, DIL_QCHUNK), :].astype(BF16)
        k = k_ref[rows(residue, k0, DIL_KWIN), :].astype(BF16)
        return _dot_nt(q, k) + mask_ref[(q0 - k0) // DIL_RADIUS], residue, q0, k0

    def finish(s, residue, q0, k0):
        v1 = jnp.concatenate([v_ref[rows(residue, k0, DIL_KWIN), :].astype(BF16), ones], axis=1)
        m = jnp.max(s, axis=-1, keepdims=True)
        acc = _dot(jnp.exp2(s - m).astype(BF16), v1)
        den = acc[:, HEAD_DIM:]
        o_ref[rows(residue, q0, DIL_QCHUNK), :] = acc[:, :HEAD_DIM] / den
        lse_ref[rows(residue, q0, DIL_QCHUNK), :] = m * LN2 + jnp.log(den[:, :1])

    def body(i, carry):
        pending = [scores(i * DIL_GROUP + j) for j in range(DIL_GROUP)]
        for item in pending:
            finish(*item)
        return carry

    assert (dil * n_chunk) % DIL_GROUP == 0
    lax.fori_loop(0, dil * n_chunk // DIL_GROUP, body, 0)


def _dil_group(qk, v, group, dil, batch, seq):
    length = seq // dil
    assert length % DIL_QCHUNK == 0 and length >= DIL_KWIN
    hg = DIL_HEADS_PER_GROUP
    blk = (seq, HEAD_DIM)
    mask = _dil_window_mask()
    return pl.pallas_call(
        functools.partial(_dil_kernel, seq=seq, dil=dil),
        grid=(batch, hg),
        in_specs=[pl.BlockSpec(blk, lambda b, h: (b, group * hg + h)),
                  pl.BlockSpec(blk, lambda b, h: (b, DIL_HEADS + group * hg + h)),
                  pl.BlockSpec(blk, lambda b, h: (b, group * hg + h)),
                  pl.BlockSpec(mask.shape, lambda b, h: (0, 0, 0))],
        out_specs=[pl.BlockSpec(blk, lambda b, h: (b, h)),
                   pl.BlockSpec((None, seq, 1), lambda b, h: (b * hg + h, 0, 0))],
        out_shape=[jax.ShapeDtypeStruct((batch * seq, DIL_OUT), F32),
                   jax.ShapeDtypeStruct((batch * hg, seq, 1), F32)],
        compiler_params=_params("parallel", "parallel"),
        name="dilated_attention",
    )(qk, qk, v, mask)


def _dil_mix_kernel(o0_ref, o1_ref, o2_ref, l0_ref, l1_ref, l2_ref, o_ref):
    l0, l1, l2 = l0_ref[...], l1_ref[...], l2_ref[...]
    mx = jnp.maximum(jnp.maximum(l0, l1), l2)
    e0, e1, e2 = jnp.exp(l0 - mx), jnp.exp(l1 - mx), jnp.exp(l2 - mx)
    z = e0 + e1 + e2
    w0, w1, w2 = e0 / z, e1 / z, e2 / z
    for h in range(DIL_HEADS_PER_GROUP):
        sl = slice(h * HEAD_DIM, (h + 1) * HEAD_DIM)
        hs = slice(h, h + 1)
        o_ref[:, sl] = (w0[:, hs] * o0_ref[:, sl] + w1[:, hs] * o1_ref[:, sl]
                        + w2[:, hs] * o2_ref[:, sl]).astype(o_ref.dtype)


def _dil_mix(outs, lses, tm=512):
    t = outs[0].shape[0]
    ospec = pl.BlockSpec((tm, DIL_OUT), lambda i: (i, 0))
    lspec = pl.BlockSpec((tm, DIL_HEADS_PER_GROUP), lambda i: (i, 0))
    return pl.pallas_call(
        _dil_mix_kernel,
        grid=(t // tm,),
        in_specs=[ospec] * 3 + [lspec] * 3,
        out_specs=ospec,
        out_shape=jax.ShapeDtypeStruct((t, DIL_OUT), BF16),
        compiler_params=_params("parallel"),
        name="dilated_mix",
    )(*outs, *lses)


def _dilated_attention(qk, v, batch, seq):
    hg = DIL_HEADS_PER_GROUP
    outs, lses = [], []
    for g, (window, dil) in enumerate(DIL_GROUPS):
        assert window // (2 * dil) == DIL_RADIUS and seq % dil == 0
        o, lse = _dil_group(qk, v, g, dil, batch, seq)
        outs.append(o)
        lses.append(lse.reshape(batch, hg, seq).transpose(0, 2, 1).reshape(batch * seq, hg))
    return _dil_mix(outs, lses)


def _mla_q_kernel(a_ref, w_ref, g_ref, cos_ref, sin_ref, o_ref):
    a = a_ref[...]
    gain = g_ref[...]
    cos_t, sin_t = cos_ref[...], sin_ref[...]
    acc = _dot(a, w_ref[0])
    for h in range(MLA_HEADS):
        nxt = _dot(a, w_ref[h + 1]) if h + 1 < MLA_HEADS else None
        ms = jnp.sum(acc * acc, axis=-1, keepdims=True) * (1.0 / MLA_QK)
        y = acc * lax.rsqrt(ms + RMS_EPS) * gain
        o_ref[h, :, :MLA_NOPE] = y[:, :MLA_NOPE].astype(o_ref.dtype)
        o_ref[h, :, MLA_NOPE:] = _rotate_pairs(y[:, MLA_NOPE:], cos_t, sin_t,
                                               MLA_ROPE // 2).astype(o_ref.dtype)
        acc = nxt


def _mla_q_proj(cq_n, w_uq_h, gain, cos_t, sin_t, seq, tm=512):
    t, rank = cq_n.shape
    per = seq // tm
    return pl.pallas_call(
        _mla_q_kernel,
        grid=(t // tm,),
        in_specs=[pl.BlockSpec((tm, rank), lambda i: (i, 0)),
                  pl.BlockSpec((MLA_HEADS, rank, MLA_QK_PAD), lambda i: (0, 0, 0)),
                  pl.BlockSpec((1, MLA_QK_PAD), lambda i: (0, 0)),
                  pl.BlockSpec((tm, LANES), lambda i: (i % per, 0)),
                  pl.BlockSpec((tm, LANES), lambda i: (i % per, 0))],
        out_specs=pl.BlockSpec((MLA_HEADS, tm, MLA_QK_PAD), lambda i: (0, i, 0)),
        out_shape=jax.ShapeDtypeStruct((MLA_HEADS, t, MLA_QK_PAD), BF16),
        compiler_params=_params("parallel"),
        name="mla_q_proj",
    )(cq_n, w_uq_h, gain, cos_t, sin_t)


def _mla_kv_kernel(a_ref, w_ref, kr_ref, g0_ref, g1_ref, cos_ref, sin_ref, k_ref, v_ref):
    a = a_ref[...]
    kr = kr_ref[...]
    kr_ss = jnp.sum(kr * kr, axis=-1, keepdims=True)
    g0, g1 = g0_ref[...], g1_ref[...]
    cos_t, sin_t = cos_ref[...], sin_ref[...]
    ones = jnp.ones((a.shape[0], MLA_V), v_ref.dtype)
    kr_rot = _rotate_pairs(kr * g1, cos_t, sin_t, MLA_ROPE // 2)
    acc = _dot(a, w_ref[0])
    for h in range(MLA_HEADS):
        nxt = _dot(a, w_ref[h + 1]) if h + 1 < MLA_HEADS else None
        kn = acc[:, :MLA_NOPE]
        ms = (jnp.sum(kn * kn, axis=-1, keepdims=True) + kr_ss) * (1.0 / MLA_QK)
        inv = lax.rsqrt(ms + RMS_EPS)
        k_ref[h, :, :MLA_NOPE] = (kn * inv * g0).astype(k_ref.dtype)
        k_ref[h, :, MLA_NOPE:] = (kr_rot * inv).astype(k_ref.dtype)
        v_ref[h, :, :MLA_V] = acc[:, MLA_NOPE:].astype(v_ref.dtype)
        v_ref[h, :, MLA_V:] = ones
        acc = nxt


def _mla_kv_proj(ckv_n, w_ukv_h, k_r, g0, g1, cos_t, sin_t, seq, tm=1024):
    t, rank = ckv_n.shape
    per = seq // tm
    return pl.pallas_call(
        _mla_kv_kernel,
        grid=(t // tm,),
        in_specs=[pl.BlockSpec((tm, rank), lambda i: (i, 0)),
                  pl.BlockSpec((MLA_HEADS, rank, MLA_NOPE + MLA_V), lambda i: (0, 0, 0)),
                  pl.BlockSpec((tm, LANES), lambda i: (i, 0)),
                  pl.BlockSpec((1, LANES), lambda i: (0, 0)),
                  pl.BlockSpec((1, LANES), lambda i: (0, 0)),
                  pl.BlockSpec((tm, LANES), lambda i: (i % per, 0)),
                  pl.BlockSpec((tm, LANES), lambda i: (i % per, 0))],
        out_specs=[pl.BlockSpec((MLA_HEADS, tm, MLA_QK_PAD), lambda i: (0, i, 0)),
                   pl.BlockSpec((MLA_HEADS, tm, 2 * MLA_V), lambda i: (0, i, 0))],
        out_shape=[jax.ShapeDtypeStruct((MLA_HEADS, t, MLA_QK_PAD), BF16),
                   jax.ShapeDtypeStruct((MLA_HEADS, t, 2 * MLA_V), BF16)],
        compiler_params=_params("parallel"),
        name="mla_kv_proj",
    )(ckv_n, w_ukv_h, k_r, g0, g1, cos_t, sin_t)


MLA_KV_CHUNKS = 8
MLA_QK_AHEAD = 1


def _mla_attn_kernel(q_ref, k_ref, v_ref, o_ref):
    q = q_ref[...]
    tk = k_ref.shape[0] // MLA_KV_CHUNKS
    m = acc = None
    ahead = [_dot_nt(q, k_ref[i * tk:(i + 1) * tk, :]) for i in range(MLA_QK_AHEAD)]
    for c in range(MLA_KV_CHUNKS):
        s = ahead.pop(0)
        nxt = c + MLA_QK_AHEAD
        if nxt < MLA_KV_CHUNKS:
            ahead.append(_dot_nt(q, k_ref[nxt * tk:(nxt + 1) * tk, :]))
        m_c = jnp.max(s, axis=-1, keepdims=True)
        if c == 0:
            m = m_c
            acc = _dot(jnp.exp2(s - m).astype(BF16), v_ref[:tk, :])
        else:
            m_new = jnp.maximum(m, m_c)
            acc = jnp.exp2(m - m_new) * acc + _dot(jnp.exp2(s - m_new).astype(BF16),
                                                   v_ref[c * tk:(c + 1) * tk, :])
            m = m_new
    o_ref[...] = (acc[:, :MLA_V] / acc[:, MLA_V:]).astype(o_ref.dtype)


def _mla_attention(q, k, v, batch, seq, tq=1024):
    t = q.shape[1]
    nq = seq // tq
    return pl.pallas_call(
        _mla_attn_kernel,
        grid=(batch, MLA_HEADS, nq),
        in_specs=[pl.BlockSpec((None, tq, MLA_QK_PAD), lambda b, h, i: (h, b * nq + i, 0)),
                  pl.BlockSpec((None, seq, MLA_QK_PAD), lambda b, h, i: (h, b, 0)),
                  pl.BlockSpec((None, seq, 2 * MLA_V), lambda b, h, i: (h, b, 0))],
        out_specs=pl.BlockSpec((tq, MLA_V), lambda b, h, i: (b * nq + i, h)),
        out_shape=jax.ShapeDtypeStruct((t, MLA_HEADS * MLA_V), BF16),
        compiler_params=_params("parallel", "parallel", "arbitrary"),
        name="mla_attention",
    )(q, k, v)


def _merge_kernel(ona_ref, odl_ref, omla_ref, wna_ref, wdl_ref, wmla_ref, g0_ref, g1_ref, g2_ref, o_ref):
    ona, odl, omla = ona_ref[...], odl_ref[...], omla_ref[...]
    for c in range(o_ref.shape[1] // MXU_COLS):
        cols = slice(c * MXU_COLS, (c + 1) * MXU_COLS)
        acc = g0_ref[:, cols].astype(F32) * _dot(ona, wna_ref[:, cols])
        acc = acc + g1_ref[:, cols].astype(F32) * _dot(odl, wdl_ref[:, cols])
        acc = acc + g2_ref[:, cols].astype(F32) * _dot(omla, wmla_ref[:, cols])
        o_ref[:, cols] = acc.astype(o_ref.dtype)


def _merge(o_na, o_dl, o_mla, w_na, w_dl, w_mla, gates, tm=1024, tn=512):
    t = o_na.shape[0]
    d = w_na.shape[1]
    nj = d // tn

    def act(a):
        return pl.BlockSpec((tm, a.shape[1]), lambda i, j: (i, 0))

    def wgt(w):
        return pl.BlockSpec((w.shape[0], tn), lambda i, j: (0, j))

    def gate(idx):
        return pl.BlockSpec((tm, tn), lambda i, j: (i, idx * nj + j))

    return pl.pallas_call(
        _merge_kernel,
        grid=(t // tm, nj),
        in_specs=[act(o_na), act(o_dl), act(o_mla), wgt(w_na), wgt(w_dl), wgt(w_mla),
                  gate(0), gate(1), gate(2)],
        out_specs=pl.BlockSpec((tm, tn), lambda i, j: (i, j)),
        out_shape=jax.ShapeDtypeStruct((t, d), BF16),
        compiler_params=_params("parallel", "parallel"),
        name="branch_merge",
    )(o_na, o_dl, o_mla, w_na, w_dl, w_mla, gates, gates, gates)


def _swiglu_pieces(h, wg_ref, wu_ref, o_ref, rows=slice(None)):
    for c in range(o_ref.shape[1] // MXU_COLS):
        cols = slice(c * MXU_COLS, (c + 1) * MXU_COLS)
        g = _dot(h, wg_ref[:, cols].astype(BF16))
        u = _dot(h, wu_ref[:, cols].astype(BF16))
        o_ref[rows, cols] = (g * jax.nn.sigmoid(g) * u).astype(o_ref.dtype)


def _gu_kernel(h_ref, wg_ref, wu_ref, o_ref):
    _swiglu_pieces(h_ref[...], wg_ref, wu_ref, o_ref)


def _swiglu_up(h, w_gu, tm=1024, tn=512):
    t, d = h.shape
    ff = w_gu.shape[1] // 2
    nj = ff // tn
    return pl.pallas_call(
        _gu_kernel,
        grid=(t // tm, nj),
        in_specs=[pl.BlockSpec((tm, d), lambda i, j: (i, 0)),
                  pl.BlockSpec((d, tn), lambda i, j: (0, j)),
                  pl.BlockSpec((d, tn), lambda i, j: (0, nj + j))],
        out_specs=pl.BlockSpec((tm, tn), lambda i, j: (i, j)),
        out_shape=jax.ShapeDtypeStruct((t, ff), BF16),
        compiler_params=_params("parallel", "parallel"),
        name="swiglu_up",
    )(h, w_gu, w_gu)


TOP_K = 2
MOE_TILE_ROWS = 1024


def _moe_routing(route, tokens, tm):
    e_flat = jnp.concatenate([route[:, N_EXPERTS], route[:, N_EXPERTS + 1]]).astype(jnp.int32)
    onehot = (e_flat[:, None] == jnp.arange(N_EXPERTS, dtype=jnp.int32)[None, :]).astype(jnp.int32)
    csum = jnp.cumsum(onehot, axis=0)
    rank = jnp.sum((csum - onehot) * onehot, axis=1)
    padded = ((csum[-1] + tm - 1) // tm) * tm
    ends = jnp.cumsum(padded)
    slot = jnp.sum(onehot * (ends - padded)[None, :], axis=1) + rank
    n_tiles = (TOP_K * tokens) // tm + N_EXPERTS
    tile_start = jnp.arange(n_tiles, dtype=jnp.int32) * tm
    tile_expert = jnp.minimum(jnp.sum(tile_start[:, None] >= ends[None, :], axis=1), N_EXPERTS - 1)
    group_end = jnp.sum(jnp.where(jnp.arange(N_EXPERTS)[None, :] == tile_expert[:, None],
                                  (ends - padded + csum[-1])[None, :], 0), axis=1)
    tile_valid = jnp.clip(group_end - tile_start, 0, tm)
    te = jnp.concatenate([tile_expert, ends[-1:] // tm, tile_valid]).astype(jnp.int32)
    return slot.astype(jnp.int32), te


def _dispatch_kernel(slot_ref, x_ref, g_ref, sc_ref, sh_ref, dst_in_ref, dst_ref, h_ref, sem, *, rows, tokens):
    del dst_in_ref
    base = pl.program_id(0) * rows
    h_ref[...] = _modulated_norm(x_ref[...], g_ref[...], sc_ref[...], sh_ref[...])

    def row_copy(r, choice):
        slot = slot_ref[choice * tokens + base + r]
        return pltpu.make_async_copy(h_ref.at[pl.ds(r, 1), :], dst_ref.at[pl.ds(slot, 1), :], sem)

    def start(r, carry):
        row_copy(r, 0).start()
        row_copy(r, 1).start(priority=1)
        return carry

    def wait(r, carry):
        row_copy(r, 0).wait()
        row_copy(r, 1).wait()
        return carry

    lax.fori_loop(0, rows, start, 0, unroll=8)
    lax.fori_loop(0, rows, wait, 0, unroll=8)


def _moe_dispatch(x, g, sc, sh, slot, seq, n_rows, rows=256):
    t, d = x.shape
    per = seq // rows
    return pl.pallas_call(
        functools.partial(_dispatch_kernel, rows=rows, tokens=t),
        grid_spec=pltpu.PrefetchScalarGridSpec(
            num_scalar_prefetch=1,
            grid=(t // rows,),
            in_specs=[pl.BlockSpec((rows, d), lambda i, s: (i, 0)),
                      pl.BlockSpec((1, d), lambda i, s: (0, 0)),
                      pl.BlockSpec((None, 1, d), lambda i, s: (i // per, 0, 0)),
                      pl.BlockSpec((None, 1, d), lambda i, s: (i // per, 0, 0)),
                      pl.BlockSpec(memory_space=pl.ANY)],
            out_specs=pl.BlockSpec(memory_space=pl.ANY),
            scratch_shapes=[pltpu.VMEM((rows, d), F32), pltpu.SemaphoreType.DMA(())]),
        out_shape=jax.ShapeDtypeStruct((n_rows, d), F32),
        input_output_aliases={5: 0},
        compiler_params=_params("arbitrary"),
        name="moe_dispatch",
    )(slot, x, g.reshape(1, d), sc, sh, jnp.zeros((n_rows, d), F32))


def _moe_up_kernel(te_ref, xs_ref, wg_ref, wu_ref, o_ref, hb_ref, *, n_tiles):
    i = pl.program_id(0)
    used = i < te_ref[n_tiles]
    half = o_ref.shape[0] // 2
    full = te_ref[n_tiles + 1 + i] > half

    @pl.when(used & (pl.program_id(1) == 0))
    def _():
        hb_ref[...] = xs_ref[...].astype(BF16)

    @pl.when(used & full)
    def _():
        _swiglu_pieces(hb_ref[...], wg_ref, wu_ref, o_ref)

    @pl.when(used & jnp.logical_not(full))
    def _():
        _swiglu_pieces(hb_ref[:half, :], wg_ref, wu_ref, o_ref, rows=slice(0, half))
        o_ref[half:, :] = jnp.zeros((o_ref.shape[0] - half, o_ref.shape[1]), o_ref.dtype)

    @pl.when(jnp.logical_not(used))
    def _():
        o_ref[...] = jnp.zeros(o_ref.shape, o_ref.dtype)


def _moe_up(xs, w_gu, te, tn=512):
    n_rows, d = xs.shape
    ff = w_gu.shape[2] // 2
    nj = ff // tn
    n_tiles = (te.shape[0] - 1) // 2
    tm = n_rows // n_tiles

    def last_used(i, te_ref):
        return jnp.minimum(i, te_ref[n_tiles] - 1)

    return pl.pallas_call(
        functools.partial(_moe_up_kernel, n_tiles=n_tiles),
        grid_spec=pltpu.PrefetchScalarGridSpec(
            num_scalar_prefetch=1,
            grid=(n_tiles, nj),
            in_specs=[pl.BlockSpec((tm, d), lambda i, j, te_ref: (last_used(i, te_ref), 0)),
                      pl.BlockSpec((None, d, tn), lambda i, j, te_ref: (te_ref[i], 0, j)),
                      pl.BlockSpec((None, d, tn), lambda i, j, te_ref: (te_ref[i], 0, nj + j))],
            out_specs=pl.BlockSpec((tm, tn), lambda i, j, te_ref: (i, j)),
            scratch_shapes=[pltpu.VMEM((tm, d), BF16)]),
        out_shape=jax.ShapeDtypeStruct((n_rows, ff), BF16),
        compiler_params=_params("parallel", "arbitrary"),
        name="moe_up",
    )(te, xs, w_gu, w_gu)


def _moe_down_kernel(te_ref, a_ref, w_ref, o_ref, *, n_tiles):
    i = pl.program_id(0)
    used = i < te_ref[n_tiles]
    half = o_ref.shape[0] // 2
    full = te_ref[n_tiles + 1 + i] > half

    @pl.when(used & full)
    def _():
        o_ref[...] = _dot(a_ref[...], w_ref[...])

    @pl.when(used & jnp.logical_not(full))
    def _():
        o_ref[:half, :] = _dot(a_ref[:half, :], w_ref[...])
        o_ref[half:, :] = jnp.zeros((o_ref.shape[0] - half, o_ref.shape[1]), o_ref.dtype)

    @pl.when(jnp.logical_not(used))
    def _():
        o_ref[...] = jnp.zeros(o_ref.shape, o_ref.dtype)


def _moe_down(act, w_down, te, tn=256):
    n_rows, ff = act.shape
    d = w_down.shape[2]
    n_tiles = (te.shape[0] - 1) // 2
    tm = n_rows // n_tiles

    def last_used(i, te_ref):
        return jnp.minimum(i, te_ref[n_tiles] - 1)

    return pl.pallas_call(
        functools.partial(_moe_down_kernel, n_tiles=n_tiles),
        grid_spec=pltpu.PrefetchScalarGridSpec(
            num_scalar_prefetch=1,
            grid=(n_tiles, d // tn),
            in_specs=[pl.BlockSpec((tm, ff), lambda i, j, te_ref: (last_used(i, te_ref), 0)),
                      pl.BlockSpec((None, ff, tn), lambda i, j, te_ref: (te_ref[i], 0, j))],
            out_specs=pl.BlockSpec((tm, tn), lambda i, j, te_ref: (i, j))),
        out_shape=jax.ShapeDtypeStruct((n_rows, d), F32),
        compiler_params=_params("parallel", "arbitrary"),
        name="moe_down",
    )(te, act, w_down)


def _combine_kernel(slot_ref, x_ref, gate_ref, w_ref, ys_ref, o_ref, buf_ref, sem, *, rows, tokens):
    base = pl.program_id(0) * rows

    def row_copy(r, choice):
        slot = slot_ref[choice * tokens + base + r]
        return pltpu.make_async_copy(ys_ref.at[pl.ds(slot, 1), :], buf_ref.at[choice, pl.ds(r, 1), :], sem)

    def start(r, carry):
        row_copy(r, 0).start()
        row_copy(r, 1).start(priority=1)
        return carry

    def wait(r, carry):
        row_copy(r, 0).wait()
        row_copy(r, 1).wait()
        return carry

    lax.fori_loop(0, rows, start, 0, unroll=8)
    lax.fori_loop(0, rows, wait, 0, unroll=8)
    w = w_ref[...]
    y = w[:, 0:1] * buf_ref[0] + w[:, 1:2] * buf_ref[1]
    o_ref[...] = x_ref[...] + gate_ref[...] * y


def _moe_combine(x, gate, w12, ys, slot, seq, rows=256):
    t, d = x.shape
    per = seq // rows
    return pl.pallas_call(
        functools.partial(_combine_kernel, rows=rows, tokens=t),
        grid_spec=pltpu.PrefetchScalarGridSpec(
            num_scalar_prefetch=1,
            grid=(t // rows,),
            in_specs=[pl.BlockSpec((rows, d), lambda i, s: (i, 0)),
                      pl.BlockSpec((None, 1, d), lambda i, s: (i // per, 0, 0)),
                      pl.BlockSpec((rows, TOP_K), lambda i, s: (i, 0)),
                      pl.BlockSpec(memory_space=pl.ANY)],
            out_specs=pl.BlockSpec((rows, d), lambda i, s: (i, 0)),
            scratch_shapes=[pltpu.VMEM((TOP_K, rows, d), F32), pltpu.SemaphoreType.DMA(())]),
        out_shape=jax.ShapeDtypeStruct((t, d), F32),
        compiler_params=_params("arbitrary"),
        name="moe_combine",
    )(slot, x, gate, w12, ys)


def _router_kernel(x_ref, g_ref, sc_ref, sh_ref, w_ref, o_ref):
    h = _modulated_norm(x_ref[...], g_ref[...], sc_ref[...], sh_ref[...])
    logits = jnp.dot(h, w_ref[...], preferred_element_type=F32, precision=lax.Precision.HIGHEST)
    lane = lax.broadcasted_iota(jnp.int32, logits.shape, 1).astype(F32)
    lg = jnp.where(lane < N_EXPERTS, logits, NEG_INF)
    m1 = jnp.max(lg, axis=-1, keepdims=True)
    i1 = jnp.min(jnp.where(lg == m1, lane, float(LANES)), axis=-1, keepdims=True)
    lg2 = jnp.where(lane == i1, NEG_INF, lg)
    m2 = jnp.max(lg2, axis=-1, keepdims=True)
    i2 = jnp.min(jnp.where(lg2 == m2, lane, float(LANES)), axis=-1, keepdims=True)
    e2 = jnp.exp(m2 - m1)
    z = 1.0 + e2
    out = jnp.where(lane == N_EXPERTS, i1, 0.0) + jnp.where(lane == N_EXPERTS + 1, i2, 0.0)
    out = out + jnp.where(lane == N_EXPERTS + 2, 1.0 / z, 0.0) + jnp.where(lane == N_EXPERTS + 3, e2 / z, 0.0)
    o_ref[...] = out


def _router(x, g, sc, sh, w_router_pad, seq, tm=512):
    t, d = x.shape
    per = seq // tm
    return pl.pallas_call(
        _router_kernel,
        grid=(t // tm,),
        in_specs=[pl.BlockSpec((tm, d), lambda i: (i, 0)),
                  pl.BlockSpec((1, d), lambda i: (0, 0)),
                  pl.BlockSpec((None, 1, d), lambda i: (i // per, 0, 0)),
                  pl.BlockSpec((None, 1, d), lambda i: (i // per, 0, 0)),
                  pl.BlockSpec((d, LANES), lambda i: (0, 0))],
        out_specs=pl.BlockSpec((tm, LANES), lambda i: (i, 0)),
        out_shape=jax.ShapeDtypeStruct((t, LANES), F32),
        compiler_params=_params("parallel"),
        name="router",
    )(x, g.reshape(1, d), sc, sh, w_router_pad)


MM_TM = 1024
MM_TN = 1024


def _col_tile(n):
    return MM_TN if n % MM_TN == 0 else MM_TN // 2


def _residual_matmul(a, w, x, gate, seq, name, tm=MM_TM, tn=None):
    per = seq // tm
    tn = _col_tile(w.shape[1]) if tn is None else tn
    return _matmul(a, w, tm=tm, tn=tn, out_dtype=F32, epilogue=_ep_residual, split=MXU_COLS,
                   extras=[(x, (tm, tn), lambda i, j: (i, j)),
                           (gate, (None, 1, tn), lambda i, j: (i // per, 0, j))],
                   name=name)


def _token_mixer(h, x, gate, lw, batch, seq, tables):
    d = h.shape[1]
    w_in = lw["w_in"]
    tm = MM_TM
    per = seq // tm
    cos_d, sin_d, cos_m, sin_m = tables

    def cols(lo, hi):
        return w_in[:, lo:hi].astype(BF16)

    def tile_gain(gq, gk, heads):
        gq = gq * (LOG2E / math.sqrt(HEAD_DIM))
        return jnp.concatenate([jnp.tile(gq, heads), jnp.tile(gk, heads)]).reshape(1, -1).astype(F32)

    o = 0
    tn = _col_tile(2 * NA_W)
    na_qk = _matmul(h, cols(o, o + 2 * NA_W), tm=tm, tn=tn, out_dtype=BF16, epilogue=_ep_headnorm,
                    split=MXU_COLS,
                    extras=[(tile_gain(lw["na_q_norm"], lw["na_k_norm"], NA_HEADS), (1, tn),
                             lambda i, j: (0, j))], name="na_qk_proj")
    o += 2 * NA_W
    na_v = _matmul(h, cols(o, o + NA_W), tm=tm, tn=_col_tile(NA_W), out_dtype=BF16, split=MXU_COLS, name="na_v_proj")
    o += NA_W
    tn = _col_tile(2 * DIL_W)
    dl_qk = _matmul(h, cols(o, o + 2 * DIL_W), tm=tm, tn=tn, out_dtype=F32, epilogue=_ep_headnorm_rope,
                    split=MXU_COLS,
                    extras=[(tile_gain(lw["dil_q_norm"], lw["dil_k_norm"], DIL_HEADS), (1, tn),
                             lambda i, j: (0, j)),
                            (cos_d, (tm, LANES), lambda i, j: (i % per, 0)),
                            (sin_d, (tm, LANES), lambda i, j: (i % per, 0)),
                            (_pair_swap_matrix(MXU_COLS, ROT_DIM // 2), (MXU_COLS, MXU_COLS),
                             lambda i, j: (0, 0))], name="dil_qk_proj")
    o += 2 * DIL_W
    dl_v = _matmul(h, cols(o, o + DIL_W), tm=tm, tn=_col_tile(DIL_W), out_dtype=F32, split=MXU_COLS, name="dil_v_proj")
    o += DIL_W
    cq_n = _matmul(h, cols(o, o + MLA_Q_RANK), tm=tm, tn=MLA_Q_RANK, out_dtype=BF16, epilogue=_ep_rownorm,
                   extras=[(lw["mla_q_a_norm"].reshape(1, -1), (1, MLA_Q_RANK), lambda i, j: (0, 0))],
                   name="mla_cq_proj")
    o += MLA_Q_RANK
    ckv_n = _matmul(h, cols(o, o + MLA_KV_RANK), tm=tm, tn=MLA_KV_RANK, out_dtype=BF16, epilogue=_ep_rownorm,
                    extras=[(lw["mla_kv_a_norm"].reshape(1, -1), (1, MLA_KV_RANK), lambda i, j: (0, 0))],
                    name="mla_ckv_proj")
    o += MLA_KV_RANK
    w_kr = jnp.pad(w_in[:, o:o + MLA_ROPE], ((0, 0), (0, LANES - MLA_ROPE))).astype(BF16)
    k_r = _matmul(h, w_kr, tm=tm, tn=LANES, out_dtype=F32, name="mla_kr_proj")
    o += MLA_ROPE
    gates = _matmul(h, cols(o, o + 3 * d), tm=tm, tn=_col_tile(3 * d), out_dtype=BF16, epilogue=_ep_sigmoid,
                    split=MXU_COLS, name="gate_proj")

    o_na = _na_attention(na_qk, na_v, lw["na_rpb"], batch, seq)
    o_dl = _dilated_attention(dl_qk, dl_v, batch, seq)
    w_uq = lw["mla_w_uq"].reshape(MLA_Q_RANK, MLA_HEADS, MLA_QK).transpose(1, 0, 2)
    w_uq = jnp.pad(w_uq, ((0, 0), (0, 0), (0, MLA_QK_PAD - MLA_QK))).astype(BF16)
    w_ukv = lw["mla_w_ukv"].reshape(MLA_KV_RANK, MLA_HEADS, MLA_NOPE + MLA_V).transpose(1, 0, 2).astype(BF16)
    gq = jnp.pad(lw["mla_q_norm"] * (LOG2E / math.sqrt(MLA_QK)), (0, MLA_QK_PAD - MLA_QK)).reshape(1, MLA_QK_PAD)
    gk = lw["mla_k_norm"]
    gk0 = gk[:MLA_NOPE].reshape(1, LANES)
    gk1 = jnp.pad(gk[MLA_NOPE:], (0, LANES - MLA_ROPE)).reshape(1, LANES)
    q_m = _mla_q_proj(cq_n, w_uq, gq, cos_m, sin_m, seq)
    k_m, v_m = _mla_kv_proj(ckv_n, w_ukv, k_r, gk0, gk1, cos_m, sin_m, seq)
    o_mla = _mla_attention(q_m, k_m, v_m, batch, seq)

    merged = _merge(o_na, o_dl, o_mla, lw["w_branch_na"].astype(BF16), lw["w_branch_dil"].astype(BF16),
                    lw["w_branch_mla"].astype(BF16), gates)
    return _residual_matmul(merged, lw["w_out"].astype(BF16), x, gate, seq, "out_proj")


def kernel(x, c, w_ada, b_ada, norm_mix, norm_ffn, w_in, na_q_norm, na_k_norm, na_rpb, dil_q_norm, dil_k_norm, mla_q_a_norm, mla_w_uq, mla_kv_a_norm, mla_w_ukv, mla_q_norm, mla_k_norm, w_branch_na, w_branch_dil, w_branch_mla, w_out, ffn_w_gu, ffn_w_down, moe_router, moe_w_gu, moe_w_down):
    batch, seq, d = x.shape
    depth = w_ada.shape[0]
    assert seq % MM_TM == 0 and seq % GRID_W == 0
    t = batch * seq
    xf = x.reshape(t, d)

    c_pad = jnp.pad(c, ((0, (-batch) % 8), (0, 0)))
    mod = _adaln(c_pad, w_ada, b_ada)[:, :batch].reshape(depth, batch, 6, 1, d)
    tables = _rope_tables(seq, ROT_DIM, 1.0) + _rope_tables(seq, MLA_ROPE, 1.0)

    for l in range(depth):
        sh1, sc1, g1, sh2, sc2, g2 = [mod[l, :, i] for i in range(6)]
        lw = dict(w_in=w_in[l], na_q_norm=na_q_norm[l], na_k_norm=na_k_norm[l], na_rpb=na_rpb[l],
                  dil_q_norm=dil_q_norm[l], dil_k_norm=dil_k_norm[l], mla_q_a_norm=mla_q_a_norm[l],
                  mla_w_uq=mla_w_uq[l], mla_kv_a_norm=mla_kv_a_norm[l], mla_w_ukv=mla_w_ukv[l],
                  mla_q_norm=mla_q_norm[l], mla_k_norm=mla_k_norm[l], w_branch_na=w_branch_na[l],
                  w_branch_dil=w_branch_dil[l], w_branch_mla=w_branch_mla[l], w_out=w_out[l])
        h = _normmod(xf, norm_mix[l], sc1, sh1, seq)
        xf = _token_mixer(h, xf, g1, lw, batch, seq, tables)

        if l % 2 == 0:
            h = _normmod(xf, norm_ffn[l], sc2, sh2, seq)
            act = _swiglu_up(h, ffn_w_gu[l // 2].astype(BF16))
            xf = _residual_matmul(act, ffn_w_down[l // 2].astype(BF16), xf, g2, seq, "down_proj", tm=1024, tn=256)
        else:
            w_r = jnp.pad(moe_router[l // 2], ((0, 0), (0, LANES - N_EXPERTS)))
            route = _router(xf, norm_ffn[l], sc2, sh2, w_r, seq)
            slot, te = _moe_routing(route, t, MOE_TILE_ROWS)
            n_rows = ((te.shape[0] - 1) // 2) * MOE_TILE_ROWS
            xs = _moe_dispatch(xf, norm_ffn[l], sc2, sh2, slot, seq, n_rows)
            act = _moe_up(xs, moe_w_gu[l // 2], te)
            ys = _moe_down(act, moe_w_down[l // 2].astype(BF16), te)
            xf = _moe_combine(xf, g2, route[:, N_EXPERTS + 2:N_EXPERTS + 4], ys, slot, seq)
    return xf.reshape(batch, seq, d)
```
